```python
import jax, jax.numpy as jnp
from jax import lax
import numpy as np

D_MODEL = 1024
BATCH = 1
SEQ = 16384
DEPTH = 1

HEAD_DIM = 64
N_MOBA_HEADS = 6
N_DIL_HEADS = 6
N_MEM_HEADS = 4
N_MEM = 256
MOBA_BLOCK = 256
MOBA_TOPK = 3
MOBA_QCHUNK = 128
DIL_PATTERNS = ((128, 1), (512, 4), (2048, 16))
DIL_BLOCK = 128
D_FF = 2816
CONV_WIDTH = 3
ROPE_THETA = 10000.0
EPS = 1e-6
N_BRANCHES = 3
MOBA_W = N_MOBA_HEADS * HEAD_DIM
DIL_W = N_DIL_HEADS * HEAD_DIM
MEM_W = N_MEM_HEADS * HEAD_DIM
IN_COLS = 3 * MOBA_W + 3 * DIL_W + MEM_W + N_BRANCHES * D_MODEL

kernel_name = "hybrid_moba_dilated_memory_convffn"


def rmsnorm(x, g):
    x32 = x.astype(jnp.float32)
    y = x32 * lax.rsqrt(jnp.mean(x32 * x32, axis=-1, keepdims=True) + EPS)
    return (y * g.astype(jnp.float32)).astype(x.dtype)


def rope(x, positions):
    half = HEAD_DIM // 2
    inv_freq = ROPE_THETA ** (-jnp.arange(half, dtype=jnp.float32) / half)
    ang = positions.astype(jnp.float32)[..., None] * inv_freq
    cos = jnp.cos(ang)[:, :, None, :]
    sin = jnp.sin(ang)[:, :, None, :]
    x1 = x[..., :half].astype(jnp.float32)
    x2 = x[..., half:].astype(jnp.float32)
    out = jnp.concatenate([x1 * cos - x2 * sin, x1 * sin + x2 * cos], axis=-1)
    return out.astype(x.dtype)


def moba_attention(q, k, v):
    B, S, H, Dh = q.shape
    s_pad = -(-S // MOBA_BLOCK) * MOBA_BLOCK
    pad = s_pad - S
    q, k, v = [jnp.pad(t, ((0, 0), (0, pad), (0, 0), (0, 0))).transpose(0, 2, 1, 3) for t in (q, k, v)]
    nb = s_pad // MOBA_BLOCK
    topk = min(MOBA_TOPK, nb)
    kb = k.reshape(B, H, nb, MOBA_BLOCK, Dh)
    vb = v.reshape(B, H, nb, MOBA_BLOCK, Dh)
    k_mean = jnp.mean(kb.astype(jnp.float32), axis=3)
    scale = Dh ** -0.5
    bi = jnp.arange(B)[:, None, None, None]
    hi = jnp.arange(H)[None, :, None, None]
    blk_ids = jnp.arange(nb)

    def one_chunk(c):
        q0 = c * MOBA_QCHUNK
        qc = lax.dynamic_slice_in_dim(q, q0, MOBA_QCHUNK, axis=2)
        qpos = q0 + jnp.arange(MOBA_QCHUNK)
        own = q0 // MOBA_BLOCK
        gate = jnp.einsum('bhqd,bhnd->bhqn', qc.astype(jnp.float32), k_mean)
        gate = jnp.where(blk_ids < own, gate, -jnp.inf)
        gval, sel = lax.top_k(gate, topk)
        sel_ok = jnp.isfinite(gval)
        k_sel = kb[bi, hi, sel]
        v_sel = vb[bi, hi, sel]
        s_sel = jnp.einsum('bhqd,bhqnjd->bhqnj', qc, k_sel).astype(jnp.float32) * scale
        s_sel = jnp.where(sel_ok[..., None], s_sel, -jnp.inf)
        s_sel = s_sel.reshape(B, H, MOBA_QCHUNK, topk * MOBA_BLOCK)
        k_own = lax.dynamic_slice_in_dim(k, own * MOBA_BLOCK, MOBA_BLOCK, axis=2)
        v_own = lax.dynamic_slice_in_dim(v, own * MOBA_BLOCK, MOBA_BLOCK, axis=2)
        kpos = own * MOBA_BLOCK + jnp.arange(MOBA_BLOCK)
        s_own = jnp.einsum('bhqd,bhjd->bhqj', qc, k_own).astype(jnp.float32) * scale
        s_own = jnp.where(kpos[None, :] <= qpos[:, None], s_own, -jnp.inf)
        p = jax.nn.softmax(jnp.concatenate([s_sel, s_own], axis=-1), axis=-1)
        p_sel = p[..., :topk * MOBA_BLOCK].reshape(B, H, MOBA_QCHUNK, topk, MOBA_BLOCK)
        p_own = p[..., topk * MOBA_BLOCK:]
        o = (jnp.einsum('bhqnj,bhqnjd->bhqd', p_sel.astype(v.dtype), v_sel)
             + jnp.einsum('bhqj,bhjd->bhqd', p_own.astype(v.dtype), v_own))
        return o

    outs = lax.map(one_chunk, jnp.arange(s_pad // MOBA_QCHUNK))
    o = outs.transpose(1, 0, 3, 2, 4).reshape(B, s_pad, H, Dh)
    return o[:, :S]


def dilated_attention(q, k, v):
    B, S, H, Dh = q.shape
    max_dil = max(d for _, d in DIL_PATTERNS)
    unit = max_dil * DIL_BLOCK
    s_pad = -(-S // unit) * unit
    pad = s_pad - S
    q, k, v = [jnp.pad(t, ((0, 0), (0, pad), (0, 0), (0, 0))) for t in (q, k, v)]
    scale = Dh ** -0.5
    outs, lses = [], []
    for window, dil in DIL_PATTERNS:
        span = window // dil
        L = s_pad // dil
        nblk = L // DIL_BLOCK

        def to_sub(t):
            return t.reshape(B, L, dil, H, Dh).transpose(0, 2, 1, 3, 4).reshape(B, dil, nblk, DIL_BLOCK, H, Dh)

        def from_sub(t):
            rest = t.shape[4:]
            t = t.reshape((B, dil, L) + rest)
            t = jnp.moveaxis(t, 1, 2)
            return t.reshape((B, s_pad) + rest)

        def with_prev(t):
            prev = jnp.pad(t[:, :, :-1], ((0, 0), (0, 0), (1, 0), (0, 0), (0, 0), (0, 0)))
            return jnp.concatenate([prev, t], axis=3)

        qs = to_sub(q)
        ks = with_prev(to_sub(k))
        vs = with_prev(to_sub(v))
        s = jnp.einsum('brnqhd,brnkhd->brnhqk', qs, ks).astype(jnp.float32) * scale
        qi = jnp.arange(DIL_BLOCK)[:, None] + DIL_BLOCK
        kj = jnp.arange(2 * DIL_BLOCK)[None, :]
        dist = qi - kj
        band = (dist >= 0) & (dist <= span)
        not_before_start = (jnp.arange(nblk) > 0)[:, None, None] | (kj >= DIL_BLOCK)[None]
        mask = band[None] & not_before_start
        s = jnp.where(mask[None, None, :, None], s, -jnp.inf)
        m = jnp.max(s, axis=-1, keepdims=True)
        p = jnp.exp(s - m)
        den = jnp.sum(p, axis=-1)
        o = jnp.einsum('brnhqk,brnkhd->brnqhd', p.astype(v.dtype), vs)
        o = o / den.transpose(0, 1, 2, 4, 3)[..., None]
        lse = (m[..., 0] + jnp.log(den)).transpose(0, 1, 2, 4, 3)
        outs.append(from_sub(o))
        lses.append(from_sub(lse))
    w = jax.nn.softmax(jnp.stack(lses, axis=0), axis=0)
    o = jnp.einsum('pbsh,pbshd->bshd', w, jnp.stack(outs, axis=0).astype(jnp.float32))
    return o[:, :S].astype(q.dtype)


def memory_attention(q, mem_k, mem_v):
    s = jnp.einsum('bshd,bmhd->bhsm', q, mem_k).astype(jnp.float32) * (HEAD_DIM ** -0.5)
    p = jax.nn.softmax(s, axis=-1)
    return jnp.einsum('bhsm,bmhd->bshd', p.astype(mem_v.dtype), mem_v)


def causal_depthwise_conv(u, w, b):
    S = u.shape[1]
    u_pad = jnp.pad(u, ((0, 0), (CONV_WIDTH - 1, 0), (0, 0)))
    y = b
    for j in range(CONV_WIDTH):
        y = y + w[j] * u_pad[:, j:j + S]
    return y


def setup_inputs(seed: int = 0) -> dict:
    key = jax.random.key(seed)
    ks = jax.random.split(key, 24)
    f32 = jnp.float32

    def nrm(k, shape, scale):
        return jax.random.normal(k, shape, f32) * scale

    def gain(k, shape):
        return 1.0 + 0.02 * jax.random.normal(k, shape, f32)

    return {
        "x": nrm(ks[0], (BATCH, SEQ, D_MODEL), 1.0),
        "mem": nrm(ks[1], (BATCH, N_MEM, D_MODEL), 1.0),
        "positions": jnp.broadcast_to(jnp.arange(SEQ, dtype=jnp.int32)[None], (BATCH, SEQ)),
        "mix_norm_g": gain(ks[2], (DEPTH, D_MODEL)),
        "mem_norm_g": gain(ks[3], (DEPTH, D_MODEL)),
        "w_in": nrm(ks[4], (DEPTH, D_MODEL, IN_COLS), D_MODEL ** -0.5),
        "moba_q_norm_g": gain(ks[5], (DEPTH, HEAD_DIM)),
        "moba_k_norm_g": gain(ks[6], (DEPTH, HEAD_DIM)),
        "dil_q_norm_g": gain(ks[7], (DEPTH, HEAD_DIM)),
        "dil_k_norm_g": gain(ks[8], (DEPTH, HEAD_DIM)),
        "mem_q_norm_g": gain(ks[9], (DEPTH, HEAD_DIM)),
        "mem_k_norm_g": gain(ks[10], (DEPTH, HEAD_DIM)),
        "w_mem_kv": nrm(ks[11], (DEPTH, D_MODEL, 2 * MEM_W), D_MODEL ** -0.5),
        "w_branch_moba": nrm(ks[12], (DEPTH, MOBA_W, D_MODEL), MOBA_W ** -0.5),
        "w_branch_dil": nrm(ks[13], (DEPTH, DIL_W, D_MODEL), DIL_W ** -0.5),
        "w_branch_mem": nrm(ks[14], (DEPTH, MEM_W, D_MODEL), MEM_W ** -0.5),
        "w_out": nrm(ks[15], (DEPTH, D_MODEL, D_MODEL), D_MODEL ** -0.5),
        "ffn_norm_g": gain(ks[16], (DEPTH, D_MODEL)),
        "w_ffn_up": nrm(ks[17], (DEPTH, D_MODEL, 2 * D_FF), D_MODEL ** -0.5),
        "ffn_conv_w": nrm(ks[18], (DEPTH, CONV_WIDTH, 2 * D_FF), CONV_WIDTH ** -0.5),
        "ffn_conv_b": nrm(ks[19], (DEPTH, 2 * D_FF), 0.01),
        "w_ffn_down": nrm(ks[20], (DEPTH, D_FF, D_MODEL), D_FF ** -0.5),
    }


def reference(x, mem, positions, mix_norm_g, mem_norm_g, w_in, moba_q_norm_g, moba_k_norm_g,
              dil_q_norm_g, dil_k_norm_g, mem_q_norm_g, mem_k_norm_g, w_mem_kv,
              w_branch_moba, w_branch_dil, w_branch_mem, w_out, ffn_norm_g,
              w_ffn_up, ffn_conv_w, ffn_conv_b, w_ffn_down):
    B, S, _ = x.shape
    split_at = [int(c) for c in np.cumsum([MOBA_W, MOBA_W, MOBA_W, DIL_W, DIL_W, DIL_W, MEM_W])]
    for l in range(DEPTH):
        h = rmsnorm(x, mix_norm_g[l])
        proj = h @ w_in[l]
        qa, ka, va, qd, kd, vd, qm, graw = jnp.split(proj, split_at, axis=-1)
        heads = lambda t, n: t.reshape(B, S, n, HEAD_DIM)
        qa = rope(rmsnorm(heads(qa, N_MOBA_HEADS), moba_q_norm_g[l]), positions)
        ka = rope(rmsnorm(heads(ka, N_MOBA_HEADS), moba_k_norm_g[l]), positions)
        va = heads(va, N_MOBA_HEADS)
        qd = rope(rmsnorm(heads(qd, N_DIL_HEADS), dil_q_norm_g[l]), positions)
        kd = rope(rmsnorm(heads(kd, N_DIL_HEADS), dil_k_norm_g[l]), positions)
        vd = heads(vd, N_DIL_HEADS)
        qm = rmsnorm(heads(qm, N_MEM_HEADS), mem_q_norm_g[l])
        gates = jax.nn.sigmoid(graw.astype(jnp.float32)).astype(x.dtype).reshape(B, S, N_BRANCHES, D_MODEL)

        mem_n = rmsnorm(mem, mem_norm_g[l])
        mkv = (mem_n @ w_mem_kv[l]).reshape(B, N_MEM, 2, N_MEM_HEADS, HEAD_DIM)
        mk = rmsnorm(mkv[:, :, 0], mem_k_norm_g[l])
        mv = mkv[:, :, 1]

        o_a = moba_attention(qa, ka, va).reshape(B, S, MOBA_W) @ w_branch_moba[l]
        o_d = dilated_attention(qd, kd, vd).reshape(B, S, DIL_W) @ w_branch_dil[l]
        o_m = memory_attention(qm, mk, mv).reshape(B, S, MEM_W) @ w_branch_mem[l]
        merged = gates[:, :, 0] * o_a + gates[:, :, 1] * o_d + gates[:, :, 2] * o_m
        x = x + merged @ w_out[l]

        h2 = rmsnorm(x, ffn_norm_g[l])
        u = causal_depthwise_conv(h2 @ w_ffn_up[l], ffn_conv_w[l], ffn_conv_b[l])
        u_gate, u_val = jnp.split(u, 2, axis=-1)
        x = x + (jax.nn.silu(u_gate) * u_val) @ w_ffn_down[l]
    return x
```

```python
import functools

import numpy as np
import jax
import jax.numpy as jnp
from jax import lax
from jax.experimental import pallas as pl
from jax.experimental.pallas import tpu as pltpu

D_MODEL = 1024
HEAD_DIM = 64
HALF = HEAD_DIM // 2
N_MOBA_HEADS = 6
N_DIL_HEADS = 6
N_MEM_HEADS = 4
N_MEM = 256
MOBA_BLOCK = 256
MOBA_TOPK = 3
DIL_PATTERNS = ((128, 1), (512, 4), (2048, 16))
DIL_BLOCK = 128
D_FF = 2816
CONV_WIDTH = 3
ROPE_THETA = 10000.0
EPS = 1e-6
MOBA_W = N_MOBA_HEADS * HEAD_DIM
DIL_W = N_DIL_HEADS * HEAD_DIM
MEM_W = N_MEM_HEADS * HEAD_DIM
QKV_COLS = 3 * MOBA_W + 3 * DIL_W + MEM_W
SCALE = HEAD_DIM ** -0.5

LANES = 128
MXU_TILE = 256
VMEM_LIMIT = 56 * 1024 * 1024

TM_PROJ = 512
TM_MERGE = 256
TM_FFN = 512
FFN_CHUNK = 256
DIL_ROWS = 1024

F32 = jnp.float32
BF16 = jnp.bfloat16
NEG_INF = float("-inf")


def _dot(a, b):
    return jnp.dot(a, b, preferred_element_type=F32)


def _dot_nt(a, b):
    return lax.dot_general(a, b, (((1,), (1,)), ((), ())), preferred_element_type=F32)


def _rms_rows(x, g):
    ms = jnp.mean(x * x, axis=-1, keepdims=True)
    return x * lax.rsqrt(ms + EPS) * g


def _params(*sem):
    return pltpu.CompilerParams(dimension_semantics=sem, vmem_limit_bytes=VMEM_LIMIT)


def _rope_kernel(pos_ref, inv_ref, cos_t_ref, sin_t_ref, cos_n_ref, sin_n_ref):
    pos = pos_ref[...].astype(F32)
    ang = inv_ref[...] * pos
    c = jnp.cos(ang)
    s = jnp.sin(ang)
    cos_t_ref[...] = jnp.concatenate([c, c], axis=0)
    sin_t_ref[...] = jnp.concatenate([-s, s], axis=0)
    cos_n_ref[...] = jnp.concatenate([c, c, c, c], axis=0).T
    sin_n_ref[...] = jnp.concatenate([-s, s, -s, s], axis=0).T


def _rope_tables(positions, S):
    tm = 512
    inv = (ROPE_THETA ** (-jnp.arange(HALF, dtype=F32) / HALF)).reshape(HALF, 1)
    return pl.pallas_call(
        _rope_kernel,
        grid=(S // tm,),
        in_specs=[pl.BlockSpec((1, tm), lambda i: (0, i)),
                  pl.BlockSpec((HALF, 1), lambda i: (0, 0))],
        out_specs=[pl.BlockSpec((HEAD_DIM, tm), lambda i: (0, i)),
                   pl.BlockSpec((HEAD_DIM, tm), lambda i: (0, i)),
                   pl.BlockSpec((tm, LANES), lambda i: (i, 0)),
                   pl.BlockSpec((tm, LANES), lambda i: (i, 0))],
        out_shape=[jax.ShapeDtypeStruct((HEAD_DIM, S), F32),
                   jax.ShapeDtypeStruct((HEAD_DIM, S), F32),
                   jax.ShapeDtypeStruct((S, LANES), F32),
                   jax.ShapeDtypeStruct((S, LANES), F32)],
        compiler_params=_params("parallel"),
        name="rope_tables",
    )(positions.reshape(1, S), inv)


def _memkv_kernel(mem_ref, g_ref, w_ref, gk_ref, gmat_ref, mk_ref, mv_ref):
    h = _rms_rows(mem_ref[...], g_ref[...]).astype(BF16)
    kv = _dot(h, w_ref[...])
    k = kv[:, :MEM_W]
    ms = _dot((k * k).astype(BF16), gmat_ref[...])
    mk_ref[...] = (k * lax.rsqrt(ms + EPS) * gk_ref[...]).astype(BF16)
    mv_ref[...] = kv[:, MEM_W:].astype(BF16)


def _mem_kv(mem, g, w_kv, gk_tiled, gmat):
    full = lambda shape: pl.BlockSpec(shape, lambda i: (0,) * len(shape))
    return pl.pallas_call(
        _memkv_kernel,
        grid=(1,),
        in_specs=[full((N_MEM, D_MODEL)), full((1, D_MODEL)), full((D_MODEL, 2 * MEM_W)),
                  full((1, MEM_W)), full((MXU_TILE, MXU_TILE))],
        out_specs=[full((N_MEM, MEM_W)), full((N_MEM, MEM_W))],
        out_shape=[jax.ShapeDtypeStruct((N_MEM, MEM_W), BF16)] * 2,
        compiler_params=_params("arbitrary"),
        name="mem_kv",
    )(mem, g, w_kv, gk_tiled, gmat)


_NAT_NORM = 3 * 384 + MEM_W
_NAT_NORM_PAD = 1536
_NAT_ROPE = 3 * 384
_NAT_COLS = _NAT_NORM + DIL_W


def _proj_kernel(x_ref, g_ref, wt_ref, wn_ref, gq_ref, gnat_ref, gmat_ref,
                 cos_t_ref, sin_t_ref, cos_n_ref, sin_n_ref,
                 qat_ref, vat_ref, ka_ref, km_ref, qd_ref, kd_ref, vd_ref, qm_ref):
    tm = x_ref.shape[0]
    h = _rms_rows(x_ref[...], g_ref[...]).astype(BF16)

    yt = _dot_nt(wt_ref[...], h)
    cos_t = cos_t_ref[...]
    sin_t = sin_t_ref[...]
    gq = gq_ref[...]
    for hd in range(N_MOBA_HEADS):
        q = yt[hd * HEAD_DIM:(hd + 1) * HEAD_DIM]
        ms = jnp.mean(q * q, axis=0, keepdims=True)
        q = q * lax.rsqrt(ms + EPS) * gq
        rot = jnp.concatenate([q[HALF:], q[:HALF]], axis=0)
        q = (q * cos_t + rot * sin_t) * SCALE
        for b in range(tm // MOBA_BLOCK):
            qat_ref[b, hd * HEAD_DIM:(hd + 1) * HEAD_DIM, :] = (
                q[:, b * MOBA_BLOCK:(b + 1) * MOBA_BLOCK].astype(BF16))
    for b in range(tm // MOBA_BLOCK):
        vat_ref[b] = yt[MOBA_W:, b * MOBA_BLOCK:(b + 1) * MOBA_BLOCK].astype(BF16)

    y = _dot(h, wn_ref[...])
    gmat = gmat_ref[...]
    lane = lax.broadcasted_iota(jnp.int32, (tm, LANES), 1)
    first_half = (lane % HEAD_DIM) < HALF
    cos_n = cos_n_ref[...]
    sin_n = sin_n_ref[...]
    chunks = []
    for c in range(_NAT_NORM_PAD // MXU_TILE):
        blk = y[:, c * MXU_TILE:(c + 1) * MXU_TILE]
        ms = _dot((blk * blk).astype(BF16), gmat)
        blk = blk * lax.rsqrt(ms + EPS) * gnat_ref[:, c * MXU_TILE:(c + 1) * MXU_TILE]
        for hc in range(MXU_TILE // LANES):
            col = c * MXU_TILE + hc * LANES
            if col >= _NAT_NORM:
                break
            v = blk[:, hc * LANES:(hc + 1) * LANES]
            if col < _NAT_ROPE:
                rot = jnp.where(first_half, pltpu.roll(v, LANES - HALF, axis=1),
                                pltpu.roll(v, HALF, axis=1))
                v = v * cos_n + rot * sin_n
            chunks.append(v)
    pairs = DIL_W // LANES
    ka = jnp.concatenate(chunks[0:pairs], axis=1)
    for b in range(tm // MOBA_BLOCK):
        km_ref[b] = jnp.mean(ka[b * MOBA_BLOCK:(b + 1) * MOBA_BLOCK], axis=0, keepdims=True)
    ka_ref[...] = ka.astype(BF16)
    qd_ref[...] = (jnp.concatenate(chunks[pairs:2 * pairs], axis=1) * SCALE).astype(BF16)
    kd_ref[...] = jnp.concatenate(chunks[2 * pairs:3 * pairs], axis=1).astype(BF16)
    qm_ref[...] = (jnp.concatenate(chunks[3 * pairs:], axis=1) * SCALE).astype(BF16)
    vd_ref[...] = y[:, _NAT_NORM:].astype(BF16)


def _proj(x, g_mix, wt, wn, gq_col, gnat, gmat, cos_t, sin_t, cos_n, sin_n, S):
    tm = TM_PROJ
    nb = S // MOBA_BLOCK
    bpt = tm // MOBA_BLOCK
    const = lambda shape: pl.BlockSpec(shape, lambda i: (0,) * len(shape))
    rows = lambda w: pl.BlockSpec((tm, w), lambda i: (i, 0))
    return pl.pallas_call(
        _proj_kernel,
        grid=(S // tm,),
        in_specs=[rows(D_MODEL), const((1, D_MODEL)), const((2 * MOBA_W, D_MODEL)),
                  const((D_MODEL, _NAT_COLS)), const((HEAD_DIM, 1)), const((1, _NAT_NORM_PAD)),
                  const((MXU_TILE, MXU_TILE)),
                  pl.BlockSpec((HEAD_DIM, tm), lambda i: (0, i)),
                  pl.BlockSpec((HEAD_DIM, tm), lambda i: (0, i)),
                  rows(LANES), rows(LANES)],
        out_specs=[pl.BlockSpec((bpt, MOBA_W, MOBA_BLOCK), lambda i: (i, 0, 0)),
                   pl.BlockSpec((bpt, MOBA_W, MOBA_BLOCK), lambda i: (i, 0, 0)),
                   rows(MOBA_W),
                   pl.BlockSpec((bpt, 1, MOBA_W), lambda i: (i, 0, 0)),
                   rows(DIL_W), rows(DIL_W), rows(DIL_W), rows(MEM_W)],
        out_shape=[jax.ShapeDtypeStruct((nb, MOBA_W, MOBA_BLOCK), BF16),
                   jax.ShapeDtypeStruct((nb, MOBA_W, MOBA_BLOCK), BF16),
                   jax.ShapeDtypeStruct((S, MOBA_W), BF16),
                   jax.ShapeDtypeStruct((nb, 1, MOBA_W), F32),
                   jax.ShapeDtypeStruct((S, DIL_W), BF16),
                   jax.ShapeDtypeStruct((S, DIL_W), BF16),
                   jax.ShapeDtypeStruct((S, DIL_W), BF16),
                   jax.ShapeDtypeStruct((S, MEM_W), BF16)],
        compiler_params=_params("parallel"),
        name="in_proj",
    )(x, g_mix, wt, wn, gq_col, gnat, gmat, cos_t, sin_t, cos_n, sin_n)


def _moba_kernel(qt_ref, k_ref, vt_ref, km_ref, o_ref, bias_ref):
    i = pl.program_id(1)
    tq = MOBA_BLOCK
    qt = qt_ref[0]
    row = lax.broadcasted_iota(jnp.int32, (LANES, tq), 0)
    km = km_ref[...]
    nb = km.shape[0]
    km_lane = lax.broadcasted_iota(jnp.int32, (nb, LANES), 1)
    blk = lax.broadcasted_iota(jnp.int32, (nb, tq), 0)
    qs = []
    for hh in range(2):
        q_h = jnp.where((row // HEAD_DIM) == hh, qt, jnp.zeros_like(qt))
        qs.append(q_h)
        km_h = jnp.where((km_lane // HEAD_DIM) == hh, km, 0.0)
        gate = jnp.dot(km_h, q_h.astype(F32), preferred_element_type=F32,
                       precision=lax.Precision.HIGHEST)
        gate = jnp.where(blk < i, gate, NEG_INF)
        bias = jnp.full((nb, tq), NEG_INF, F32)
        for _ in range(MOBA_TOPK):
            m = jnp.max(gate, axis=0, keepdims=True)
            idx = jnp.min(jnp.where(gate == m, blk, nb), axis=0, keepdims=True)
            hit = blk == idx
            bias = jnp.where(hit & (m > NEG_INF), 0.0, bias)
            gate = jnp.where(hit, NEG_INF, gate)
        bias_ref[hh] = bias

    k_own = k_ref[pl.ds(pl.multiple_of(i * tq, tq), tq), :]
    kpos = lax.broadcasted_iota(jnp.int32, (tq, tq), 0)
    qpos = lax.broadcasted_iota(jnp.int32, (tq, tq), 1)
    causal = kpos <= qpos
    carry = []
    for hh in range(2):
        s = jnp.where(causal, _dot(k_own, qs[hh]), NEG_INF)
        m = jnp.max(s, axis=0, keepdims=True)
        p = jnp.exp(s - m)
        l = jnp.sum(p, axis=0, keepdims=True)
        acc = _dot(vt_ref[i, hh * HEAD_DIM:(hh + 1) * HEAD_DIM, :], p.astype(BF16))
        carry += [m, l, acc]

    def body(j, carry):
        k_j = k_ref[pl.ds(pl.multiple_of(j * tq, tq), tq), :]
        out = []
        for hh in range(2):
            m, l, acc = carry[3 * hh:3 * hh + 3]
            s = _dot(k_j, qs[hh]) + bias_ref[hh, pl.ds(j, 1), :]
            m_new = jnp.maximum(m, jnp.max(s, axis=0, keepdims=True))
            alpha = jnp.exp(m - m_new)
            p = jnp.exp(s - m_new)
            l = alpha * l + jnp.sum(p, axis=0, keepdims=True)
            acc = alpha * acc + _dot(vt_ref[j, hh * HEAD_DIM:(hh + 1) * HEAD_DIM, :], p.astype(BF16))
            out += [m_new, l, acc]
        return tuple(out)

    carry = lax.fori_loop(0, i, body, tuple(carry))
    o_t = jnp.concatenate([carry[2] / carry[1], carry[5] / carry[4]], axis=0)
    o_ref[...] = o_t.T.astype(BF16)


def _moba(qat, ka, vat, kmean, S):
    nb = S // MOBA_BLOCK
    pairs = MOBA_W // LANES
    return pl.pallas_call(
        _moba_kernel,
        grid=(pairs, nb),
        in_specs=[pl.BlockSpec((1, LANES, MOBA_BLOCK), lambda p, i: (i, p, 0)),
                  pl.BlockSpec((S, LANES), lambda p, i: (0, p)),
                  pl.BlockSpec((nb, LANES, MOBA_BLOCK), lambda p, i: (0, p, 0)),
                  pl.BlockSpec((nb, LANES), lambda p, i: (0, p))],
        out_specs=pl.BlockSpec((MOBA_BLOCK, LANES), lambda p, i: (i, p)),
        out_shape=jax.ShapeDtypeStruct((S, MOBA_W), BF16),
        scratch_shapes=[pltpu.VMEM((2, nb, MOBA_BLOCK), F32)],
        compiler_params=_params("arbitrary", "arbitrary"),
        name="moba_attn",
    )(qat, ka, vat, kmean)


def _dil_kernel(span, q_ref, kp_ref, kc_ref, vp_ref, vc_ref, o_ref, lse_ref):
    n = pl.program_id(2)
    blk = DIL_BLOCK
    nsub = q_ref.shape[0] // blk
    lane = lax.broadcasted_iota(jnp.int32, (blk, LANES), 1)
    lo_half = lane < HEAD_DIM
    qi = lax.broadcasted_iota(jnp.int32, (blk, 2 * blk), 0) + blk
    kj = lax.broadcasted_iota(jnp.int32, (blk, 2 * blk), 1)
    dist = qi - kj
    band = jnp.where((dist >= 0) & (dist <= span), 0.0, NEG_INF)
    band_first = jnp.where(kj >= blk, band, NEG_INF)
    for b in range(nsub):
        q = q_ref[b * blk:(b + 1) * blk, :]
        if b == 0:
            kk = jnp.concatenate([kp_ref[...], kc_ref[0:blk, :]], axis=0)
            vv = jnp.concatenate([vp_ref[...], vc_ref[0:blk, :]], axis=0)
            mask = jnp.where(n > 0, band, band_first)
        else:
            kk = kc_ref[(b - 1) * blk:(b + 1) * blk, :]
            vv = vc_ref[(b - 1) * blk:(b + 1) * blk, :]
            mask = band
        outs, lses = [], []
        for hh in range(2):
            q_h = jnp.where(lo_half if hh == 0 else ~lo_half, q, jnp.zeros_like(q))
            s = _dot_nt(q_h, kk) + mask
            m = jnp.max(s, axis=-1, keepdims=True)
            p = jnp.exp(s - m)
            den = jnp.sum(p, axis=-1, keepdims=True)
            outs.append(_dot(p.astype(BF16), vv) / den)
            lses.append(jnp.broadcast_to(m + jnp.log(den), (blk, LANES)))
        o_ref[b * blk:(b + 1) * blk, :] = jnp.where(lo_half, outs[0], outs[1]).astype(BF16)
        lse_ref[b * blk:(b + 1) * blk, :] = jnp.where(lo_half, lses[0], lses[1])


def _dilated(qd, kd, vd, window, dil, S):
    span = window // dil
    L = S // dil
    rows = min(DIL_ROWS, L)
    pairs = DIL_W // LANES
    sub = rows // DIL_BLOCK
    view = lambda t: t.reshape(L, dil * DIL_W)
    cur = pl.BlockSpec((rows, LANES), lambda r, p, n: (n, r * pairs + p))
    prev = pl.BlockSpec((DIL_BLOCK, LANES), lambda r, p, n: (jnp.maximum(n * sub - 1, 0), r * pairs + p))
    o, lse = pl.pallas_call(
        functools.partial(_dil_kernel, span),
        grid=(dil, pairs, L // rows),
        in_specs=[cur, prev, cur, prev, cur],
        out_specs=[cur, cur],
        out_shape=[jax.ShapeDtypeStruct((L, dil * DIL_W), BF16),
                   jax.ShapeDtypeStruct((L, dil * DIL_W), F32)],
        compiler_params=_params("parallel", "parallel", "arbitrary"),
        name=f"dilated_attn_d{dil}",
    )(view(qd), view(kd), view(kd), view(vd), view(vd))
    return o.reshape(S, DIL_W), lse.reshape(S, DIL_W)


def _merge_kernel(x_ref, g_ref, wg_ref, oa_ref, od1_ref, od2_ref, od3_ref, l1_ref, l2_ref, l3_ref,
                  qm_ref, mk_ref, mv_ref, wba_ref, wbd_ref, wbm_ref, wo_ref, out_ref):
    x = x_ref[...]
    tm = x.shape[0]
    h = _rms_rows(x, g_ref[...]).astype(BF16)

    l1, l2, l3 = l1_ref[...], l2_ref[...], l3_ref[...]
    lmax = jnp.maximum(jnp.maximum(l1, l2), l3)
    e1, e2, e3 = jnp.exp(l1 - lmax), jnp.exp(l2 - lmax), jnp.exp(l3 - lmax)
    o_d = (e1 * od1_ref[...].astype(F32) + e2 * od2_ref[...].astype(F32)
           + e3 * od3_ref[...].astype(F32)) / (e1 + e2 + e3)

    lane = lax.broadcasted_iota(jnp.int32, (tm, LANES), 1)
    lo_half = lane < HEAD_DIM
    o_m = []
    for pr in range(MEM_W // LANES):
        q = qm_ref[:, pr * LANES:(pr + 1) * LANES]
        mk = mk_ref[:, pr * LANES:(pr + 1) * LANES]
        mv = mv_ref[:, pr * LANES:(pr + 1) * LANES]
        outs = []
        for hh in range(2):
            q_h = jnp.where(lo_half if hh == 0 else ~lo_half, q, jnp.zeros_like(q))
            s = _dot_nt(q_h, mk)
            p = jnp.exp(s - jnp.max(s, axis=-1, keepdims=True))
            p = p / jnp.sum(p, axis=-1, keepdims=True)
            outs.append(_dot(p.astype(BF16), mv))
        o_m.append(jnp.where(lo_half, outs[0], outs[1]))
    o_m = jnp.concatenate(o_m, axis=1)

    merged = None
    branches = ((oa_ref[...], wba_ref), (o_d.astype(BF16), wbd_ref), (o_m.astype(BF16), wbm_ref))
    for bi, (o_b, w_ref) in enumerate(branches):
        gate = jax.nn.sigmoid(_dot(h, wg_ref[:, bi * D_MODEL:(bi + 1) * D_MODEL]))
        term = gate * _dot(o_b, w_ref[...])
        merged = term if merged is None else merged + term
    out_ref[...] = x + _dot(merged.astype(BF16), wo_ref[...])


def _merge(x, g_mix, w_gate, o_a, o_ds, lses, qm, mk, mv, w_ba, w_bd, w_bm, w_out, S):
    tm = TM_MERGE
    const = lambda shape: pl.BlockSpec(shape, lambda i: (0,) * len(shape))
    rows = lambda w: pl.BlockSpec((tm, w), lambda i: (i, 0))
    return pl.pallas_call(
        _merge_kernel,
        grid=(S // tm,),
        in_specs=[rows(D_MODEL), const((1, D_MODEL)), const((D_MODEL, 3 * D_MODEL)),
                  rows(MOBA_W), rows(DIL_W), rows(DIL_W), rows(DIL_W),
                  rows(DIL_W), rows(DIL_W), rows(DIL_W),
                  rows(MEM_W), const((N_MEM, MEM_W)), const((N_MEM, MEM_W)),
                  const((MOBA_W, D_MODEL)), const((DIL_W, D_MODEL)), const((MEM_W, D_MODEL)),
                  const((D_MODEL, D_MODEL))],
        out_specs=rows(D_MODEL),
        out_shape=jax.ShapeDtypeStruct((S, D_MODEL), F32),
        compiler_params=_params("parallel"),
        name="gated_merge",
    )(x, g_mix, w_gate, o_a, *o_ds, *lses, qm, mk, mv, w_ba, w_bd, w_bm, w_out)


def _ffn_kernel(x_ref, g_ref, wup_ref, cw_ref, cb_ref, wdn_ref, out_ref, halo_ref):
    i = pl.program_id(0)
    tm = x_ref.shape[0]

    @pl.when(i == 0)
    def _():
        halo_ref[...] = jnp.zeros_like(halo_ref)

    x = x_ref[...]
    h = _rms_rows(x, g_ref[...]).astype(BF16)
    row8 = lax.broadcasted_iota(jnp.int32, (8, FFN_CHUNK), 0)

    def conv(u, col):
        prev = halo_ref[:, col:col + FFN_CHUNK]
        halo_ref[:, col:col + FFN_CHUNK] = u[tm - 8:, :]
        p6 = jnp.broadcast_to(prev[6:7], (8, FFN_CHUNK))
        p7 = jnp.broadcast_to(prev[7:8], (8, FFN_CHUNK))
        r1 = pltpu.roll(u, 1, axis=0)
        r2 = pltpu.roll(u, 2, axis=0)
        top1 = jnp.where(row8 == 0, p7, r1[:8])
        top2 = jnp.where(row8 == 0, p6, jnp.where(row8 == 1, p7, r2[:8]))
        u1 = jnp.concatenate([top1, r1[8:]], axis=0)
        u2 = jnp.concatenate([top2, r2[8:]], axis=0)
        w = cw_ref[:, col:col + FFN_CHUNK]
        return cb_ref[:, col:col + FFN_CHUNK] + w[0:1] * u2 + w[1:2] * u1 + w[2:3] * u

    acc = jnp.zeros((tm, D_MODEL), F32)
    for c in range(D_FF // FFN_CHUNK):
        cg = c * FFN_CHUNK
        cv = D_FF + c * FFN_CHUNK
        u_g = conv(_dot(h, wup_ref[:, cg:cg + FFN_CHUNK]), cg)
        u_v = conv(_dot(h, wup_ref[:, cv:cv + FFN_CHUNK]), cv)
        a = (jax.nn.silu(u_g) * u_v).astype(BF16)
        acc = acc + _dot(a, wdn_ref[c * FFN_CHUNK:(c + 1) * FFN_CHUNK, :])
    out_ref[...] = x + acc


def _ffn(x, g_ffn, w_up, conv_w, conv_b, w_down, S):
    tm = TM_FFN
    const = lambda shape: pl.BlockSpec(shape, lambda i: (0,) * len(shape), pipeline_mode=pl.Buffered(1))
    rows = pl.BlockSpec((tm, D_MODEL), lambda i: (i, 0))
    return pl.pallas_call(
        _ffn_kernel,
        grid=(S // tm,),
        in_specs=[rows, const((1, D_MODEL)), const((D_MODEL, 2 * D_FF)),
                  const((CONV_WIDTH, 2 * D_FF)), const((1, 2 * D_FF)), const((D_FF, D_MODEL))],
        out_specs=rows,
        out_shape=jax.ShapeDtypeStruct((S, D_MODEL), F32),
        scratch_shapes=[pltpu.VMEM((8, 2 * D_FF), F32)],
        compiler_params=_params("arbitrary"),
        name="conv_ffn",
    )(x, g_ffn, w_up, conv_w, conv_b, w_down)


def _group_mean_matrix():
    g = np.arange(MXU_TILE) // HEAD_DIM
    return jnp.asarray((g[:, None] == g[None, :]).astype(np.float32) / HEAD_DIM, dtype=BF16)


def _layer(x, mem, positions, p):
    S = x.shape[0]
    assert S % (max(d for _, d in DIL_PATTERNS) * DIL_BLOCK) == 0 and S % TM_PROJ == 0
    row = lambda v: v.reshape(1, -1).astype(F32)
    c = np.cumsum([0, MOBA_W, MOBA_W, MOBA_W, DIL_W, DIL_W, DIL_W, MEM_W])
    w_in = p["w_in"]
    seg = lambda k: w_in[:, c[k]:c[k + 1]]
    wt = jnp.concatenate([seg(0), seg(2)], axis=1).T.astype(BF16)
    wn = jnp.concatenate([seg(1), seg(3), seg(4), seg(6), seg(5)], axis=1).astype(BF16)
    w_gate = w_in[:, QKV_COLS:].astype(BF16)
    gnat = jnp.concatenate([jnp.tile(p["moba_k_norm_g"], N_MOBA_HEADS), jnp.tile(p["dil_q_norm_g"], N_DIL_HEADS),
                            jnp.tile(p["dil_k_norm_g"], N_DIL_HEADS), jnp.tile(p["mem_q_norm_g"], N_MEM_HEADS),
                            jnp.ones((_NAT_NORM_PAD - _NAT_NORM,), F32)]).reshape(1, _NAT_NORM_PAD)
    gmat = _group_mean_matrix()

    cos_t, sin_t, cos_n, sin_n = _rope_tables(positions, S)
    mk, mv = _mem_kv(mem, row(p["mem_norm_g"]), p["w_mem_kv"].astype(BF16),
                     row(jnp.tile(p["mem_k_norm_g"], N_MEM_HEADS)), gmat)
    qat, vat, ka, kmean, qd, kd, vd, qm = _proj(
        x, row(p["mix_norm_g"]), wt, wn, p["moba_q_norm_g"].reshape(HEAD_DIM, 1).astype(F32), gnat, gmat,
        cos_t, sin_t, cos_n, sin_n, S)
    o_a = _moba(qat, ka, vat, kmean.reshape(S // MOBA_BLOCK, MOBA_W), S)
    o_ds, lses = zip(*[_dilated(qd, kd, vd, w, d, S) for w, d in DIL_PATTERNS])
    x1 = _merge(x, row(p["mix_norm_g"]), w_gate, o_a, o_ds, lses, qm, mk, mv,
                p["w_branch_moba"].astype(BF16), p["w_branch_dil"].astype(BF16),
                p["w_branch_mem"].astype(BF16), p["w_out"].astype(BF16), S)
    return _ffn(x1, row(p["ffn_norm_g"]), p["w_ffn_up"].astype(BF16), p["ffn_conv_w"].astype(F32),
                row(p["ffn_conv_b"]), p["w_ffn_down"].astype(BF16), S)


def kernel(x, mem, positions, mix_norm_g, mem_norm_g, w_in, moba_q_norm_g, moba_k_norm_g, dil_q_norm_g, dil_k_norm_g, mem_q_norm_g, mem_k_norm_g, w_mem_kv, w_branch_moba, w_branch_dil, w_branch_mem, w_out, ffn_norm_g, w_ffn_up, ffn_conv_w, ffn_conv_b, w_ffn_down):
    params = dict(mix_norm_g=mix_norm_g, mem_norm_g=mem_norm_g, w_in=w_in, moba_q_norm_g=moba_q_norm_g,
                  moba_k_norm_g=moba_k_norm_g, dil_q_norm_g=dil_q_norm_g, dil_k_norm_g=dil_k_norm_g,
                  mem_q_norm_g=mem_q_norm_g, mem_k_norm_g=mem_k_norm_g, w_mem_kv=w_mem_kv,
                  w_branch_moba=w_branch_moba, w_branch_dil=w_branch_dil, w_branch_mem=w_branch_mem,
                  w_out=w_out, ffn_norm_g=ffn_norm_g, w_ffn_up=w_ffn_up, ffn_conv_w=ffn_conv_w,
                  ffn_conv_b=ffn_conv_b, w_ffn_down=w_ffn_down)
    B = x.shape[0]
    depth = w_in.shape[0]
    outs = []
    for b in range(B):
        xb = x[b]
        for l in range(depth):
            xb = _layer(xb, mem[b], positions[b], {k: v[l] for k, v in params.items()})
        outs.append(xb)
    return jnp.stack(outs, axis=0)
```

```python
import functools

import numpy as np
import jax
import jax.numpy as jnp
from jax import lax
from jax.experimental import pallas as pl
from jax.experimental.pallas import tpu as pltpu

D_MODEL = 1024
HEAD_DIM = 64
HALF = HEAD_DIM // 2
N_MOBA_HEADS = 6
N_DIL_HEADS = 6
N_MEM_HEADS = 4
N_MEM = 256
MOBA_BLOCK = 256
MOBA_TOPK = 3
DIL_PATTERNS = ((128, 1), (512, 4), (2048, 16))
DIL_BLOCK = 128
D_FF = 2816
CONV_WIDTH = 3
ROPE_THETA = 10000.0
EPS = 1e-6
MOBA_W = N_MOBA_HEADS * HEAD_DIM
DIL_W = N_DIL_HEADS * HEAD_DIM
MEM_W = N_MEM_HEADS * HEAD_DIM
QKV_COLS = 3 * MOBA_W + 3 * DIL_W + MEM_W
SCALE = HEAD_DIM ** -0.5

LANES = 128
MXU_TILE = 256
VMEM_LIMIT = 56 * 1024 * 1024

TM_PROJ = 512
TM_MERGE = 256
TM_FFN = 512
FFN_CHUNK = 256
DIL_ROWS = 1024

F32 = jnp.float32
BF16 = jnp.bfloat16
NEG_INF = float("-inf")


def _dot(a, b):
    return jnp.dot(a, b, preferred_element_type=F32)


def _dot_nt(a, b):
    return lax.dot_general(a, b, (((1,), (1,)), ((), ())), preferred_element_type=F32)


def _rms_rows(x, g):
    ms = jnp.mean(x * x, axis=-1, keepdims=True)
    return x * lax.rsqrt(ms + EPS) * g


def _params(*sem):
    return pltpu.CompilerParams(dimension_semantics=sem, vmem_limit_bytes=VMEM_LIMIT)


def _rope_kernel(pos_ref, inv_ref, cos_t_ref, sin_t_ref, cos_n_ref, sin_n_ref):
    pos = pos_ref[...].astype(F32)
    ang = inv_ref[...] * pos
    c = jnp.cos(ang)
    s = jnp.sin(ang)
    cos_t_ref[...] = jnp.concatenate([c, c], axis=0)
    sin_t_ref[...] = jnp.concatenate([-s, s], axis=0)
    cos_n_ref[...] = jnp.concatenate([c, c, c, c], axis=0).T
    sin_n_ref[...] = jnp.concatenate([-s, s, -s, s], axis=0).T


def _rope_tables(positions, S):
    tm = 512
    inv = (ROPE_THETA ** (-jnp.arange(HALF, dtype=F32) / HALF)).reshape(HALF, 1)
    return pl.pallas_call(
        _rope_kernel,
        grid=(S // tm,),
        in_specs=[pl.BlockSpec((1, tm), lambda i: (0, i)),
                  pl.BlockSpec((HALF, 1), lambda i: (0, 0))],
        out_specs=[pl.BlockSpec((HEAD_DIM, tm), lambda i: (0, i)),
                   pl.BlockSpec((HEAD_DIM, tm), lambda i: (0, i)),
                   pl.BlockSpec((tm, LANES), lambda i: (i, 0)),
                   pl.BlockSpec((tm, LANES), lambda i: (i, 0))],
        out_shape=[jax.ShapeDtypeStruct((HEAD_DIM, S), F32),
                   jax.ShapeDtypeStruct((HEAD_DIM, S), F32),
                   jax.ShapeDtypeStruct((S, LANES), F32),
                   jax.ShapeDtypeStruct((S, LANES), F32)],
        compiler_params=_params("parallel"),
        name="rope_tables",
    )(positions.reshape(1, S), inv)


def _memkv_kernel(mem_ref, g_ref, w_ref, gk_ref, gmat_ref, mk_ref, mv_ref):
    h = _rms_rows(mem_ref[...], g_ref[...]).astype(BF16)
    kv = _dot(h, w_ref[...])
    k = kv[:, :MEM_W]
    ms = _dot((k * k).astype(BF16), gmat_ref[...])
    mk_ref[...] = (k * lax.rsqrt(ms + EPS) * gk_ref[...]).astype(BF16)
    mv_ref[...] = kv[:, MEM_W:].astype(BF16)


def _mem_kv(mem, g, w_kv, gk_tiled, gmat):
    full = lambda shape: pl.BlockSpec(shape, lambda i: (0,) * len(shape))
    return pl.pallas_call(
        _memkv_kernel,
        grid=(1,),
        in_specs=[full((N_MEM, D_MODEL)), full((1, D_MODEL)), full((D_MODEL, 2 * MEM_W)),
                  full((1, MEM_W)), full((MXU_TILE, MXU_TILE))],
        out_specs=[full((N_MEM, MEM_W)), full((N_MEM, MEM_W))],
        out_shape=[jax.ShapeDtypeStruct((N_MEM, MEM_W), BF16)] * 2,
        compiler_params=_params("arbitrary"),
        name="mem_kv",
    )(mem, g, w_kv, gk_tiled, gmat)


_NAT_NORM = 3 * 384 + MEM_W
_NAT_NORM_PAD = 1536
_NAT_ROPE = 3 * 384
_NAT_COLS = _NAT_NORM + DIL_W


def _proj_kernel(x_ref, g_ref, wt_ref, wn_ref, gq_ref, gnat_ref, gmat_ref,
                 cos_t_ref, sin_t_ref, cos_n_ref, sin_n_ref,
                 qat_ref, vat_ref, ka_ref, km_ref, qd_ref, kd_ref, vd_ref, qm_ref):
    tm = x_ref.shape[0]
    h = _rms_rows(x_ref[...], g_ref[...]).astype(BF16)

    yt = _dot_nt(wt_ref[...], h)
    cos_t = cos_t_ref[...]
    sin_t = sin_t_ref[...]
    gq = gq_ref[...]
    for hd in range(N_MOBA_HEADS):
        q = yt[hd * HEAD_DIM:(hd + 1) * HEAD_DIM]
        ms = jnp.mean(q * q, axis=0, keepdims=True)
        q = q * lax.rsqrt(ms + EPS) * gq
        rot = jnp.concatenate([q[HALF:], q[:HALF]], axis=0)
        q = (q * cos_t + rot * sin_t) * SCALE
        for b in range(tm // MOBA_BLOCK):
            qat_ref[b, hd * HEAD_DIM:(hd + 1) * HEAD_DIM, :] = (
                q[:, b * MOBA_BLOCK:(b + 1) * MOBA_BLOCK].astype(BF16))
    for b in range(tm // MOBA_BLOCK):
        vat_ref[b] = yt[MOBA_W:, b * MOBA_BLOCK:(b + 1) * MOBA_BLOCK].astype(BF16)

    y = _dot(h, wn_ref[...])
    gmat = gmat_ref[...]
    lane = lax.broadcasted_iota(jnp.int32, (tm, LANES), 1)
    first_half = (lane % HEAD_DIM) < HALF
    cos_n = cos_n_ref[...]
    sin_n = sin_n_ref[...]
    chunks = []
    for c in range(_NAT_NORM_PAD // MXU_TILE):
        blk = y[:, c * MXU_TILE:(c + 1) * MXU_TILE]
        ms = _dot((blk * blk).astype(BF16), gmat)
        blk = blk * lax.rsqrt(ms + EPS) * gnat_ref[:, c * MXU_TILE:(c + 1) * MXU_TILE]
        for hc in range(MXU_TILE // LANES):
            col = c * MXU_TILE + hc * LANES
            if col >= _NAT_NORM:
                break
            v = blk[:, hc * LANES:(hc + 1) * LANES]
            if col < _NAT_ROPE:
                rot = jnp.where(first_half, pltpu.roll(v, LANES - HALF, axis=1),
                                pltpu.roll(v, HALF, axis=1))
                v = v * cos_n + rot * sin_n
            chunks.append(v)
    pairs = DIL_W // LANES
    ka = jnp.concatenate(chunks[0:pairs], axis=1)
    for b in range(tm // MOBA_BLOCK):
        km_ref[b] = jnp.mean(ka[b * MOBA_BLOCK:(b + 1) * MOBA_BLOCK], axis=0, keepdims=True)
    ka_ref[...] = ka.astype(BF16)
    qd_ref[...] = (jnp.concatenate(chunks[pairs:2 * pairs], axis=1) * SCALE).astype(BF16)
    kd_ref[...] = jnp.concatenate(chunks[2 * pairs:3 * pairs], axis=1).astype(BF16)
    qm_ref[...] = (jnp.concatenate(chunks[3 * pairs:], axis=1) * SCALE).astype(BF16)
    vd_ref[...] = y[:, _NAT_NORM:].astype(BF16)


def _proj(x, g_mix, wt, wn, gq_col, gnat, gmat, cos_t, sin_t, cos_n, sin_n, S):
    tm = TM_PROJ
    nb = S // MOBA_BLOCK
    bpt = tm // MOBA_BLOCK
    const = lambda shape: pl.BlockSpec(shape, lambda i: (0,) * len(shape))
    rows = lambda w: pl.BlockSpec((tm, w), lambda i: (i, 0))
    return pl.pallas_call(
        _proj_kernel,
        grid=(S // tm,),
        in_specs=[rows(D_MODEL), const((1, D_MODEL)), const((2 * MOBA_W, D_MODEL)),
                  const((D_MODEL, _NAT_COLS)), const((HEAD_DIM, 1)), const((1, _NAT_NORM_PAD)),
                  const((MXU_TILE, MXU_TILE)),
                  pl.BlockSpec((HEAD_DIM, tm), lambda i: (0, i)),
                  pl.BlockSpec((HEAD_DIM, tm), lambda i: (0, i)),
                  rows(LANES), rows(LANES)],
        out_specs=[pl.BlockSpec((bpt, MOBA_W, MOBA_BLOCK), lambda i: (i, 0, 0)),
                   pl.BlockSpec((bpt, MOBA_W, MOBA_BLOCK), lambda i: (i, 0, 0)),
                   rows(MOBA_W),
                   pl.BlockSpec((bpt, 1, MOBA_W), lambda i: (i, 0, 0)),
                   rows(DIL_W), rows(DIL_W), rows(DIL_W), rows(MEM_W)],
        out_shape=[jax.ShapeDtypeStruct((nb, MOBA_W, MOBA_BLOCK), BF16),
                   jax.ShapeDtypeStruct((nb, MOBA_W, MOBA_BLOCK), BF16),
                   jax.ShapeDtypeStruct((S, MOBA_W), BF16),
                   jax.ShapeDtypeStruct((nb, 1, MOBA_W), F32),
                   jax.ShapeDtypeStruct((S, DIL_W), BF16),
                   jax.ShapeDtypeStruct((S, DIL_W), BF16),
                   jax.ShapeDtypeStruct((S, DIL_W), BF16),
                   jax.ShapeDtypeStruct((S, MEM_W), BF16)],
        compiler_params=_params("parallel"),
        name="in_proj",
    )(x, g_mix, wt, wn, gq_col, gnat, gmat, cos_t, sin_t, cos_n, sin_n)


_SHIFT_HEADROOM = 60.0
_SHIFT_SPAN = 120.0
_MOBA_UNROLL = 4


def _moba_kernel(bound_ref, qt_ref, k_ref, vt_ref, km_ref, o_ref, bias_ref, ot_ref):
    i = pl.program_id(1)
    tq = MOBA_BLOCK
    qt = qt_ref[0]
    row = lax.broadcasted_iota(jnp.int32, (LANES, tq), 0)
    km = km_ref[...]
    nb = km.shape[0]
    km_lane = lax.broadcasted_iota(jnp.int32, (nb, LANES), 1)
    blk = lax.broadcasted_iota(jnp.int32, (nb, tq), 0)
    qs = []
    for hh in range(2):
        q_h = jnp.where((row // HEAD_DIM) == hh, qt, jnp.zeros_like(qt))
        qs.append(q_h)
        km_h = jnp.where((km_lane // HEAD_DIM) == hh, km, 0.0)
        gate = jnp.dot(km_h, q_h.astype(F32), preferred_element_type=F32,
                       precision=lax.Precision.HIGHEST)
        gate = jnp.where(blk < i, gate, NEG_INF)
        bias = jnp.full((nb, tq), NEG_INF, F32)
        for _ in range(MOBA_TOPK):
            m = jnp.max(gate, axis=0, keepdims=True)
            idx = jnp.min(jnp.where(gate == m, blk, nb), axis=0, keepdims=True)
            hit = blk == idx
            bias = jnp.where(hit & (m > NEG_INF), 0.0, bias)
            gate = jnp.where(hit, NEG_INF, gate)
        bias_ref[hh] = bias

    k_own = k_ref[pl.ds(pl.multiple_of(i * tq, tq), tq), :]
    kpos = lax.broadcasted_iota(jnp.int32, (tq, tq), 0)
    qpos = lax.broadcasted_iota(jnp.int32, (tq, tq), 1)
    causal = kpos <= qpos
    s_own = [jnp.where(causal, _dot(k_own, qs[hh]), NEG_INF) for hh in range(2)]
    m_own = [jnp.max(s, axis=0, keepdims=True) for s in s_own]

    bound = bound_ref[0]
    fast = (bound - jnp.minimum(jnp.min(m_own[0]), jnp.min(m_own[1]))) <= _SHIFT_SPAN

    def v_rows(j, hh):
        return vt_ref[j, hh * HEAD_DIM:(hh + 1) * HEAD_DIM, :]

    @pl.when(fast)
    def _():
        ones = jnp.ones((16, tq), BF16)
        accs = []
        for hh in range(2):
            shift = jnp.maximum(m_own[hh], bound - _SHIFT_HEADROOM)
            bias_ref[hh] = bias_ref[hh] - shift
            p = jnp.exp(s_own[hh] - shift).astype(BF16)
            accs.append(_dot(jnp.concatenate([v_rows(i, hh), ones], axis=0), p))

        def body(t, accs):
            out = list(accs)
            chains = [(_MOBA_UNROLL * t + u, hh) for u in range(_MOBA_UNROLL) for hh in range(2)]
            ss = [_dot(k_ref[pl.ds(pl.multiple_of(j * tq, tq), tq), :], qs[hh]) for j, hh in chains]
            ps = [jnp.exp(s + bias_ref[hh, pl.ds(j, 1), :]).astype(BF16) for s, (j, hh) in zip(ss, chains)]
            for p, (j, hh) in zip(ps, chains):
                out[hh] = out[hh] + _dot(jnp.concatenate([v_rows(j, hh), ones], axis=0), p)
            return tuple(out)

        accs = lax.fori_loop(0, (i + _MOBA_UNROLL - 1) // _MOBA_UNROLL, body, tuple(accs))
        ot_ref[...] = jnp.concatenate(
            [a[:HEAD_DIM] / a[HEAD_DIM:HEAD_DIM + 1] for a in accs], axis=0)

    @pl.when(jnp.logical_not(fast))
    def _():
        carry = []
        for hh in range(2):
            p = jnp.exp(s_own[hh] - m_own[hh])
            l = jnp.sum(p, axis=0, keepdims=True)
            carry += [m_own[hh], l, _dot(v_rows(i, hh), p.astype(BF16))]

        def body(j, carry):
            k_j = k_ref[pl.ds(pl.multiple_of(j * tq, tq), tq), :]
            out = []
            for hh in range(2):
                m, l, acc = carry[3 * hh:3 * hh + 3]
                s = _dot(k_j, qs[hh]) + bias_ref[hh, pl.ds(j, 1), :]
                m_new = jnp.maximum(m, jnp.max(s, axis=0, keepdims=True))
                alpha = jnp.exp(m - m_new)
                p = jnp.exp(s - m_new)
                l = alpha * l + jnp.sum(p, axis=0, keepdims=True)
                acc = alpha * acc + _dot(v_rows(j, hh), p.astype(BF16))
                out += [m_new, l, acc]
            return tuple(out)

        carry = lax.fori_loop(0, i, body, tuple(carry))
        ot_ref[...] = jnp.concatenate([carry[2] / carry[1], carry[5] / carry[4]], axis=0)

    o_ref[...] = ot_ref[...].T.astype(BF16)


def _moba(bound, qat, ka, vat, kmean, S):
    nb = S // MOBA_BLOCK
    assert nb % _MOBA_UNROLL == 0
    pairs = MOBA_W // LANES
    return pl.pallas_call(
        _moba_kernel,
        grid=(pairs, nb),
        in_specs=[pl.BlockSpec(memory_space=pltpu.SMEM),
                  pl.BlockSpec((1, LANES, MOBA_BLOCK), lambda p, i: (i, p, 0)),
                  pl.BlockSpec((S, LANES), lambda p, i: (0, p)),
                  pl.BlockSpec((nb, LANES, MOBA_BLOCK), lambda p, i: (0, p, 0)),
                  pl.BlockSpec((nb, LANES), lambda p, i: (0, p))],
        out_specs=pl.BlockSpec((MOBA_BLOCK, LANES), lambda p, i: (i, p)),
        out_shape=jax.ShapeDtypeStruct((S, MOBA_W), BF16),
        scratch_shapes=[pltpu.VMEM((2, nb, MOBA_BLOCK), F32), pltpu.VMEM((LANES, MOBA_BLOCK), F32)],
        compiler_params=_params("arbitrary", "arbitrary"),
        name="moba_attn",
    )(bound, qat, ka, vat, kmean)


def _dil_kernel(span, q_ref, kp_ref, kc_ref, vp_ref, vc_ref, o_ref, lse_ref):
    n = pl.program_id(2)
    blk = DIL_BLOCK
    nsub = q_ref.shape[0] // blk
    lane = lax.broadcasted_iota(jnp.int32, (blk, LANES), 1)
    lo_half = lane < HEAD_DIM
    qi = lax.broadcasted_iota(jnp.int32, (blk, 2 * blk), 0) + blk
    kj = lax.broadcasted_iota(jnp.int32, (blk, 2 * blk), 1)
    dist = qi - kj
    band = jnp.where((dist >= 0) & (dist <= span), 0.0, NEG_INF)
    band_first = jnp.where(kj >= blk, band, NEG_INF)
    for b in range(nsub):
        q = q_ref[b * blk:(b + 1) * blk, :]
        if b == 0:
            kk = jnp.concatenate([kp_ref[...], kc_ref[0:blk, :]], axis=0)
            vv = jnp.concatenate([vp_ref[...], vc_ref[0:blk, :]], axis=0)
            mask = jnp.where(n > 0, band, band_first)
        else:
            kk = kc_ref[(b - 1) * blk:(b + 1) * blk, :]
            vv = vc_ref[(b - 1) * blk:(b + 1) * blk, :]
            mask = band
        outs, lses = [], []
        for hh in range(2):
            q_h = jnp.where(lo_half if hh == 0 else ~lo_half, q, jnp.zeros_like(q))
            s = _dot_nt(q_h, kk) + mask
            m = jnp.max(s, axis=-1, keepdims=True)
            p = jnp.exp(s - m)
            den = jnp.sum(p, axis=-1, keepdims=True)
            outs.append(_dot(p.astype(BF16), vv) / den)
            lses.append(jnp.broadcast_to(m + jnp.log(den), (blk, LANES)))
        o_ref[b * blk:(b + 1) * blk, :] = jnp.where(lo_half, outs[0], outs[1]).astype(BF16)
        lse_ref[b * blk:(b + 1) * blk, :] = jnp.where(lo_half, lses[0], lses[1])


def _dilated(qd, kd, vd, window, dil, S):
    span = window // dil
    L = S // dil
    rows = min(DIL_ROWS, L)
    pairs = DIL_W // LANES
    sub = rows // DIL_BLOCK
    view = lambda t: t.reshape(L, dil * DIL_W)
    cur = pl.BlockSpec((rows, LANES), lambda r, p, n: (n, r * pairs + p))
    prev = pl.BlockSpec((DIL_BLOCK, LANES), lambda r, p, n: (jnp.maximum(n * sub - 1, 0), r * pairs + p))
    o, lse = pl.pallas_call(
        functools.partial(_dil_kernel, span),
        grid=(dil, pairs, L // rows),
        in_specs=[cur, prev, cur, prev, cur],
        out_specs=[cur, cur],
        out_shape=[jax.ShapeDtypeStruct((L, dil * DIL_W), BF16),
                   jax.ShapeDtypeStruct((L, dil * DIL_W), F32)],
        compiler_params=_params("parallel", "parallel", "arbitrary"),
        name=f"dilated_attn_d{dil}",
    )(view(qd), view(kd), view(kd), view(vd), view(vd))
    return o.reshape(S, DIL_W), lse.reshape(S, DIL_W)


def _merge_kernel(x_ref, g_ref, wg_ref, oa_ref, od1_ref, od2_ref, od3_ref, l1_ref, l2_ref, l3_ref,
                  qm_ref, mk_ref, mv_ref, wba_ref, wbd_ref, wbm_ref, wo_ref, out_ref):
    x = x_ref[...]
    tm = x.shape[0]
    h = _rms_rows(x, g_ref[...]).astype(BF16)

    l1, l2, l3 = l1_ref[...], l2_ref[...], l3_ref[...]
    lmax = jnp.maximum(jnp.maximum(l1, l2), l3)
    e1, e2, e3 = jnp.exp(l1 - lmax), jnp.exp(l2 - lmax), jnp.exp(l3 - lmax)
    o_d = (e1 * od1_ref[...].astype(F32) + e2 * od2_ref[...].astype(F32)
           + e3 * od3_ref[...].astype(F32)) / (e1 + e2 + e3)

    lane = lax.broadcasted_iota(jnp.int32, (tm, LANES), 1)
    lo_half = lane < HEAD_DIM
    o_m = []
    for pr in range(MEM_W // LANES):
        q = qm_ref[:, pr * LANES:(pr + 1) * LANES]
        mk = mk_ref[:, pr * LANES:(pr + 1) * LANES]
        mv = mv_ref[:, pr * LANES:(pr + 1) * LANES]
        outs = []
        for hh in range(2):
            q_h = jnp.where(lo_half if hh == 0 else ~lo_half, q, jnp.zeros_like(q))
            s = _dot_nt(q_h, mk)
            p = jnp.exp(s - jnp.max(s, axis=-1, keepdims=True))
            p = p / jnp.sum(p, axis=-1, keepdims=True)
            outs.append(_dot(p.astype(BF16), mv))
        o_m.append(jnp.where(lo_half, outs[0], outs[1]))
    o_m = jnp.concatenate(o_m, axis=1)

    merged = None
    branches = ((oa_ref[...], wba_ref), (o_d.astype(BF16), wbd_ref), (o_m.astype(BF16), wbm_ref))
    for bi, (o_b, w_ref) in enumerate(branches):
        gate = jax.nn.sigmoid(_dot(h, wg_ref[:, bi * D_MODEL:(bi + 1) * D_MODEL]))
        term = gate * _dot(o_b, w_ref[...])
        merged = term if merged is None else merged + term
    out_ref[...] = x + _dot(merged.astype(BF16), wo_ref[...])


def _merge(x, g_mix, w_gate, o_a, o_ds, lses, qm, mk, mv, w_ba, w_bd, w_bm, w_out, S):
    tm = TM_MERGE
    const = lambda shape: pl.BlockSpec(shape, lambda i: (0,) * len(shape))
    rows = lambda w: pl.BlockSpec((tm, w), lambda i: (i, 0))
    return pl.pallas_call(
        _merge_kernel,
        grid=(S // tm,),
        in_specs=[rows(D_MODEL), const((1, D_MODEL)), const((D_MODEL, 3 * D_MODEL)),
                  rows(MOBA_W), rows(DIL_W), rows(DIL_W), rows(DIL_W),
                  rows(DIL_W), rows(DIL_W), rows(DIL_W),
                  rows(MEM_W), const((N_MEM, MEM_W)), const((N_MEM, MEM_W)),
                  const((MOBA_W, D_MODEL)), const((DIL_W, D_MODEL)), const((MEM_W, D_MODEL)),
                  const((D_MODEL, D_MODEL))],
        out_specs=rows(D_MODEL),
        out_shape=jax.ShapeDtypeStruct((S, D_MODEL), F32),
        compiler_params=_params("parallel"),
        name="gated_merge",
    )(x, g_mix, w_gate, o_a, *o_ds, *lses, qm, mk, mv, w_ba, w_bd, w_bm, w_out)


def _ffn_kernel(x_ref, g_ref, wup_ref, cw_ref, cb_ref, wdn_ref, out_ref, halo_ref):
    i = pl.program_id(0)
    tm = x_ref.shape[0]

    @pl.when(i == 0)
    def _():
        halo_ref[...] = jnp.zeros_like(halo_ref)

    x = x_ref[...]
    h = _rms_rows(x, g_ref[...]).astype(BF16)
    row8 = lax.broadcasted_iota(jnp.int32, (8, FFN_CHUNK), 0)

    def conv(u, col):
        prev = halo_ref[:, col:col + FFN_CHUNK]
        halo_ref[:, col:col + FFN_CHUNK] = u[tm - 8:, :]
        p6 = jnp.broadcast_to(prev[6:7], (8, FFN_CHUNK))
        p7 = jnp.broadcast_to(prev[7:8], (8, FFN_CHUNK))
        r1 = pltpu.roll(u, 1, axis=0)
        r2 = pltpu.roll(u, 2, axis=0)
        top1 = jnp.where(row8 == 0, p7, r1[:8])
        top2 = jnp.where(row8 == 0, p6, jnp.where(row8 == 1, p7, r2[:8]))
        u1 = jnp.concatenate([top1, r1[8:]], axis=0)
        u2 = jnp.concatenate([top2, r2[8:]], axis=0)
        w = cw_ref[:, col:col + FFN_CHUNK]
        return cb_ref[:, col:col + FFN_CHUNK] + w[0:1] * u2 + w[1:2] * u1 + w[2:3] * u

    acc = jnp.zeros((tm, D_MODEL), F32)
    for c in range(D_FF // FFN_CHUNK):
        cg = c * FFN_CHUNK
        cv = D_FF + c * FFN_CHUNK
        u_g = conv(_dot(h, wup_ref[:, cg:cg + FFN_CHUNK]), cg)
        u_v = conv(_dot(h, wup_ref[:, cv:cv + FFN_CHUNK]), cv)
        a = (jax.nn.silu(u_g) * u_v).astype(BF16)
        acc = acc + _dot(a, wdn_ref[c * FFN_CHUNK:(c + 1) * FFN_CHUNK, :])
    out_ref[...] = x + acc


def _ffn(x, g_ffn, w_up, conv_w, conv_b, w_down, S):
    tm = TM_FFN
    const = lambda shape: pl.BlockSpec(shape, lambda i: (0,) * len(shape), pipeline_mode=pl.Buffered(1))
    rows = pl.BlockSpec((tm, D_MODEL), lambda i: (i, 0))
    return pl.pallas_call(
        _ffn_kernel,
        grid=(S // tm,),
        in_specs=[rows, const((1, D_MODEL)), const((D_MODEL, 2 * D_FF)),
                  const((CONV_WIDTH, 2 * D_FF)), const((1, 2 * D_FF)), const((D_FF, D_MODEL))],
        out_specs=rows,
        out_shape=jax.ShapeDtypeStruct((S, D_MODEL), F32),
        scratch_shapes=[pltpu.VMEM((8, 2 * D_FF), F32)],
        compiler_params=_params("arbitrary"),
        name="conv_ffn",
    )(x, g_ffn, w_up, conv_w, conv_b, w_down)


def _group_mean_matrix():
    g = np.arange(MXU_TILE) // HEAD_DIM
    return jnp.asarray((g[:, None] == g[None, :]).astype(np.float32) / HEAD_DIM, dtype=BF16)


def _layer(x, mem, positions, p):
    S = x.shape[0]
    assert S % (max(d for _, d in DIL_PATTERNS) * DIL_BLOCK) == 0 and S % TM_PROJ == 0
    row = lambda v: v.reshape(1, -1).astype(F32)
    c = np.cumsum([0, MOBA_W, MOBA_W, MOBA_W, DIL_W, DIL_W, DIL_W, MEM_W])
    w_in = p["w_in"]
    seg = lambda k: w_in[:, c[k]:c[k + 1]]
    wt = jnp.concatenate([seg(0), seg(2)], axis=1).T.astype(BF16)
    wn = jnp.concatenate([seg(1), seg(3), seg(4), seg(6), seg(5)], axis=1).astype(BF16)
    w_gate = w_in[:, QKV_COLS:].astype(BF16)
    gnat = jnp.concatenate([jnp.tile(p["moba_k_norm_g"], N_MOBA_HEADS), jnp.tile(p["dil_q_norm_g"], N_DIL_HEADS),
                            jnp.tile(p["dil_k_norm_g"], N_DIL_HEADS), jnp.tile(p["mem_q_norm_g"], N_MEM_HEADS),
                            jnp.ones((_NAT_NORM_PAD - _NAT_NORM,), F32)]).reshape(1, _NAT_NORM_PAD)
    gmat = _group_mean_matrix()

    cos_t, sin_t, cos_n, sin_n = _rope_tables(positions, S)
    mk, mv = _mem_kv(mem, row(p["mem_norm_g"]), p["w_mem_kv"].astype(BF16),
                     row(jnp.tile(p["mem_k_norm_g"], N_MEM_HEADS)), gmat)
    qat, vat, ka, kmean, qd, kd, vd, qm = _proj(
        x, row(p["mix_norm_g"]), wt, wn, p["moba_q_norm_g"].reshape(HEAD_DIM, 1).astype(F32), gnat, gmat,
        cos_t, sin_t, cos_n, sin_n, S)
    bound = (1.02 * HEAD_DIM * SCALE * jnp.max(jnp.abs(p["moba_q_norm_g"]))
             * jnp.max(jnp.abs(p["moba_k_norm_g"]))).reshape(1).astype(F32)
    o_a = _moba(bound, qat, ka, vat, kmean.reshape(S // MOBA_BLOCK, MOBA_W), S)
    o_ds, lses = zip(*[_dilated(qd, kd, vd, w, d, S) for w, d in DIL_PATTERNS])
    x1 = _merge(x, row(p["mix_norm_g"]), w_gate, o_a, o_ds, lses, qm, mk, mv,
                p["w_branch_moba"].astype(BF16), p["w_branch_dil"].astype(BF16),
                p["w_branch_mem"].astype(BF16), p["w_out"].astype(BF16), S)
    return _ffn(x1, row(p["ffn_norm_g"]), p["w_ffn_up"].astype(BF16), p["ffn_conv_w"].astype(F32),
                row(p["ffn_conv_b"]), p["w_ffn_down"].astype(BF16), S)


def kernel(x, mem, positions, mix_norm_g, mem_norm_g, w_in, moba_q_norm_g, moba_k_norm_g, dil_q_norm_g, dil_k_norm_g, mem_q_norm_g, mem_k_norm_g, w_mem_kv, w_branch_moba, w_branch_dil, w_branch_mem, w_out, ffn_norm_g, w_ffn_up, ffn_conv_w, ffn_conv_b, w_ffn_down):
    params = dict(mix_norm_g=mix_norm_g, mem_norm_g=mem_norm_g, w_in=w_in, moba_q_norm_g=moba_q_norm_g,
                  moba_k_norm_g=moba_k_norm_g, dil_q_norm_g=dil_q_norm_g, dil_k_norm_g=dil_k_norm_g,
                  mem_q_norm_g=mem_q_norm_g, mem_k_norm_g=mem_k_norm_g, w_mem_kv=w_mem_kv,
                  w_branch_moba=w_branch_moba, w_branch_dil=w_branch_dil, w_branch_mem=w_branch_mem,
                  w_out=w_out, ffn_norm_g=ffn_norm_g, w_ffn_up=w_ffn_up, ffn_conv_w=ffn_conv_w,
                  ffn_conv_b=ffn_conv_b, w_ffn_down=w_ffn_down)
    B = x.shape[0]
    depth = w_in.shape[0]
    outs = []
    for b in range(B):
        xb = x[b]
        for l in range(depth):
            xb = _layer(xb, mem[b], positions[b], {k: v[l] for k, v in params.items()})
        outs.append(xb)
    return jnp.stack(outs, axis=0)
```

```python
import functools

import numpy as np
import jax
import jax.numpy as jnp
from jax import lax
from jax.experimental import pallas as pl
from jax.experimental.pallas import tpu as pltpu

D_MODEL = 1024
HEAD_DIM = 64
HALF = HEAD_DIM // 2
N_MOBA_HEADS = 6
N_DIL_HEADS = 6
N_MEM_HEADS = 4
N_MEM = 256
MOBA_BLOCK = 256
MOBA_TOPK = 3
DIL_PATTERNS = ((128, 1), (512, 4), (2048, 16))
DIL_BLOCK = 128
D_FF = 2816
CONV_WIDTH = 3
ROPE_THETA = 10000.0
EPS = 1e-6
MOBA_W = N_MOBA_HEADS * HEAD_DIM
DIL_W = N_DIL_HEADS * HEAD_DIM
MEM_W = N_MEM_HEADS * HEAD_DIM
QKV_COLS = 3 * MOBA_W + 3 * DIL_W + MEM_W
SCALE = HEAD_DIM ** -0.5

LANES = 128
MXU_TILE = 256
VMEM_LIMIT = 56 * 1024 * 1024

TM_PROJ = 512
TM_MERGE = 256
TM_FFN = 512
FFN_CHUNK = 256
DIL_ROWS = 1024

F32 = jnp.float32
BF16 = jnp.bfloat16
NEG_INF = float("-inf")


def _dot(a, b):
    return jnp.dot(a, b, preferred_element_type=F32)


def _dot_nt(a, b):
    return lax.dot_general(a, b, (((1,), (1,)), ((), ())), preferred_element_type=F32)


def _rms_rows(x, g):
    ms = jnp.mean(x * x, axis=-1, keepdims=True)
    return x * lax.rsqrt(ms + EPS) * g


def _params(*sem):
    return pltpu.CompilerParams(dimension_semantics=sem, vmem_limit_bytes=VMEM_LIMIT)


def _rope_kernel(pos_ref, inv_ref, cos_t_ref, sin_t_ref, cos_n_ref, sin_n_ref):
    pos = pos_ref[...].astype(F32)
    ang = inv_ref[...] * pos
    c = jnp.cos(ang)
    s = jnp.sin(ang)
    cos_t_ref[...] = jnp.concatenate([c, c], axis=0)
    sin_t_ref[...] = jnp.concatenate([-s, s], axis=0)
    cos_n_ref[...] = jnp.concatenate([c, c, c, c], axis=0).T
    sin_n_ref[...] = jnp.concatenate([-s, s, -s, s], axis=0).T


def _rope_tables(positions, S):
    tm = 512
    inv = (ROPE_THETA ** (-jnp.arange(HALF, dtype=F32) / HALF)).reshape(HALF, 1)
    return pl.pallas_call(
        _rope_kernel,
        grid=(S // tm,),
        in_specs=[pl.BlockSpec((1, tm), lambda i: (0, i)),
                  pl.BlockSpec((HALF, 1), lambda i: (0, 0))],
        out_specs=[pl.BlockSpec((HEAD_DIM, tm), lambda i: (0, i)),
                   pl.BlockSpec((HEAD_DIM, tm), lambda i: (0, i)),
                   pl.BlockSpec((tm, LANES), lambda i: (i, 0)),
                   pl.BlockSpec((tm, LANES), lambda i: (i, 0))],
        out_shape=[jax.ShapeDtypeStruct((HEAD_DIM, S), F32),
                   jax.ShapeDtypeStruct((HEAD_DIM, S), F32),
                   jax.ShapeDtypeStruct((S, LANES), F32),
                   jax.ShapeDtypeStruct((S, LANES), F32)],
        compiler_params=_params("parallel"),
        name="rope_tables",
    )(positions.reshape(1, S), inv)


def _memkv_kernel(mem_ref, g_ref, w_ref, gk_ref, gmat_ref, mk_ref, mv_ref):
    h = _rms_rows(mem_ref[...], g_ref[...]).astype(BF16)
    kv = _dot(h, w_ref[...])
    k = kv[:, :MEM_W]
    ms = _dot((k * k).astype(BF16), gmat_ref[...])
    mk_ref[...] = (k * lax.rsqrt(ms + EPS) * gk_ref[...]).astype(BF16)
    mv_ref[...] = kv[:, MEM_W:].astype(BF16)


def _mem_kv(mem, g, w_kv, gk_tiled, gmat):
    full = lambda shape: pl.BlockSpec(shape, lambda i: (0,) * len(shape))
    return pl.pallas_call(
        _memkv_kernel,
        grid=(1,),
        in_specs=[full((N_MEM, D_MODEL)), full((1, D_MODEL)), full((D_MODEL, 2 * MEM_W)),
                  full((1, MEM_W)), full((MXU_TILE, MXU_TILE))],
        out_specs=[full((N_MEM, MEM_W)), full((N_MEM, MEM_W))],
        out_shape=[jax.ShapeDtypeStruct((N_MEM, MEM_W), BF16)] * 2,
        compiler_params=_params("arbitrary"),
        name="mem_kv",
    )(mem, g, w_kv, gk_tiled, gmat)


_NAT_NORM = 3 * 384 + MEM_W
_NAT_NORM_PAD = 1536
_NAT_ROPE = 3 * 384
_NAT_COLS = _NAT_NORM + DIL_W


def _proj_kernel(x_ref, g_ref, wt_ref, wn_ref, gq_ref, gnat_ref, gmat_ref,
                 cos_t_ref, sin_t_ref, cos_n_ref, sin_n_ref,
                 qat_ref, vat_ref, ka_ref, km_ref, qm_ref, *rest):
    dil_refs, stage_ref = rest[:-1], rest[-1]
    tm = x_ref.shape[0]
    h = _rms_rows(x_ref[...], g_ref[...]).astype(BF16)

    yt = _dot_nt(wt_ref[...], h)
    cos_t = cos_t_ref[...]
    sin_t = sin_t_ref[...]
    gq = gq_ref[...]
    for hd in range(N_MOBA_HEADS):
        q = yt[hd * HEAD_DIM:(hd + 1) * HEAD_DIM]
        ms = jnp.mean(q * q, axis=0, keepdims=True)
        q = q * lax.rsqrt(ms + EPS) * gq
        rot = jnp.concatenate([q[HALF:], q[:HALF]], axis=0)
        q = (q * cos_t + rot * sin_t) * SCALE
        for b in range(tm // MOBA_BLOCK):
            qat_ref[b, hd * HEAD_DIM:(hd + 1) * HEAD_DIM, :] = (
                q[:, b * MOBA_BLOCK:(b + 1) * MOBA_BLOCK].astype(BF16))
    for b in range(tm // MOBA_BLOCK):
        vat_ref[b] = yt[MOBA_W:, b * MOBA_BLOCK:(b + 1) * MOBA_BLOCK].astype(BF16)

    y = _dot(h, wn_ref[...])
    gmat = gmat_ref[...]
    lane = lax.broadcasted_iota(jnp.int32, (tm, LANES), 1)
    first_half = (lane % HEAD_DIM) < HALF
    cos_n = cos_n_ref[...]
    sin_n = sin_n_ref[...]
    chunks = []
    for c in range(_NAT_NORM_PAD // MXU_TILE):
        blk = y[:, c * MXU_TILE:(c + 1) * MXU_TILE]
        ms = _dot((blk * blk).astype(BF16), gmat)
        blk = blk * lax.rsqrt(ms + EPS) * gnat_ref[:, c * MXU_TILE:(c + 1) * MXU_TILE]
        for hc in range(MXU_TILE // LANES):
            col = c * MXU_TILE + hc * LANES
            if col >= _NAT_NORM:
                break
            v = blk[:, hc * LANES:(hc + 1) * LANES]
            if col < _NAT_ROPE:
                rot = jnp.where(first_half, pltpu.roll(v, LANES - HALF, axis=1),
                                pltpu.roll(v, HALF, axis=1))
                v = v * cos_n + rot * sin_n
            chunks.append(v)
    pairs = DIL_W // LANES
    ka = jnp.concatenate(chunks[0:pairs], axis=1)
    for b in range(tm // MOBA_BLOCK):
        km_ref[b] = jnp.mean(ka[b * MOBA_BLOCK:(b + 1) * MOBA_BLOCK], axis=0, keepdims=True)
    ka_ref[...] = ka.astype(BF16)
    qm_ref[...] = (jnp.concatenate(chunks[3 * pairs:], axis=1) * SCALE).astype(BF16)

    for lt in range(pairs):
        stage_ref[lt] = chunks[pairs + lt] * SCALE
        stage_ref[pairs + lt] = chunks[2 * pairs + lt]
        stage_ref[2 * pairs + lt] = y[:, _NAT_NORM + lt * LANES:_NAT_NORM + (lt + 1) * LANES]
    for pi, (_, d) in enumerate(DIL_PATTERNS):
        for ti in range(3):
            ref = dil_refs[3 * pi + ti]
            for r in range(d):
                for lt in range(pairs):
                    col = r * DIL_W + lt * LANES
                    ref[:, col:col + LANES] = stage_ref[
                        ti * pairs + lt, pl.ds(r, tm // d, stride=d), :].astype(BF16)


def _proj(x, g_mix, wt, wn, gq_col, gnat, gmat, cos_t, sin_t, cos_n, sin_n, S):
    tm = TM_PROJ
    nb = S // MOBA_BLOCK
    bpt = tm // MOBA_BLOCK
    const = lambda shape: pl.BlockSpec(shape, lambda i: (0,) * len(shape))
    rows = lambda w: pl.BlockSpec((tm, w), lambda i: (i, 0))
    dils = [d for _, d in DIL_PATTERNS for _ in range(3)]
    outs = pl.pallas_call(
        _proj_kernel,
        grid=(S // tm,),
        in_specs=[rows(D_MODEL), const((1, D_MODEL)), const((2 * MOBA_W, D_MODEL)),
                  const((D_MODEL, _NAT_COLS)), const((HEAD_DIM, 1)), const((1, _NAT_NORM_PAD)),
                  const((MXU_TILE, MXU_TILE)),
                  pl.BlockSpec((HEAD_DIM, tm), lambda i: (0, i)),
                  pl.BlockSpec((HEAD_DIM, tm), lambda i: (0, i)),
                  rows(LANES), rows(LANES)],
        out_specs=[pl.BlockSpec((bpt, MOBA_W, MOBA_BLOCK), lambda i: (i, 0, 0)),
                   pl.BlockSpec((bpt, MOBA_W, MOBA_BLOCK), lambda i: (i, 0, 0)),
                   rows(MOBA_W),
                   pl.BlockSpec((bpt, 1, MOBA_W), lambda i: (i, 0, 0)),
                   rows(MEM_W)]
                  + [pl.BlockSpec((tm // d, d * DIL_W), lambda i: (i, 0)) for d in dils],
        out_shape=[jax.ShapeDtypeStruct((nb, MOBA_W, MOBA_BLOCK), BF16),
                   jax.ShapeDtypeStruct((nb, MOBA_W, MOBA_BLOCK), BF16),
                   jax.ShapeDtypeStruct((S, MOBA_W), BF16),
                   jax.ShapeDtypeStruct((nb, 1, MOBA_W), F32),
                   jax.ShapeDtypeStruct((S, MEM_W), BF16)]
                  + [jax.ShapeDtypeStruct((S // d, d * DIL_W), BF16) for d in dils],
        scratch_shapes=[pltpu.VMEM((3 * DIL_W // LANES, tm, LANES), F32)],
        compiler_params=_params("parallel"),
        name="in_proj",
    )(x, g_mix, wt, wn, gq_col, gnat, gmat, cos_t, sin_t, cos_n, sin_n)
    return outs[:5], [outs[5 + 3 * k:8 + 3 * k] for k in range(len(DIL_PATTERNS))]


_SHIFT_HEADROOM = 30.0
_FAST_BOUND = 50.0
_MOBA_UNROLL = 4


def _moba_kernel(bound_ref, qt_ref, k_ref, vt_ref, km_ref, o_ref, bias_ref, ot_ref):
    i = pl.program_id(1)
    tq = MOBA_BLOCK
    unroll = _MOBA_UNROLL
    qt = qt_ref[0]
    row = lax.broadcasted_iota(jnp.int32, (LANES, tq), 0)
    km = km_ref[...]
    nb = km.shape[0]
    blk = lax.broadcasted_iota(jnp.int32, (nb, tq), 0)

    bound = bound_ref[0]
    fast = bound <= _FAST_BOUND
    sel_bias = jnp.where(fast, _SHIFT_HEADROOM - bound, 0.0)

    km1 = km.astype(BF16)
    r1 = km - km1.astype(F32)
    km2 = r1.astype(BF16)
    km3 = (r1 - km2.astype(F32)).astype(BF16)
    km_parts = jnp.concatenate([km1, km2, km3], axis=0)
    qs = []
    for hh in range(2):
        q_h = jnp.where((row // HEAD_DIM) == hh, qt, jnp.zeros_like(qt))
        qs.append(q_h)
        g3 = _dot(km_parts, q_h)
        gate = g3[:nb] + g3[nb:2 * nb] + g3[2 * nb:]
        gate = jnp.where(blk < i, gate, NEG_INF)
        bias = jnp.full((nb, tq), NEG_INF, F32)
        for _ in range(MOBA_TOPK):
            m = jnp.max(gate, axis=0, keepdims=True)
            idx = jnp.min(jnp.where(gate == m, blk, nb), axis=0, keepdims=True)
            hit = blk == idx
            bias = jnp.where(hit & (m > NEG_INF), sel_bias, bias)
            gate = jnp.where(hit, NEG_INF, gate)
        bias_ref[hh] = bias

    kpos = lax.broadcasted_iota(jnp.int32, (tq, tq), 0)
    qpos = lax.broadcasted_iota(jnp.int32, (tq, tq), 1)
    causal = kpos <= qpos

    def k_rows(j):
        return k_ref[pl.ds(pl.multiple_of(j * tq, tq), tq), :]

    def v_rows(j, hh):
        return vt_ref[j, hh * HEAD_DIM:(hh + 1) * HEAD_DIM, :]

    ones = jnp.ones((16, tq), BF16)

    def values(j, hh, p):
        return _dot(jnp.concatenate([v_rows(j, hh), ones], axis=0), p)

    own_accs = []
    for hh in range(2):
        s = jnp.where(causal, _dot(k_rows(i), qs[hh]) + sel_bias, NEG_INF)
        own_accs.append(values(i, hh, jnp.exp(s).astype(BF16)))

    @pl.when(fast)
    def _():
        accs = own_accs

        def body(t, accs):
            out = list(accs)
            chains = [(unroll * t + u, hh) for u in range(unroll) for hh in range(2)]
            ss = [_dot(k_rows(j), qs[hh]) for j, hh in chains]
            ps = [jnp.exp(s + bias_ref[hh, pl.ds(j, 1), :]).astype(BF16) for s, (j, hh) in zip(ss, chains)]
            for p, (j, hh) in zip(ps, chains):
                out[hh] = out[hh] + values(j, hh, p)
            return tuple(out)

        accs = lax.fori_loop(0, (i + unroll - 1) // unroll, body, tuple(accs))
        ot_ref[...] = jnp.concatenate(
            [a[:HEAD_DIM] / a[HEAD_DIM:HEAD_DIM + 1] for a in accs], axis=0)

    @pl.when(jnp.logical_not(fast))
    def _():
        k_own = k_rows(i)
        carry = []
        for hh in range(2):
            s = jnp.where(causal, _dot(k_own, qs[hh]), NEG_INF)
            m = jnp.max(s, axis=0, keepdims=True)
            p = jnp.exp(s - m)
            l = jnp.sum(p, axis=0, keepdims=True)
            carry += [m, l, _dot(v_rows(i, hh), p.astype(BF16))]

        def body(j, carry):
            k_j = k_ref[pl.ds(pl.multiple_of(j * tq, tq), tq), :]
            out = []
            for hh in range(2):
                m, l, acc = carry[3 * hh:3 * hh + 3]
                s = _dot(k_j, qs[hh]) + bias_ref[hh, pl.ds(j, 1), :]
                m_new = jnp.maximum(m, jnp.max(s, axis=0, keepdims=True))
                alpha = jnp.exp(m - m_new)
                p = jnp.exp(s - m_new)
                l = alpha * l + jnp.sum(p, axis=0, keepdims=True)
                acc = alpha * acc + _dot(v_rows(j, hh), p.astype(BF16))
                out += [m_new, l, acc]
            return tuple(out)

        carry = lax.fori_loop(0, i, body, tuple(carry))
        ot_ref[...] = jnp.concatenate([carry[2] / carry[1], carry[5] / carry[4]], axis=0)

    o_ref[...] = ot_ref[...].T.astype(BF16)


def _moba(bound, qat, ka, vat, kmean, S):
    nb = S // MOBA_BLOCK
    assert nb % _MOBA_UNROLL == 0
    pairs = MOBA_W // LANES
    return pl.pallas_call(
        _moba_kernel,
        grid=(pairs, nb),
        in_specs=[pl.BlockSpec(memory_space=pltpu.SMEM),
                  pl.BlockSpec((1, LANES, MOBA_BLOCK), lambda p, i: (i, p, 0)),
                  pl.BlockSpec((S, LANES), lambda p, i: (0, p)),
                  pl.BlockSpec((nb, LANES, MOBA_BLOCK), lambda p, i: (0, p, 0)),
                  pl.BlockSpec((nb, LANES), lambda p, i: (0, p))],
        out_specs=pl.BlockSpec((MOBA_BLOCK, LANES), lambda p, i: (i, p)),
        out_shape=jax.ShapeDtypeStruct((S, MOBA_W), BF16),
        scratch_shapes=[pltpu.VMEM((2, nb, MOBA_BLOCK), F32), pltpu.VMEM((LANES, MOBA_BLOCK), F32)],
        compiler_params=_params("arbitrary", "arbitrary"),
        name="moba_attn",
    )(bound, qat, ka, vat, kmean)


def _dil_kernel(span, q_ref, kp_ref, kc_ref, vp_ref, vc_ref, o_ref, lse_ref):
    n = pl.program_id(2)
    blk = DIL_BLOCK
    nsub = q_ref.shape[0] // blk
    lane = lax.broadcasted_iota(jnp.int32, (blk, LANES), 1)
    lo_half = lane < HEAD_DIM
    qi = lax.broadcasted_iota(jnp.int32, (blk, 2 * blk), 0) + blk
    kj = lax.broadcasted_iota(jnp.int32, (blk, 2 * blk), 1)
    dist = qi - kj
    band = jnp.where((dist >= 0) & (dist <= span), 0.0, NEG_INF)
    band_first = jnp.where(kj >= blk, band, NEG_INF)
    for b in range(nsub):
        q = q_ref[b * blk:(b + 1) * blk, :]
        if b == 0:
            kk = jnp.concatenate([kp_ref[...], kc_ref[0:blk, :]], axis=0)
            vv = jnp.concatenate([vp_ref[...], vc_ref[0:blk, :]], axis=0)
            mask = jnp.where(n > 0, band, band_first)
        else:
            kk = kc_ref[(b - 1) * blk:(b + 1) * blk, :]
            vv = vc_ref[(b - 1) * blk:(b + 1) * blk, :]
            mask = band
        outs, lses = [], []
        for hh in range(2):
            q_h = jnp.where(lo_half if hh == 0 else ~lo_half, q, jnp.zeros_like(q))
            s = _dot_nt(q_h, kk) + mask
            m = jnp.max(s, axis=-1, keepdims=True)
            p = jnp.exp(s - m)
            den = jnp.sum(p, axis=-1, keepdims=True)
            outs.append(_dot(p.astype(BF16), vv) / den)
            lses.append(jnp.broadcast_to(m + jnp.log(den), (blk, LANES)))
        o_ref[b * blk:(b + 1) * blk, :] = jnp.where(lo_half, outs[0], outs[1]).astype(BF16)
        lse_ref[b * blk:(b + 1) * blk, :] = jnp.where(lo_half, lses[0], lses[1])


def _dilated(qd, kd, vd, window, dil, S):
    span = window // dil
    L = S // dil
    rows = min(DIL_ROWS, L)
    pairs = DIL_W // LANES
    sub = rows // DIL_BLOCK
    cur = pl.BlockSpec((rows, LANES), lambda r, p, n: (n, r * pairs + p))
    prev = pl.BlockSpec((DIL_BLOCK, LANES), lambda r, p, n: (jnp.maximum(n * sub - 1, 0), r * pairs + p))
    return pl.pallas_call(
        functools.partial(_dil_kernel, span),
        grid=(dil, pairs, L // rows),
        in_specs=[cur, prev, cur, prev, cur],
        out_specs=[cur, cur],
        out_shape=[jax.ShapeDtypeStruct((L, dil * DIL_W), BF16),
                   jax.ShapeDtypeStruct((L, dil * DIL_W), F32)],
        compiler_params=_params("parallel", "parallel", "arbitrary"),
        name=f"dilated_attn_d{dil}",
    )(qd, kd, kd, vd, vd)


def _merge_kernel(x_ref, g_ref, wg_ref, oa_ref, od1_ref, od2_ref, od3_ref, l1_ref, l2_ref, l3_ref,
                  qm_ref, mk_ref, mv_ref, wba_ref, wbd_ref, wbm_ref, wo_ref, out_ref, nat_ref):
    x = x_ref[...]
    tm = x.shape[0]
    h = _rms_rows(x, g_ref[...]).astype(BF16)

    def natural(ref, dil, slot):
        if dil == 1:
            return ref[...].astype(F32)
        tiles = DIL_W // LANES
        for r in range(dil):
            for lt in range(tiles):
                col = r * DIL_W + lt * LANES
                nat_ref[slot * tiles + lt, pl.ds(r, tm // dil, stride=dil), :] = (
                    ref[:, col:col + LANES].astype(F32))
        return jnp.concatenate([nat_ref[slot * tiles + lt] for lt in range(tiles)], axis=1)

    dils = [d for _, d in DIL_PATTERNS]
    l1, l2, l3 = [natural(ref, d, k) for k, (ref, d) in enumerate(zip((l1_ref, l2_ref, l3_ref), dils))]
    o1, o2, o3 = [natural(ref, d, 3 + k) for k, (ref, d) in enumerate(zip((od1_ref, od2_ref, od3_ref), dils))]
    lmax = jnp.maximum(jnp.maximum(l1, l2), l3)
    e1, e2, e3 = jnp.exp(l1 - lmax), jnp.exp(l2 - lmax), jnp.exp(l3 - lmax)
    o_d = (e1 * o1 + e2 * o2 + e3 * o3) / (e1 + e2 + e3)

    lane = lax.broadcasted_iota(jnp.int32, (tm, LANES), 1)
    lo_half = lane < HEAD_DIM
    o_m = []
    for pr in range(MEM_W // LANES):
        q = qm_ref[:, pr * LANES:(pr + 1) * LANES]
        mk = mk_ref[:, pr * LANES:(pr + 1) * LANES]
        mv = mv_ref[:, pr * LANES:(pr + 1) * LANES]
        outs = []
        for hh in range(2):
            q_h = jnp.where(lo_half if hh == 0 else ~lo_half, q, jnp.zeros_like(q))
            s = _dot_nt(q_h, mk)
            p = jnp.exp(s - jnp.max(s, axis=-1, keepdims=True))
            p = p / jnp.sum(p, axis=-1, keepdims=True)
            outs.append(_dot(p.astype(BF16), mv))
        o_m.append(jnp.where(lo_half, outs[0], outs[1]))
    o_m = jnp.concatenate(o_m, axis=1)

    merged = None
    branches = ((oa_ref[...], wba_ref), (o_d.astype(BF16), wbd_ref), (o_m.astype(BF16), wbm_ref))
    for bi, (o_b, w_ref) in enumerate(branches):
        gate = jax.nn.sigmoid(_dot(h, wg_ref[:, bi * D_MODEL:(bi + 1) * D_MODEL]))
        term = gate * _dot(o_b, w_ref[...])
        merged = term if merged is None else merged + term
    out_ref[...] = x + _dot(merged.astype(BF16), wo_ref[...])


def _merge(x, g_mix, w_gate, o_a, o_ds, lses, qm, mk, mv, w_ba, w_bd, w_bm, w_out, S):
    tm = TM_MERGE
    const = lambda shape: pl.BlockSpec(shape, lambda i: (0,) * len(shape))
    rows = lambda w: pl.BlockSpec((tm, w), lambda i: (i, 0))
    dil_rows = [pl.BlockSpec((tm // d, d * DIL_W), lambda i: (i, 0)) for _, d in DIL_PATTERNS]
    return pl.pallas_call(
        _merge_kernel,
        grid=(S // tm,),
        in_specs=[rows(D_MODEL), const((1, D_MODEL)), const((D_MODEL, 3 * D_MODEL)),
                  rows(MOBA_W), *dil_rows, *dil_rows,
                  rows(MEM_W), const((N_MEM, MEM_W)), const((N_MEM, MEM_W)),
                  const((MOBA_W, D_MODEL)), const((DIL_W, D_MODEL)), const((MEM_W, D_MODEL)),
                  const((D_MODEL, D_MODEL))],
        out_specs=rows(D_MODEL),
        out_shape=jax.ShapeDtypeStruct((S, D_MODEL), F32),
        scratch_shapes=[pltpu.VMEM((2 * len(DIL_PATTERNS) * DIL_W // LANES, tm, LANES), F32)],
        compiler_params=_params("parallel"),
        name="gated_merge",
    )(x, g_mix, w_gate, o_a, *o_ds, *lses, qm, mk, mv, w_ba, w_bd, w_bm, w_out)


def _ffn_kernel(x_ref, g_ref, wup_ref, cw_ref, cb_ref, wdn_ref, out_ref, halo_ref):
    i = pl.program_id(0)
    tm = x_ref.shape[0]

    @pl.when(i == 0)
    def _():
        halo_ref[...] = jnp.zeros_like(halo_ref)

    x = x_ref[...]
    h = _rms_rows(x, g_ref[...]).astype(BF16)
    row8 = lax.broadcasted_iota(jnp.int32, (8, FFN_CHUNK), 0)

    def conv(u, col):
        prev = halo_ref[:, col:col + FFN_CHUNK]
        halo_ref[:, col:col + FFN_CHUNK] = u[tm - 8:, :]
        p6 = jnp.broadcast_to(prev[6:7], (8, FFN_CHUNK))
        p7 = jnp.broadcast_to(prev[7:8], (8, FFN_CHUNK))
        r1 = pltpu.roll(u, 1, axis=0)
        r2 = pltpu.roll(u, 2, axis=0)
        top1 = jnp.where(row8 == 0, p7, r1[:8])
        top2 = jnp.where(row8 == 0, p6, jnp.where(row8 == 1, p7, r2[:8]))
        u1 = jnp.concatenate([top1, r1[8:]], axis=0)
        u2 = jnp.concatenate([top2, r2[8:]], axis=0)
        w = cw_ref[:, col:col + FFN_CHUNK]
        return cb_ref[:, col:col + FFN_CHUNK] + w[0:1] * u2 + w[1:2] * u1 + w[2:3] * u

    acc = jnp.zeros((tm, D_MODEL), F32)
    for c in range(D_FF // FFN_CHUNK):
        cg = c * FFN_CHUNK
        cv = D_FF + c * FFN_CHUNK
        u_g = conv(_dot(h, wup_ref[:, cg:cg + FFN_CHUNK]), cg)
        u_v = conv(_dot(h, wup_ref[:, cv:cv + FFN_CHUNK]), cv)
        a = (jax.nn.silu(u_g) * u_v).astype(BF16)
        acc = acc + _dot(a, wdn_ref[c * FFN_CHUNK:(c + 1) * FFN_CHUNK, :])
    out_ref[...] = x + acc


def _ffn(x, g_ffn, w_up, conv_w, conv_b, w_down, S):
    tm = TM_FFN
    const = lambda shape: pl.BlockSpec(shape, lambda i: (0,) * len(shape), pipeline_mode=pl.Buffered(1))
    rows = pl.BlockSpec((tm, D_MODEL), lambda i: (i, 0))
    return pl.pallas_call(
        _ffn_kernel,
        grid=(S // tm,),
        in_specs=[rows, const((1, D_MODEL)), const((D_MODEL, 2 * D_FF)),
                  const((CONV_WIDTH, 2 * D_FF)), const((1, 2 * D_FF)), const((D_FF, D_MODEL))],
        out_specs=rows,
        out_shape=jax.ShapeDtypeStruct((S, D_MODEL), F32),
        scratch_shapes=[pltpu.VMEM((8, 2 * D_FF), F32)],
        compiler_params=_params("arbitrary"),
        name="conv_ffn",
    )(x, g_ffn, w_up, conv_w, conv_b, w_down)


def _group_mean_matrix():
    g = np.arange(MXU_TILE) // HEAD_DIM
    return jnp.asarray((g[:, None] == g[None, :]).astype(np.float32) / HEAD_DIM, dtype=BF16)


def _layer(x, mem, positions, p):
    S = x.shape[0]
    assert S % (max(d for _, d in DIL_PATTERNS) * DIL_BLOCK) == 0 and S % TM_PROJ == 0
    row = lambda v: v.reshape(1, -1).astype(F32)
    c = np.cumsum([0, MOBA_W, MOBA_W, MOBA_W, DIL_W, DIL_W, DIL_W, MEM_W])
    w_in = p["w_in"]
    seg = lambda k: w_in[:, c[k]:c[k + 1]]
    wt = jnp.concatenate([seg(0), seg(2)], axis=1).T.astype(BF16)
    wn = jnp.concatenate([seg(1), seg(3), seg(4), seg(6), seg(5)], axis=1).astype(BF16)
    w_gate = w_in[:, QKV_COLS:].astype(BF16)
    gnat = jnp.concatenate([jnp.tile(p["moba_k_norm_g"], N_MOBA_HEADS), jnp.tile(p["dil_q_norm_g"], N_DIL_HEADS),
                            jnp.tile(p["dil_k_norm_g"], N_DIL_HEADS), jnp.tile(p["mem_q_norm_g"], N_MEM_HEADS),
                            jnp.ones((_NAT_NORM_PAD - _NAT_NORM,), F32)]).reshape(1, _NAT_NORM_PAD)
    gmat = _group_mean_matrix()

    cos_t, sin_t, cos_n, sin_n = _rope_tables(positions, S)
    mk, mv = _mem_kv(mem, row(p["mem_norm_g"]), p["w_mem_kv"].astype(BF16),
                     row(jnp.tile(p["mem_k_norm_g"], N_MEM_HEADS)), gmat)
    (qat, vat, ka, kmean, qm), dil_qkv = _proj(
        x, row(p["mix_norm_g"]), wt, wn, p["moba_q_norm_g"].reshape(HEAD_DIM, 1).astype(F32), gnat, gmat,
        cos_t, sin_t, cos_n, sin_n, S)
    bound = (1.02 * HEAD_DIM * SCALE * jnp.max(jnp.abs(p["moba_q_norm_g"]))
             * jnp.max(jnp.abs(p["moba_k_norm_g"]))).reshape(1).astype(F32)
    o_a = _moba(bound, qat, ka, vat, kmean.reshape(S // MOBA_BLOCK, MOBA_W), S)
    o_ds, lses = zip(*[_dilated(*qkv, w, d, S) for qkv, (w, d) in zip(dil_qkv, DIL_PATTERNS)])
    x1 = _merge(x, row(p["mix_norm_g"]), w_gate, o_a, o_ds, lses, qm, mk, mv,
                p["w_branch_moba"].astype(BF16), p["w_branch_dil"].astype(BF16),
                p["w_branch_mem"].astype(BF16), p["w_out"].astype(BF16), S)
    return _ffn(x1, row(p["ffn_norm_g"]), p["w_ffn_up"].astype(BF16), p["ffn_conv_w"].astype(F32),
                row(p["ffn_conv_b"]), p["w_ffn_down"].astype(BF16), S)


def kernel(x, mem, positions, mix_norm_g, mem_norm_g, w_in, moba_q_norm_g, moba_k_norm_g, dil_q_norm_g, dil_k_norm_g, mem_q_norm_g, mem_k_norm_g, w_mem_kv, w_branch_moba, w_branch_dil, w_branch_mem, w_out, ffn_norm_g, w_ffn_up, ffn_conv_w, ffn_conv_b, w_ffn_down):
    params = dict(mix_norm_g=mix_norm_g, mem_norm_g=mem_norm_g, w_in=w_in, moba_q_norm_g=moba_q_norm_g,
                  moba_k_norm_g=moba_k_norm_g, dil_q_norm_g=dil_q_norm_g, dil_k_norm_g=dil_k_norm_g,
                  mem_q_norm_g=mem_q_norm_g, mem_k_norm_g=mem_k_norm_g, w_mem_kv=w_mem_kv,
                  w_branch_moba=w_branch_moba, w_branch_dil=w_branch_dil, w_branch_mem=w_branch_mem,
                  w_out=w_out, ffn_norm_g=ffn_norm_g, w_ffn_up=w_ffn_up, ffn_conv_w=ffn_conv_w,
                  ffn_conv_b=ffn_conv_b, w_ffn_down=w_ffn_down)
    B = x.shape[0]
    depth = w_in.shape[0]
    outs = []
    for b in range(B):
        xb = x[b]
        for l in range(depth):
            xb = _layer(xb, mem[b], positions[b], {k: v[l] for k, v in params.items()})
        outs.append(xb)
    return jnp.stack(outs, axis=0)
```

```python
import functools

import numpy as np
import jax
import jax.numpy as jnp
from jax import lax
from jax.experimental import pallas as pl
from jax.experimental.pallas import tpu as pltpu

D_MODEL = 1024
HEAD_DIM = 64
HALF = HEAD_DIM // 2
N_MOBA_HEADS = 6
N_DIL_HEADS = 6
N_MEM_HEADS = 4
N_MEM = 256
MOBA_BLOCK = 256
MOBA_TOPK = 3
DIL_PATTERNS = ((128, 1), (512, 4), (2048, 16))
DIL_BLOCK = 128
D_FF = 2816
CONV_WIDTH = 3
ROPE_THETA = 10000.0
EPS = 1e-6
MOBA_W = N_MOBA_HEADS * HEAD_DIM
DIL_W = N_DIL_HEADS * HEAD_DIM
MEM_W = N_MEM_HEADS * HEAD_DIM
QKV_COLS = 3 * MOBA_W + 3 * DIL_W + MEM_W
SCALE = HEAD_DIM ** -0.5

LANES = 128
MXU_TILE = 256
VMEM_LIMIT = 56 * 1024 * 1024

TM_PROJ = 512
TM_MERGE = 512
TM_FFN = 512
FFN_CHUNK = 256
DIL_ROWS = 1024

F32 = jnp.float32
BF16 = jnp.bfloat16
NEG_INF = float("-inf")


def _dot(a, b):
    return jnp.dot(a, b, preferred_element_type=F32)


def _dot_nt(a, b):
    return lax.dot_general(a, b, (((1,), (1,)), ((), ())), preferred_element_type=F32)


def _rms_rows(x, g):
    ms = jnp.mean(x * x, axis=-1, keepdims=True)
    return x * lax.rsqrt(ms + EPS) * g


def _params(*sem):
    return pltpu.CompilerParams(dimension_semantics=sem, vmem_limit_bytes=VMEM_LIMIT)


def _rope_kernel(pos_ref, inv_ref, cos_t_ref, sin_t_ref, cos_n_ref, sin_n_ref):
    pos = pos_ref[...].astype(F32)
    ang = inv_ref[...] * pos
    c = jnp.cos(ang)
    s = jnp.sin(ang)
    cos_t_ref[...] = jnp.concatenate([c, c], axis=0)
    sin_t_ref[...] = jnp.concatenate([-s, s], axis=0)
    cos_n_ref[...] = jnp.concatenate([c, c, c, c], axis=0).T
    sin_n_ref[...] = jnp.concatenate([-s, s, -s, s], axis=0).T


def _rope_tables(positions, S):
    tm = 512
    inv = (ROPE_THETA ** (-jnp.arange(HALF, dtype=F32) / HALF)).reshape(HALF, 1)
    return pl.pallas_call(
        _rope_kernel,
        grid=(S // tm,),
        in_specs=[pl.BlockSpec((1, tm), lambda i: (0, i)),
                  pl.BlockSpec((HALF, 1), lambda i: (0, 0))],
        out_specs=[pl.BlockSpec((HEAD_DIM, tm), lambda i: (0, i)),
                   pl.BlockSpec((HEAD_DIM, tm), lambda i: (0, i)),
                   pl.BlockSpec((tm, LANES), lambda i: (i, 0)),
                   pl.BlockSpec((tm, LANES), lambda i: (i, 0))],
        out_shape=[jax.ShapeDtypeStruct((HEAD_DIM, S), F32),
                   jax.ShapeDtypeStruct((HEAD_DIM, S), F32),
                   jax.ShapeDtypeStruct((S, LANES), F32),
                   jax.ShapeDtypeStruct((S, LANES), F32)],
        compiler_params=_params("parallel"),
        name="rope_tables",
    )(positions.reshape(1, S), inv)


def _memkv_kernel(mem_ref, g_ref, w_ref, gk_ref, gmat_ref, mk_ref, mv_ref):
    h = _rms_rows(mem_ref[...], g_ref[...]).astype(BF16)
    kv = _dot(h, w_ref[...])
    k = kv[:, :MEM_W]
    ms = _dot((k * k).astype(BF16), gmat_ref[...])
    mk_ref[...] = (k * lax.rsqrt(ms + EPS) * gk_ref[...]).astype(BF16)
    mv_ref[...] = kv[:, MEM_W:].astype(BF16)


def _mem_kv(mem, g, w_kv, gk_tiled, gmat):
    full = lambda shape: pl.BlockSpec(shape, lambda i: (0,) * len(shape))
    return pl.pallas_call(
        _memkv_kernel,
        grid=(1,),
        in_specs=[full((N_MEM, D_MODEL)), full((1, D_MODEL)), full((D_MODEL, 2 * MEM_W)),
                  full((1, MEM_W)), full((MXU_TILE, MXU_TILE))],
        out_specs=[full((N_MEM, MEM_W)), full((N_MEM, MEM_W))],
        out_shape=[jax.ShapeDtypeStruct((N_MEM, MEM_W), BF16)] * 2,
        compiler_params=_params("arbitrary"),
        name="mem_kv",
    )(mem, g, w_kv, gk_tiled, gmat)


_NAT_NORM = 3 * 384 + MEM_W
_NAT_NORM_PAD = 1536
_NAT_ROPE = 3 * 384
_NAT_COLS = _NAT_NORM + DIL_W


def _proj_kernel(x_ref, g_ref, wt_ref, wn_ref, gq_ref, gnat_ref, gmat_ref,
                 cos_t_ref, sin_t_ref, cos_n_ref, sin_n_ref,
                 qat_ref, vat_ref, ka_ref, km_ref, qm_ref, *rest):
    dil_refs, stage_ref = rest[:-1], rest[-1]
    tm = x_ref.shape[0]
    h = _rms_rows(x_ref[...], g_ref[...]).astype(BF16)

    yt = _dot_nt(wt_ref[...], h)
    cos_t = cos_t_ref[...]
    sin_t = sin_t_ref[...]
    gq = gq_ref[...]
    for hd in range(N_MOBA_HEADS):
        q = yt[hd * HEAD_DIM:(hd + 1) * HEAD_DIM]
        ms = jnp.mean(q * q, axis=0, keepdims=True)
        q = q * lax.rsqrt(ms + EPS) * gq
        rot = jnp.concatenate([q[HALF:], q[:HALF]], axis=0)
        q = (q * cos_t + rot * sin_t) * SCALE
        for b in range(tm // MOBA_BLOCK):
            qat_ref[b, hd * HEAD_DIM:(hd + 1) * HEAD_DIM, :] = (
                q[:, b * MOBA_BLOCK:(b + 1) * MOBA_BLOCK].astype(BF16))
    for b in range(tm // MOBA_BLOCK):
        vat_ref[b] = yt[MOBA_W:, b * MOBA_BLOCK:(b + 1) * MOBA_BLOCK].astype(BF16)

    gmat = gmat_ref[...]
    lane = lax.broadcasted_iota(jnp.int32, (tm, LANES), 1)
    first_half = (lane % HEAD_DIM) < HALF
    cos_n = cos_n_ref[...]
    sin_n = sin_n_ref[...]
    pairs = DIL_W // LANES

    def emit_dilated(ti, tiles):
        for lt in range(pairs):
            stage_ref[ti * pairs + lt] = tiles[lt]
        for pi, (_, d) in enumerate(DIL_PATTERNS):
            ref = dil_refs[3 * pi + ti]
            for r in range(d):
                for lt in range(pairs):
                    col = r * DIL_W + lt * LANES
                    ref[:, col:col + LANES] = stage_ref[
                        ti * pairs + lt, pl.ds(r, tm // d, stride=d), :].astype(BF16)

    chunks = []
    for c in range(_NAT_NORM_PAD // MXU_TILE):
        if c % 2 == 0:
            wide = _dot(h, wn_ref[:, c * MXU_TILE:(c + 2) * MXU_TILE])
        raw = wide[:, (c % 2) * MXU_TILE:(c % 2 + 1) * MXU_TILE]
        ms = _dot((raw * raw).astype(BF16), gmat)
        blk = raw * lax.rsqrt(ms + EPS) * gnat_ref[:, c * MXU_TILE:(c + 1) * MXU_TILE]
        for hc in range(MXU_TILE // LANES):
            col = c * MXU_TILE + hc * LANES
            if col >= _NAT_NORM:
                chunks.append(raw[:, hc * LANES:(hc + 1) * LANES])
                continue
            v = blk[:, hc * LANES:(hc + 1) * LANES]
            if col < _NAT_ROPE:
                rot = jnp.where(first_half, pltpu.roll(v, LANES - HALF, axis=1),
                                pltpu.roll(v, HALF, axis=1))
                v = v * cos_n + rot * sin_n
            chunks.append(v)
        if len(chunks) >= pairs and c == 1:
            ka = jnp.concatenate(chunks[0:pairs], axis=1)
            for b in range(tm // MOBA_BLOCK):
                km_ref[b] = jnp.mean(ka[b * MOBA_BLOCK:(b + 1) * MOBA_BLOCK], axis=0, keepdims=True)
            ka_ref[...] = ka.astype(BF16)
        if c == 2:
            emit_dilated(0, [t * SCALE for t in chunks[pairs:2 * pairs]])
        if c == 4:
            emit_dilated(1, chunks[2 * pairs:3 * pairs])
    qm_ref[...] = (jnp.concatenate(chunks[3 * pairs:3 * pairs + MEM_W // LANES], axis=1) * SCALE).astype(BF16)
    vd_rest = _dot(h, wn_ref[:, _NAT_NORM_PAD:])
    emit_dilated(2, [chunks[-1]] + [vd_rest[:, t * LANES:(t + 1) * LANES] for t in range(pairs - 1)])


def _proj(x, g_mix, wt, wn, gq_col, gnat, gmat, cos_t, sin_t, cos_n, sin_n, S):
    tm = TM_PROJ
    nb = S // MOBA_BLOCK
    bpt = tm // MOBA_BLOCK
    const = lambda shape: pl.BlockSpec(shape, lambda i: (0,) * len(shape))
    rows = lambda w: pl.BlockSpec((tm, w), lambda i: (i, 0))
    dils = [d for _, d in DIL_PATTERNS for _ in range(3)]
    outs = pl.pallas_call(
        _proj_kernel,
        grid=(S // tm,),
        in_specs=[rows(D_MODEL), const((1, D_MODEL)), const((2 * MOBA_W, D_MODEL)),
                  const((D_MODEL, _NAT_COLS)), const((HEAD_DIM, 1)), const((1, _NAT_NORM_PAD)),
                  const((MXU_TILE, MXU_TILE)),
                  pl.BlockSpec((HEAD_DIM, tm), lambda i: (0, i)),
                  pl.BlockSpec((HEAD_DIM, tm), lambda i: (0, i)),
                  rows(LANES), rows(LANES)],
        out_specs=[pl.BlockSpec((bpt, MOBA_W, MOBA_BLOCK), lambda i: (i, 0, 0)),
                   pl.BlockSpec((bpt, MOBA_W, MOBA_BLOCK), lambda i: (i, 0, 0)),
                   rows(MOBA_W),
                   pl.BlockSpec((bpt, 1, MOBA_W), lambda i: (i, 0, 0)),
                   rows(MEM_W)]
                  + [pl.BlockSpec((tm // d, d * DIL_W), lambda i: (i, 0)) for d in dils],
        out_shape=[jax.ShapeDtypeStruct((nb, MOBA_W, MOBA_BLOCK), BF16),
                   jax.ShapeDtypeStruct((nb, MOBA_W, MOBA_BLOCK), BF16),
                   jax.ShapeDtypeStruct((S, MOBA_W), BF16),
                   jax.ShapeDtypeStruct((nb, 1, MOBA_W), F32),
                   jax.ShapeDtypeStruct((S, MEM_W), BF16)]
                  + [jax.ShapeDtypeStruct((S // d, d * DIL_W), BF16) for d in dils],
        scratch_shapes=[pltpu.VMEM((3 * DIL_W // LANES, tm, LANES), F32)],
        compiler_params=_params("parallel"),
        name="in_proj",
    )(x, g_mix, wt, wn, gq_col, gnat, gmat, cos_t, sin_t, cos_n, sin_n)
    return outs[:5], [outs[5 + 3 * k:8 + 3 * k] for k in range(len(DIL_PATTERNS))]


_SHIFT_HEADROOM = 30.0
_FAST_BOUND = 50.0
_MOBA_UNROLL = 4


def _moba_kernel(bound_ref, qt_ref, k_ref, vt_ref, km_ref, o_ref, bias_ref, ot_ref):
    i = pl.program_id(1)
    tq = MOBA_BLOCK
    unroll = _MOBA_UNROLL
    qt = qt_ref[0]
    row = lax.broadcasted_iota(jnp.int32, (LANES, tq), 0)
    km = km_ref[...]
    nb = km.shape[0]
    blk = lax.broadcasted_iota(jnp.int32, (nb, tq), 0)

    bound = bound_ref[0]
    fast = bound <= _FAST_BOUND
    sel_bias = jnp.where(fast, _SHIFT_HEADROOM - bound, 0.0)

    km1 = km.astype(BF16)
    r1 = km - km1.astype(F32)
    km2 = r1.astype(BF16)
    km3 = (r1 - km2.astype(F32)).astype(BF16)
    km_parts = jnp.concatenate([km1, km2, km3], axis=0)
    qs = []
    for hh in range(2):
        q_h = jnp.where((row // HEAD_DIM) == hh, qt, jnp.zeros_like(qt))
        qs.append(q_h)
        g3 = _dot(km_parts, q_h)
        gate = g3[:nb] + g3[nb:2 * nb] + g3[2 * nb:]
        gate = jnp.where(blk < i, gate, NEG_INF)
        bias = jnp.full((nb, tq), NEG_INF, F32)
        for _ in range(MOBA_TOPK):
            m = jnp.max(gate, axis=0, keepdims=True)
            idx = jnp.min(jnp.where(gate == m, blk, nb), axis=0, keepdims=True)
            hit = blk == idx
            bias = jnp.where(hit & (m > NEG_INF), sel_bias, bias)
            gate = jnp.where(hit, NEG_INF, gate)
        bias_ref[hh] = bias

    kpos = lax.broadcasted_iota(jnp.int32, (tq, tq), 0)
    qpos = lax.broadcasted_iota(jnp.int32, (tq, tq), 1)
    causal = kpos <= qpos

    def k_rows(j):
        return k_ref[pl.ds(pl.multiple_of(j * tq, tq), tq), :]

    def v_rows(j, hh):
        return vt_ref[j, hh * HEAD_DIM:(hh + 1) * HEAD_DIM, :]

    ones = jnp.ones((16, tq), BF16)

    def values(j, hh, p):
        return _dot(jnp.concatenate([v_rows(j, hh), ones], axis=0), p)

    own_accs = []
    for hh in range(2):
        s = jnp.where(causal, _dot(k_rows(i), qs[hh]) + sel_bias, NEG_INF)
        own_accs.append(values(i, hh, jnp.exp(s).astype(BF16)))

    @pl.when(fast)
    def _():
        accs = own_accs

        def body(t, accs):
            out = list(accs)
            chains = [(unroll * t + u, hh) for u in range(unroll) for hh in range(2)]
            ss = [_dot(k_rows(j), qs[hh]) for j, hh in chains]
            ps = [jnp.exp(s + bias_ref[hh, pl.ds(j, 1), :]).astype(BF16) for s, (j, hh) in zip(ss, chains)]
            for p, (j, hh) in zip(ps, chains):
                out[hh] = out[hh] + values(j, hh, p)
            return tuple(out)

        accs = lax.fori_loop(0, (i + unroll - 1) // unroll, body, tuple(accs))
        ot_ref[...] = jnp.concatenate(
            [a[:HEAD_DIM] / a[HEAD_DIM:HEAD_DIM + 1] for a in accs], axis=0)

    @pl.when(jnp.logical_not(fast))
    def _():
        k_own = k_rows(i)
        carry = []
        for hh in range(2):
            s = jnp.where(causal, _dot(k_own, qs[hh]), NEG_INF)
            m = jnp.max(s, axis=0, keepdims=True)
            p = jnp.exp(s - m)
            l = jnp.sum(p, axis=0, keepdims=True)
            carry += [m, l, _dot(v_rows(i, hh), p.astype(BF16))]

        def body(j, carry):
            k_j = k_ref[pl.ds(pl.multiple_of(j * tq, tq), tq), :]
            out = []
            for hh in range(2):
                m, l, acc = carry[3 * hh:3 * hh + 3]
                s = _dot(k_j, qs[hh]) + bias_ref[hh, pl.ds(j, 1), :]
                m_new = jnp.maximum(m, jnp.max(s, axis=0, keepdims=True))
                alpha = jnp.exp(m - m_new)
                p = jnp.exp(s - m_new)
                l = alpha * l + jnp.sum(p, axis=0, keepdims=True)
                acc = alpha * acc + _dot(v_rows(j, hh), p.astype(BF16))
                out += [m_new, l, acc]
            return tuple(out)

        carry = lax.fori_loop(0, i, body, tuple(carry))
        ot_ref[...] = jnp.concatenate([carry[2] / carry[1], carry[5] / carry[4]], axis=0)

    o_ref[...] = ot_ref[...].T.astype(BF16)


def _moba(bound, qat, ka, vat, kmean, S):
    nb = S // MOBA_BLOCK
    assert nb % _MOBA_UNROLL == 0
    pairs = MOBA_W // LANES
    return pl.pallas_call(
        _moba_kernel,
        grid=(pairs, nb),
        in_specs=[pl.BlockSpec(memory_space=pltpu.SMEM),
                  pl.BlockSpec((1, LANES, MOBA_BLOCK), lambda p, i: (i, p, 0)),
                  pl.BlockSpec((S, LANES), lambda p, i: (0, p)),
                  pl.BlockSpec((nb, LANES, MOBA_BLOCK), lambda p, i: (0, p, 0)),
                  pl.BlockSpec((nb, LANES), lambda p, i: (0, p))],
        out_specs=pl.BlockSpec((MOBA_BLOCK, LANES), lambda p, i: (i, p)),
        out_shape=jax.ShapeDtypeStruct((S, MOBA_W), BF16),
        scratch_shapes=[pltpu.VMEM((2, nb, MOBA_BLOCK), F32), pltpu.VMEM((LANES, MOBA_BLOCK), F32)],
        compiler_params=_params("arbitrary", "arbitrary"),
        name="moba_attn",
    )(bound, qat, ka, vat, kmean)


def _dil_kernel(span, q_ref, kp_ref, kc_ref, vp_ref, vc_ref, o_ref, lse_ref):
    n = pl.program_id(2)
    blk = DIL_BLOCK
    nsub = q_ref.shape[0] // blk
    lane = lax.broadcasted_iota(jnp.int32, (blk, LANES), 1)
    lo_half = lane < HEAD_DIM
    qi = lax.broadcasted_iota(jnp.int32, (blk, 2 * blk), 0) + blk
    kj = lax.broadcasted_iota(jnp.int32, (blk, 2 * blk), 1)
    dist = qi - kj
    band = jnp.where((dist >= 0) & (dist <= span), 0.0, NEG_INF)
    band_first = jnp.where(kj >= blk, band, NEG_INF)
    for b in range(nsub):
        q = q_ref[b * blk:(b + 1) * blk, :]
        if b == 0:
            kk = jnp.concatenate([kp_ref[...], kc_ref[0:blk, :]], axis=0)
            vv = jnp.concatenate([vp_ref[...], vc_ref[0:blk, :]], axis=0)
            mask = jnp.where(n > 0, band, band_first)
        else:
            kk = kc_ref[(b - 1) * blk:(b + 1) * blk, :]
            vv = vc_ref[(b - 1) * blk:(b + 1) * blk, :]
            mask = band
        outs, lses = [], []
        for hh in range(2):
            q_h = jnp.where(lo_half if hh == 0 else ~lo_half, q, jnp.zeros_like(q))
            s = _dot_nt(q_h, kk) + mask
            m = jnp.max(s, axis=-1, keepdims=True)
            p = jnp.exp(s - m)
            den = jnp.sum(p, axis=-1, keepdims=True)
            outs.append(_dot(p.astype(BF16), vv) / den)
            lses.append(jnp.broadcast_to(m + jnp.log(den), (blk, LANES)))
        o_ref[b * blk:(b + 1) * blk, :] = jnp.where(lo_half, outs[0], outs[1]).astype(BF16)
        lse_ref[b * blk:(b + 1) * blk, :] = jnp.where(lo_half, lses[0], lses[1])


def _dilated(qd, kd, vd, window, dil, S):
    span = window // dil
    L = S // dil
    rows = min(DIL_ROWS, L)
    pairs = DIL_W // LANES
    sub = rows // DIL_BLOCK
    cur = pl.BlockSpec((rows, LANES), lambda r, p, n: (n, r * pairs + p))
    prev = pl.BlockSpec((DIL_BLOCK, LANES), lambda r, p, n: (jnp.maximum(n * sub - 1, 0), r * pairs + p))
    return pl.pallas_call(
        functools.partial(_dil_kernel, span),
        grid=(dil, pairs, L // rows),
        in_specs=[cur, prev, cur, prev, cur],
        out_specs=[cur, cur],
        out_shape=[jax.ShapeDtypeStruct((L, dil * DIL_W), BF16),
                   jax.ShapeDtypeStruct((L, dil * DIL_W), F32)],
        compiler_params=_params("parallel", "parallel", "arbitrary"),
        name=f"dilated_attn_d{dil}",
    )(qd, kd, kd, vd, vd)


def _merge_kernel(x_ref, g_ref, wg_ref, oa_ref, od1_ref, od2_ref, od3_ref, l1_ref, l2_ref, l3_ref,
                  qm_ref, mk_ref, mv_ref, wba_ref, wbd_ref, wbm_ref, wo_ref, out_ref, nat_ref):
    x = x_ref[...]
    tm = x.shape[0]
    h = _rms_rows(x, g_ref[...]).astype(BF16)

    def natural(ref, dil, slot):
        if dil == 1:
            return ref[...].astype(F32)
        tiles = DIL_W // LANES
        for r in range(dil):
            for lt in range(tiles):
                col = r * DIL_W + lt * LANES
                nat_ref[slot * tiles + lt, pl.ds(r, tm // dil, stride=dil), :] = (
                    ref[:, col:col + LANES].astype(F32))
        return jnp.concatenate([nat_ref[slot * tiles + lt] for lt in range(tiles)], axis=1)

    dils = [d for _, d in DIL_PATTERNS]
    l1, l2, l3 = [natural(ref, d, k) for k, (ref, d) in enumerate(zip((l1_ref, l2_ref, l3_ref), dils))]
    o1, o2, o3 = [natural(ref, d, 3 + k) for k, (ref, d) in enumerate(zip((od1_ref, od2_ref, od3_ref), dils))]
    lmax = jnp.maximum(jnp.maximum(l1, l2), l3)
    e1, e2, e3 = jnp.exp(l1 - lmax), jnp.exp(l2 - lmax), jnp.exp(l3 - lmax)
    o_d = (e1 * o1 + e2 * o2 + e3 * o3) / (e1 + e2 + e3)

    lane = lax.broadcasted_iota(jnp.int32, (tm, LANES), 1)
    lo_half = lane < HEAD_DIM
    o_m = []
    for pr in range(MEM_W // LANES):
        q = qm_ref[:, pr * LANES:(pr + 1) * LANES]
        mk = mk_ref[:, pr * LANES:(pr + 1) * LANES]
        mv = mv_ref[:, pr * LANES:(pr + 1) * LANES]
        outs = []
        for hh in range(2):
            q_h = jnp.where(lo_half if hh == 0 else ~lo_half, q, jnp.zeros_like(q))
            s = _dot_nt(q_h, mk)
            p = jnp.exp(s - jnp.max(s, axis=-1, keepdims=True))
            p = p / jnp.sum(p, axis=-1, keepdims=True)
            outs.append(_dot(p.astype(BF16), mv))
        o_m.append(jnp.where(lo_half, outs[0], outs[1]))
    o_m = jnp.concatenate(o_m, axis=1)

    merged = None
    branches = ((oa_ref[...], wba_ref), (o_d.astype(BF16), wbd_ref), (o_m.astype(BF16), wbm_ref))
    for bi, (o_b, w_ref) in enumerate(branches):
        gate = jax.nn.sigmoid(_dot(h, wg_ref[:, bi * D_MODEL:(bi + 1) * D_MODEL]))
        term = gate * _dot(o_b, w_ref[...])
        merged = term if merged is None else merged + term
    out_ref[...] = x + _dot(merged.astype(BF16), wo_ref[...])


def _merge(x, g_mix, w_gate, o_a, o_ds, lses, qm, mk, mv, w_ba, w_bd, w_bm, w_out, S):
    tm = TM_MERGE
    const = lambda shape: pl.BlockSpec(shape, lambda i: (0,) * len(shape), pipeline_mode=pl.Buffered(1))
    rows = lambda w: pl.BlockSpec((tm, w), lambda i: (i, 0))
    dil_rows = [pl.BlockSpec((tm // d, d * DIL_W), lambda i: (i, 0)) for _, d in DIL_PATTERNS]
    return pl.pallas_call(
        _merge_kernel,
        grid=(S // tm,),
        in_specs=[rows(D_MODEL), const((1, D_MODEL)), const((D_MODEL, 3 * D_MODEL)),
                  rows(MOBA_W), *dil_rows, *dil_rows,
                  rows(MEM_W), const((N_MEM, MEM_W)), const((N_MEM, MEM_W)),
                  const((MOBA_W, D_MODEL)), const((DIL_W, D_MODEL)), const((MEM_W, D_MODEL)),
                  const((D_MODEL, D_MODEL))],
        out_specs=rows(D_MODEL),
        out_shape=jax.ShapeDtypeStruct((S, D_MODEL), F32),
        scratch_shapes=[pltpu.VMEM((2 * len(DIL_PATTERNS) * DIL_W // LANES, tm, LANES), F32)],
        compiler_params=_params("parallel"),
        name="gated_merge",
    )(x, g_mix, w_gate, o_a, *o_ds, *lses, qm, mk, mv, w_ba, w_bd, w_bm, w_out)


def _ffn_kernel(x_ref, g_ref, wup_ref, cw_ref, cb_ref, wdn_ref, out_ref, halo_ref, act_ref):
    i = pl.program_id(0)
    tm = x_ref.shape[0]

    @pl.when(i == 0)
    def _():
        halo_ref[...] = jnp.zeros_like(halo_ref)

    x = x_ref[...]
    h = _rms_rows(x, g_ref[...]).astype(BF16)
    row8 = lax.broadcasted_iota(jnp.int32, (8, FFN_CHUNK), 0)

    def conv(u, col):
        prev = halo_ref[:, col:col + FFN_CHUNK]
        halo_ref[:, col:col + FFN_CHUNK] = u[tm - 8:, :]
        p6 = jnp.broadcast_to(prev[6:7], (8, FFN_CHUNK))
        p7 = jnp.broadcast_to(prev[7:8], (8, FFN_CHUNK))
        r1 = pltpu.roll(u, 1, axis=0)
        r2 = pltpu.roll(u, 2, axis=0)
        top1 = jnp.where(row8 == 0, p7, r1[:8])
        top2 = jnp.where(row8 == 0, p6, jnp.where(row8 == 1, p7, r2[:8]))
        u1 = jnp.concatenate([top1, r1[8:]], axis=0)
        u2 = jnp.concatenate([top2, r2[8:]], axis=0)
        w = cw_ref[:, col:col + FFN_CHUNK]
        return cb_ref[:, col:col + FFN_CHUNK] + w[0:1] * u2 + w[1:2] * u1 + w[2:3] * u

    for c in range(D_FF // FFN_CHUNK):
        cg = c * FFN_CHUNK
        cv = D_FF + c * FFN_CHUNK
        u_g = conv(_dot(h, wup_ref[:, cg:cg + FFN_CHUNK]), cg)
        u_v = conv(_dot(h, wup_ref[:, cv:cv + FFN_CHUNK]), cv)
        act_ref[:, cg:cg + FFN_CHUNK] = (jax.nn.silu(u_g) * u_v).astype(BF16)
    out_ref[...] = x + _dot(act_ref[...], wdn_ref[...])


def _ffn(x, g_ffn, w_up, conv_w, conv_b, w_down, S):
    tm = TM_FFN
    const = lambda shape: pl.BlockSpec(shape, lambda i: (0,) * len(shape), pipeline_mode=pl.Buffered(1))
    rows = pl.BlockSpec((tm, D_MODEL), lambda i: (i, 0))
    return pl.pallas_call(
        _ffn_kernel,
        grid=(S // tm,),
        in_specs=[rows, const((1, D_MODEL)), const((D_MODEL, 2 * D_FF)),
                  const((CONV_WIDTH, 2 * D_FF)), const((1, 2 * D_FF)), const((D_FF, D_MODEL))],
        out_specs=rows,
        out_shape=jax.ShapeDtypeStruct((S, D_MODEL), F32),
        scratch_shapes=[pltpu.VMEM((8, 2 * D_FF), F32), pltpu.VMEM((tm, D_FF), BF16)],
        compiler_params=_params("arbitrary"),
        name="conv_ffn",
    )(x, g_ffn, w_up, conv_w, conv_b, w_down)


def _group_mean_matrix():
    g = np.arange(MXU_TILE) // HEAD_DIM
    return jnp.asarray((g[:, None] == g[None, :]).astype(np.float32) / HEAD_DIM, dtype=BF16)


def _layer(x, mem, positions, p):
    S = x.shape[0]
    assert S % (max(d for _, d in DIL_PATTERNS) * DIL_BLOCK) == 0 and S % TM_PROJ == 0
    row = lambda v: v.reshape(1, -1).astype(F32)
    c = np.cumsum([0, MOBA_W, MOBA_W, MOBA_W, DIL_W, DIL_W, DIL_W, MEM_W])
    w_in = p["w_in"]
    seg = lambda k: w_in[:, c[k]:c[k + 1]]
    wt = jnp.concatenate([seg(0), seg(2)], axis=1).T.astype(BF16)
    wn = jnp.concatenate([seg(1), seg(3), seg(4), seg(6), seg(5)], axis=1).astype(BF16)
    w_gate = w_in[:, QKV_COLS:].astype(BF16)
    gnat = jnp.concatenate([jnp.tile(p["moba_k_norm_g"], N_MOBA_HEADS), jnp.tile(p["dil_q_norm_g"], N_DIL_HEADS),
                            jnp.tile(p["dil_k_norm_g"], N_DIL_HEADS), jnp.tile(p["mem_q_norm_g"], N_MEM_HEADS),
                            jnp.ones((_NAT_NORM_PAD - _NAT_NORM,), F32)]).reshape(1, _NAT_NORM_PAD)
    gmat = _group_mean_matrix()

    cos_t, sin_t, cos_n, sin_n = _rope_tables(positions, S)
    mk, mv = _mem_kv(mem, row(p["mem_norm_g"]), p["w_mem_kv"].astype(BF16),
                     row(jnp.tile(p["mem_k_norm_g"], N_MEM_HEADS)), gmat)
    (qat, vat, ka, kmean, qm), dil_qkv = _proj(
        x, row(p["mix_norm_g"]), wt, wn, p["moba_q_norm_g"].reshape(HEAD_DIM, 1).astype(F32), gnat, gmat,
        cos_t, sin_t, cos_n, sin_n, S)
    bound = (1.02 * HEAD_DIM * SCALE * jnp.max(jnp.abs(p["moba_q_norm_g"]))
             * jnp.max(jnp.abs(p["moba_k_norm_g"]))).reshape(1).astype(F32)
    o_a = _moba(bound, qat, ka, vat, kmean.reshape(S // MOBA_BLOCK, MOBA_W), S)
    o_ds, lses = zip(*[_dilated(*qkv, w, d, S) for qkv, (w, d) in zip(dil_qkv, DIL_PATTERNS)])
    x1 = _merge(x, row(p["mix_norm_g"]), w_gate, o_a, o_ds, lses, qm, mk, mv,
                p["w_branch_moba"].astype(BF16), p["w_branch_dil"].astype(BF16),
                p["w_branch_mem"].astype(BF16), p["w_out"].astype(BF16), S)
    return _ffn(x1, row(p["ffn_norm_g"]), p["w_ffn_up"].astype(BF16), p["ffn_conv_w"].astype(F32),
                row(p["ffn_conv_b"]), p["w_ffn_down"].astype(BF16), S)


def kernel(x, mem, positions, mix_norm_g, mem_norm_g, w_in, moba_q_norm_g, moba_k_norm_g, dil_q_norm_g, dil_k_norm_g, mem_q_norm_g, mem_k_norm_g, w_mem_kv, w_branch_moba, w_branch_dil, w_branch_mem, w_out, ffn_norm_g, w_ffn_up, ffn_conv_w, ffn_conv_b, w_ffn_down):
    params = dict(mix_norm_g=mix_norm_g, mem_norm_g=mem_norm_g, w_in=w_in, moba_q_norm_g=moba_q_norm_g,
                  moba_k_norm_g=moba_k_norm_g, dil_q_norm_g=dil_q_norm_g, dil_k_norm_g=dil_k_norm_g,
                  mem_q_norm_g=mem_q_norm_g, mem_k_norm_g=mem_k_norm_g, w_mem_kv=w_mem_kv,
                  w_branch_moba=w_branch_moba, w_branch_dil=w_branch_dil, w_branch_mem=w_branch_mem,
                  w_out=w_out, ffn_norm_g=ffn_norm_g, w_ffn_up=w_ffn_up, ffn_conv_w=ffn_conv_w,
                  ffn_conv_b=ffn_conv_b, w_ffn_down=w_ffn_down)
    B = x.shape[0]
    depth = w_in.shape[0]
    outs = []
    for b in range(B):
        xb = x[b]
        for l in range(depth):
            xb = _layer(xb, mem[b], positions[b], {k: v[l] for k, v in params.items()})
        outs.append(xb)
    return jnp.stack(outs, axis=0)
```

```python
import functools

import numpy as np
import jax
import jax.numpy as jnp
from jax import lax
from jax.experimental import pallas as pl
from jax.experimental.pallas import tpu as pltpu

D_MODEL = 1024
HEAD_DIM = 64
HALF = HEAD_DIM // 2
N_MOBA_HEADS = 6
N_DIL_HEADS = 6
N_MEM_HEADS = 4
N_MEM = 256
MOBA_BLOCK = 256
MOBA_TOPK = 3
DIL_PATTERNS = ((128, 1), (512, 4), (2048, 16))
DIL_BLOCK = 128
D_FF = 2816
CONV_WIDTH = 3
ROPE_THETA = 10000.0
EPS = 1e-6
MOBA_W = N_MOBA_HEADS * HEAD_DIM
DIL_W = N_DIL_HEADS * HEAD_DIM
MEM_W = N_MEM_HEADS * HEAD_DIM
QKV_COLS = 3 * MOBA_W + 3 * DIL_W + MEM_W
SCALE = HEAD_DIM ** -0.5

LANES = 128
MXU_TILE = 256
VMEM_LIMIT = 56 * 1024 * 1024

TM_PROJ = 512
TM_MERGE = 512
TM_FFN = 512
FFN_CHUNK = 256
DIL_ROWS = 1024

F32 = jnp.float32
BF16 = jnp.bfloat16
NEG_INF = float("-inf")


def _dot(a, b):
    return jnp.dot(a, b, preferred_element_type=F32)


def _dot_nt(a, b):
    return lax.dot_general(a, b, (((1,), (1,)), ((), ())), preferred_element_type=F32)


def _rms_rows(x, g):
    ms = jnp.mean(x * x, axis=-1, keepdims=True)
    return x * lax.rsqrt(ms + EPS) * g


def _params(*sem):
    return pltpu.CompilerParams(dimension_semantics=sem, vmem_limit_bytes=VMEM_LIMIT)


def _rope_kernel(pos_ref, inv_ref, cos_t_ref, sin_t_ref, cos_n_ref, sin_n_ref):
    pos = pos_ref[...].astype(F32)
    ang = inv_ref[...] * pos
    c = jnp.cos(ang)
    s = jnp.sin(ang)
    cos_t_ref[...] = jnp.concatenate([c, c], axis=0)
    sin_t_ref[...] = jnp.concatenate([-s, s], axis=0)
    cos_n_ref[...] = jnp.concatenate([c, c, c, c], axis=0).T
    sin_n_ref[...] = jnp.concatenate([-s, s, -s, s], axis=0).T


def _rope_tables(positions, S):
    tm = 512
    inv = (ROPE_THETA ** (-jnp.arange(HALF, dtype=F32) / HALF)).reshape(HALF, 1)
    return pl.pallas_call(
        _rope_kernel,
        grid=(S // tm,),
        in_specs=[pl.BlockSpec((1, tm), lambda i: (0, i)),
                  pl.BlockSpec((HALF, 1), lambda i: (0, 0))],
        out_specs=[pl.BlockSpec((HEAD_DIM, tm), lambda i: (0, i)),
                   pl.BlockSpec((HEAD_DIM, tm), lambda i: (0, i)),
                   pl.BlockSpec((tm, LANES), lambda i: (i, 0)),
                   pl.BlockSpec((tm, LANES), lambda i: (i, 0))],
        out_shape=[jax.ShapeDtypeStruct((HEAD_DIM, S), F32),
                   jax.ShapeDtypeStruct((HEAD_DIM, S), F32),
                   jax.ShapeDtypeStruct((S, LANES), F32),
                   jax.ShapeDtypeStruct((S, LANES), F32)],
        compiler_params=_params("parallel"),
        name="rope_tables",
    )(positions.reshape(1, S), inv)


def _memkv_kernel(mem_ref, g_ref, w_ref, gk_ref, gmat_ref, mk_ref, mv_ref):
    h = _rms_rows(mem_ref[...], g_ref[...]).astype(BF16)
    kv = _dot(h, w_ref[...])
    k = kv[:, :MEM_W]
    ms = _dot((k * k).astype(BF16), gmat_ref[...])
    mk_ref[...] = (k * lax.rsqrt(ms + EPS) * gk_ref[...]).astype(BF16)
    mv_ref[...] = kv[:, MEM_W:].astype(BF16)


def _mem_kv(mem, g, w_kv, gk_tiled, gmat):
    full = lambda shape: pl.BlockSpec(shape, lambda i: (0,) * len(shape))
    return pl.pallas_call(
        _memkv_kernel,
        grid=(1,),
        in_specs=[full((N_MEM, D_MODEL)), full((1, D_MODEL)), full((D_MODEL, 2 * MEM_W)),
                  full((1, MEM_W)), full((MXU_TILE, MXU_TILE))],
        out_specs=[full((N_MEM, MEM_W)), full((N_MEM, MEM_W))],
        out_shape=[jax.ShapeDtypeStruct((N_MEM, MEM_W), BF16)] * 2,
        compiler_params=_params("arbitrary"),
        name="mem_kv",
    )(mem, g, w_kv, gk_tiled, gmat)


_NAT_NORM = 3 * 384 + MEM_W
_NAT_NORM_PAD = 1536
_NAT_ROPE = 3 * 384
_NAT_COLS = _NAT_NORM + DIL_W


def _proj_kernel(x_ref, g_ref, wt_ref, wn_ref, gq_ref, gnat_ref, gmat_ref,
                 cos_t_ref, sin_t_ref, cos_n_ref, sin_n_ref,
                 qat_ref, vat_ref, ka_ref, km_ref, qm_ref, *rest):
    dil_refs, stage_ref = rest[:-1], rest[-1]
    tm = x_ref.shape[0]
    h = _rms_rows(x_ref[...], g_ref[...]).astype(BF16)

    yt = _dot_nt(wt_ref[...], h)
    cos_t = cos_t_ref[...]
    sin_t = sin_t_ref[...]
    gq = gq_ref[...]
    for hd in range(N_MOBA_HEADS):
        q = yt[hd * HEAD_DIM:(hd + 1) * HEAD_DIM]
        ms = jnp.mean(q * q, axis=0, keepdims=True)
        q = q * lax.rsqrt(ms + EPS) * gq
        rot = jnp.concatenate([q[HALF:], q[:HALF]], axis=0)
        q = (q * cos_t + rot * sin_t) * SCALE
        for b in range(tm // MOBA_BLOCK):
            qat_ref[b, hd * HEAD_DIM:(hd + 1) * HEAD_DIM, :] = (
                q[:, b * MOBA_BLOCK:(b + 1) * MOBA_BLOCK].astype(BF16))
    for b in range(tm // MOBA_BLOCK):
        vat_ref[b] = yt[MOBA_W:, b * MOBA_BLOCK:(b + 1) * MOBA_BLOCK].astype(BF16)

    gmat = gmat_ref[...]
    lane = lax.broadcasted_iota(jnp.int32, (tm, LANES), 1)
    first_half = (lane % HEAD_DIM) < HALF
    cos_n = cos_n_ref[...]
    sin_n = sin_n_ref[...]
    pairs = DIL_W // LANES

    def emit_dilated(ti, tiles):
        for lt in range(pairs):
            stage_ref[ti * pairs + lt] = tiles[lt]
        for pi, (_, d) in enumerate(DIL_PATTERNS):
            ref = dil_refs[3 * pi + ti]
            for r in range(d):
                for lt in range(pairs):
                    col = r * DIL_W + lt * LANES
                    ref[:, col:col + LANES] = stage_ref[
                        ti * pairs + lt, pl.ds(r, tm // d, stride=d), :].astype(BF16)

    chunks = []
    for c in range(_NAT_NORM_PAD // MXU_TILE):
        if c % 2 == 0:
            wide = _dot(h, wn_ref[:, c * MXU_TILE:(c + 2) * MXU_TILE])
        raw = wide[:, (c % 2) * MXU_TILE:(c % 2 + 1) * MXU_TILE]
        ms = _dot((raw * raw).astype(BF16), gmat)
        blk = raw * lax.rsqrt(ms + EPS) * gnat_ref[:, c * MXU_TILE:(c + 1) * MXU_TILE]
        for hc in range(MXU_TILE // LANES):
            col = c * MXU_TILE + hc * LANES
            if col >= _NAT_NORM:
                chunks.append(raw[:, hc * LANES:(hc + 1) * LANES])
                continue
            v = blk[:, hc * LANES:(hc + 1) * LANES]
            if col < _NAT_ROPE:
                rot = jnp.where(first_half, pltpu.roll(v, LANES - HALF, axis=1),
                                pltpu.roll(v, HALF, axis=1))
                v = v * cos_n + rot * sin_n
            chunks.append(v)
        if len(chunks) >= pairs and c == 1:
            ka = jnp.concatenate(chunks[0:pairs], axis=1)
            for b in range(tm // MOBA_BLOCK):
                km_ref[b] = jnp.mean(ka[b * MOBA_BLOCK:(b + 1) * MOBA_BLOCK], axis=0, keepdims=True)
            ka_ref[...] = ka.astype(BF16)
        if c == 2:
            emit_dilated(0, [t * SCALE for t in chunks[pairs:2 * pairs]])
        if c == 4:
            emit_dilated(1, chunks[2 * pairs:3 * pairs])
    qm_ref[...] = (jnp.concatenate(chunks[3 * pairs:3 * pairs + MEM_W // LANES], axis=1) * SCALE).astype(BF16)
    vd_rest = _dot(h, wn_ref[:, _NAT_NORM_PAD:])
    emit_dilated(2, [chunks[-1]] + [vd_rest[:, t * LANES:(t + 1) * LANES] for t in range(pairs - 1)])


def _proj(x, g_mix, wt, wn, gq_col, gnat, gmat, cos_t, sin_t, cos_n, sin_n, S):
    tm = TM_PROJ
    nb = S // MOBA_BLOCK
    bpt = tm // MOBA_BLOCK
    const = lambda shape: pl.BlockSpec(shape, lambda i: (0,) * len(shape))
    rows = lambda w: pl.BlockSpec((tm, w), lambda i: (i, 0))
    dils = [d for _, d in DIL_PATTERNS for _ in range(3)]
    outs = pl.pallas_call(
        _proj_kernel,
        grid=(S // tm,),
        in_specs=[rows(D_MODEL), const((1, D_MODEL)), const((2 * MOBA_W, D_MODEL)),
                  const((D_MODEL, _NAT_COLS)), const((HEAD_DIM, 1)), const((1, _NAT_NORM_PAD)),
                  const((MXU_TILE, MXU_TILE)),
                  pl.BlockSpec((HEAD_DIM, tm), lambda i: (0, i)),
                  pl.BlockSpec((HEAD_DIM, tm), lambda i: (0, i)),
                  rows(LANES), rows(LANES)],
        out_specs=[pl.BlockSpec((bpt, MOBA_W, MOBA_BLOCK), lambda i: (i, 0, 0)),
                   pl.BlockSpec((bpt, MOBA_W, MOBA_BLOCK), lambda i: (i, 0, 0)),
                   rows(MOBA_W),
                   pl.BlockSpec((bpt, 1, MOBA_W), lambda i: (i, 0, 0)),
                   rows(MEM_W)]
                  + [pl.BlockSpec((tm // d, d * DIL_W), lambda i: (i, 0)) for d in dils],
        out_shape=[jax.ShapeDtypeStruct((nb, MOBA_W, MOBA_BLOCK), BF16),
                   jax.ShapeDtypeStruct((nb, MOBA_W, MOBA_BLOCK), BF16),
                   jax.ShapeDtypeStruct((S, MOBA_W), BF16),
                   jax.ShapeDtypeStruct((nb, 1, MOBA_W), F32),
                   jax.ShapeDtypeStruct((S, MEM_W), BF16)]
                  + [jax.ShapeDtypeStruct((S // d, d * DIL_W), BF16) for d in dils],
        scratch_shapes=[pltpu.VMEM((3 * DIL_W // LANES, tm, LANES), F32)],
        compiler_params=_params("parallel"),
        name="in_proj",
    )(x, g_mix, wt, wn, gq_col, gnat, gmat, cos_t, sin_t, cos_n, sin_n)
    return outs[:5], [outs[5 + 3 * k:8 + 3 * k] for k in range(len(DIL_PATTERNS))]


_SHIFT_HEADROOM = 30.0
_FAST_BOUND = 50.0
_MOBA_UNROLL = 4


def _moba_kernel(bound_ref, qt_ref, k_ref, vt_ref, km_ref, o_ref, bias_ref, ot_ref):
    i = pl.program_id(1)
    tq = MOBA_BLOCK
    unroll = _MOBA_UNROLL
    qt = qt_ref[0]
    row = lax.broadcasted_iota(jnp.int32, (LANES, tq), 0)
    km = km_ref[...]
    nb = km.shape[0]
    blk = lax.broadcasted_iota(jnp.int32, (nb, tq), 0)

    bound = bound_ref[0]
    fast = bound <= _FAST_BOUND
    sel_bias = jnp.where(fast, _SHIFT_HEADROOM - bound, 0.0)

    km1 = km.astype(BF16)
    r1 = km - km1.astype(F32)
    km2 = r1.astype(BF16)
    km3 = (r1 - km2.astype(F32)).astype(BF16)
    km_parts = jnp.concatenate([km1, km2, km3], axis=0)
    qs = []
    for hh in range(2):
        q_h = jnp.where((row // HEAD_DIM) == hh, qt, jnp.zeros_like(qt))
        qs.append(q_h)
        g3 = _dot(km_parts, q_h)
        gate = g3[:nb] + g3[nb:2 * nb] + g3[2 * nb:]
        gate = jnp.where(blk < i, gate, NEG_INF)
        bias = jnp.full((nb, tq), NEG_INF, F32)
        for _ in range(MOBA_TOPK):
            m = jnp.max(gate, axis=0, keepdims=True)
            idx = jnp.min(jnp.where(gate == m, blk, nb), axis=0, keepdims=True)
            hit = blk == idx
            bias = jnp.where(hit & (m > NEG_INF), sel_bias, bias)
            gate = jnp.where(hit, NEG_INF, gate)
        bias_ref[hh] = bias

    kpos = lax.broadcasted_iota(jnp.int32, (tq, tq), 0)
    qpos = lax.broadcasted_iota(jnp.int32, (tq, tq), 1)
    causal = kpos <= qpos

    def k_rows(j):
        return k_ref[pl.ds(pl.multiple_of(j * tq, tq), tq), :]

    def v_rows(j, hh):
        return vt_ref[j, hh * HEAD_DIM:(hh + 1) * HEAD_DIM, :]

    ones = jnp.ones((16, tq), BF16)

    def values(j, hh, p):
        return _dot(jnp.concatenate([v_rows(j, hh), ones], axis=0), p)

    own_accs = []
    for hh in range(2):
        s = jnp.where(causal, _dot(k_rows(i), qs[hh]) + sel_bias, NEG_INF)
        own_accs.append(values(i, hh, jnp.exp(s).astype(BF16)))

    @pl.when(fast)
    def _():
        accs = own_accs

        def body(t, accs):
            out = list(accs)
            chains = [(unroll * t + u, hh) for u in range(unroll) for hh in range(2)]
            ss = [_dot(k_rows(j), qs[hh]) for j, hh in chains]
            ps = [jnp.exp(s + bias_ref[hh, pl.ds(j, 1), :]).astype(BF16) for s, (j, hh) in zip(ss, chains)]
            for p, (j, hh) in zip(ps, chains):
                out[hh] = out[hh] + values(j, hh, p)
            return tuple(out)

        accs = lax.fori_loop(0, (i + unroll - 1) // unroll, body, tuple(accs))
        ot_ref[...] = jnp.concatenate(
            [a[:HEAD_DIM] / a[HEAD_DIM:HEAD_DIM + 1] for a in accs], axis=0)

    @pl.when(jnp.logical_not(fast))
    def _():
        k_own = k_rows(i)
        carry = []
        for hh in range(2):
            s = jnp.where(causal, _dot(k_own, qs[hh]), NEG_INF)
            m = jnp.max(s, axis=0, keepdims=True)
            p = jnp.exp(s - m)
            l = jnp.sum(p, axis=0, keepdims=True)
            carry += [m, l, _dot(v_rows(i, hh), p.astype(BF16))]

        def body(j, carry):
            k_j = k_ref[pl.ds(pl.multiple_of(j * tq, tq), tq), :]
            out = []
            for hh in range(2):
                m, l, acc = carry[3 * hh:3 * hh + 3]
                s = _dot(k_j, qs[hh]) + bias_ref[hh, pl.ds(j, 1), :]
                m_new = jnp.maximum(m, jnp.max(s, axis=0, keepdims=True))
                alpha = jnp.exp(m - m_new)
                p = jnp.exp(s - m_new)
                l = alpha * l + jnp.sum(p, axis=0, keepdims=True)
                acc = alpha * acc + _dot(v_rows(j, hh), p.astype(BF16))
                out += [m_new, l, acc]
            return tuple(out)

        carry = lax.fori_loop(0, i, body, tuple(carry))
        ot_ref[...] = jnp.concatenate([carry[2] / carry[1], carry[5] / carry[4]], axis=0)

    o_ref[...] = ot_ref[...].T.astype(BF16)


def _moba(bound, qat, ka, vat, kmean, S):
    nb = S // MOBA_BLOCK
    assert nb % _MOBA_UNROLL == 0
    pairs = MOBA_W // LANES
    return pl.pallas_call(
        _moba_kernel,
        grid=(pairs, nb),
        in_specs=[pl.BlockSpec(memory_space=pltpu.SMEM),
                  pl.BlockSpec((1, LANES, MOBA_BLOCK), lambda p, i: (i, p, 0)),
                  pl.BlockSpec((S, LANES), lambda p, i: (0, p)),
                  pl.BlockSpec((nb, LANES, MOBA_BLOCK), lambda p, i: (0, p, 0)),
                  pl.BlockSpec((nb, LANES), lambda p, i: (0, p))],
        out_specs=pl.BlockSpec((MOBA_BLOCK, LANES), lambda p, i: (i, p)),
        out_shape=jax.ShapeDtypeStruct((S, MOBA_W), BF16),
        scratch_shapes=[pltpu.VMEM((2, nb, MOBA_BLOCK), F32), pltpu.VMEM((LANES, MOBA_BLOCK), F32)],
        compiler_params=_params("arbitrary", "arbitrary"),
        name="moba_attn",
    )(bound, qat, ka, vat, kmean)


def _dil_kernel(span, bound_ref, q_ref, kp_ref, kc_ref, vp_ref, vc_ref, o_ref, lse_ref):
    n = pl.program_id(2)
    blk = DIL_BLOCK
    nsub = q_ref.shape[0] // blk
    lane = lax.broadcasted_iota(jnp.int32, (blk, LANES), 1)
    lo_half = lane < HEAD_DIM
    qi = lax.broadcasted_iota(jnp.int32, (2 * blk, 2 * blk), 0) % blk + blk
    kj = lax.broadcasted_iota(jnp.int32, (2 * blk, 2 * blk), 1)
    dist = qi - kj
    band = jnp.where((dist >= 0) & (dist <= span), 0.0, NEG_INF)
    band_first = jnp.where(kj >= blk, band, NEG_INF)
    bound = bound_ref[0]
    fast = bound <= _FAST_BOUND
    shift = bound - _SHIFT_HEADROOM

    def blocks(fixed_shift):
        for b in range(nsub):
            q = q_ref[b * blk:(b + 1) * blk, :]
            if b == 0:
                kk = jnp.concatenate([kp_ref[...], kc_ref[0:blk, :]], axis=0)
                vv = jnp.concatenate([vp_ref[...], vc_ref[0:blk, :]], axis=0)
                mask = jnp.where(n > 0, band, band_first)
            else:
                kk = kc_ref[(b - 1) * blk:(b + 1) * blk, :]
                vv = vc_ref[(b - 1) * blk:(b + 1) * blk, :]
                mask = band
            zero = jnp.zeros_like(q)
            q2 = jnp.concatenate([jnp.where(lo_half, q, zero), jnp.where(lo_half, zero, q)], axis=0)
            s = _dot_nt(q2, kk) + mask
            if fixed_shift:
                p = jnp.exp(s - shift).astype(BF16)
                ov = _dot(p, jnp.concatenate([vv, jnp.ones_like(vv)], axis=1))
                den = ov[:, LANES:]
                o2 = ov[:, :LANES] / den
                lse2 = shift + jnp.log(den)
            else:
                m = jnp.max(s, axis=-1, keepdims=True)
                p = jnp.exp(s - m)
                den = jnp.sum(p, axis=-1, keepdims=True)
                o2 = _dot(p.astype(BF16), vv) / den
                lse2 = jnp.broadcast_to(m + jnp.log(den), (2 * blk, LANES))
            o_ref[b * blk:(b + 1) * blk, :] = jnp.where(lo_half, o2[:blk], o2[blk:]).astype(BF16)
            lse_ref[b * blk:(b + 1) * blk, :] = jnp.where(lo_half, lse2[:blk], lse2[blk:])

    pl.when(fast)(lambda: blocks(True))
    pl.when(jnp.logical_not(fast))(lambda: blocks(False))


def _dilated(bound, qd, kd, vd, window, dil, S):
    span = window // dil
    L = S // dil
    rows = min(DIL_ROWS, L)
    pairs = DIL_W // LANES
    sub = rows // DIL_BLOCK
    cur = pl.BlockSpec((rows, LANES), lambda r, p, n: (n, r * pairs + p))
    prev = pl.BlockSpec((DIL_BLOCK, LANES), lambda r, p, n: (jnp.maximum(n * sub - 1, 0), r * pairs + p))
    return pl.pallas_call(
        functools.partial(_dil_kernel, span),
        grid=(dil, pairs, L // rows),
        in_specs=[pl.BlockSpec(memory_space=pltpu.SMEM), cur, prev, cur, prev, cur],
        out_specs=[cur, cur],
        out_shape=[jax.ShapeDtypeStruct((L, dil * DIL_W), BF16),
                   jax.ShapeDtypeStruct((L, dil * DIL_W), F32)],
        compiler_params=_params("parallel", "parallel", "arbitrary"),
        name=f"dilated_attn_d{dil}",
    )(bound, qd, kd, kd, vd, vd)


def _merge_kernel(x_ref, g_ref, wg_ref, oa_ref, od1_ref, od2_ref, od3_ref, l1_ref, l2_ref, l3_ref,
                  qm_ref, mk_ref, mv_ref, wba_ref, wbd_ref, wbm_ref, wo_ref, out_ref, nat_ref):
    x = x_ref[...]
    tm = x.shape[0]
    h = _rms_rows(x, g_ref[...]).astype(BF16)

    def natural(ref, dil, slot):
        if dil == 1:
            return ref[...].astype(F32)
        tiles = DIL_W // LANES
        for r in range(dil):
            for lt in range(tiles):
                col = r * DIL_W + lt * LANES
                nat_ref[slot * tiles + lt, pl.ds(r, tm // dil, stride=dil), :] = (
                    ref[:, col:col + LANES].astype(F32))
        return jnp.concatenate([nat_ref[slot * tiles + lt] for lt in range(tiles)], axis=1)

    dils = [d for _, d in DIL_PATTERNS]
    l1, l2, l3 = [natural(ref, d, k) for k, (ref, d) in enumerate(zip((l1_ref, l2_ref, l3_ref), dils))]
    o1, o2, o3 = [natural(ref, d, 3 + k) for k, (ref, d) in enumerate(zip((od1_ref, od2_ref, od3_ref), dils))]
    lmax = jnp.maximum(jnp.maximum(l1, l2), l3)
    e1, e2, e3 = jnp.exp(l1 - lmax), jnp.exp(l2 - lmax), jnp.exp(l3 - lmax)
    o_d = (e1 * o1 + e2 * o2 + e3 * o3) / (e1 + e2 + e3)

    lane = lax.broadcasted_iota(jnp.int32, (tm, LANES), 1)
    lo_half = lane < HEAD_DIM
    o_m = []
    for pr in range(MEM_W // LANES):
        q = qm_ref[:, pr * LANES:(pr + 1) * LANES]
        mk = mk_ref[:, pr * LANES:(pr + 1) * LANES]
        mv = mv_ref[:, pr * LANES:(pr + 1) * LANES]
        outs = []
        for hh in range(2):
            q_h = jnp.where(lo_half if hh == 0 else ~lo_half, q, jnp.zeros_like(q))
            s = _dot_nt(q_h, mk)
            p = jnp.exp(s - jnp.max(s, axis=-1, keepdims=True))
            p = p / jnp.sum(p, axis=-1, keepdims=True)
            outs.append(_dot(p.astype(BF16), mv))
        o_m.append(jnp.where(lo_half, outs[0], outs[1]))
    o_m = jnp.concatenate(o_m, axis=1)

    merged = None
    branches = ((oa_ref[...], wba_ref), (o_d.astype(BF16), wbd_ref), (o_m.astype(BF16), wbm_ref))
    for bi, (o_b, w_ref) in enumerate(branches):
        gate = jax.nn.sigmoid(_dot(h, wg_ref[:, bi * D_MODEL:(bi + 1) * D_MODEL]))
        term = gate * _dot(o_b, w_ref[...])
        merged = term if merged is None else merged + term
    out_ref[...] = x + _dot(merged.astype(BF16), wo_ref[...])


def _merge(x, g_mix, w_gate, o_a, o_ds, lses, qm, mk, mv, w_ba, w_bd, w_bm, w_out, S):
    tm = TM_MERGE
    const = lambda shape: pl.BlockSpec(shape, lambda i: (0,) * len(shape), pipeline_mode=pl.Buffered(1))
    rows = lambda w: pl.BlockSpec((tm, w), lambda i: (i, 0))
    dil_rows = [pl.BlockSpec((tm // d, d * DIL_W), lambda i: (i, 0)) for _, d in DIL_PATTERNS]
    return pl.pallas_call(
        _merge_kernel,
        grid=(S // tm,),
        in_specs=[rows(D_MODEL), const((1, D_MODEL)), const((D_MODEL, 3 * D_MODEL)),
                  rows(MOBA_W), *dil_rows, *dil_rows,
                  rows(MEM_W), const((N_MEM, MEM_W)), const((N_MEM, MEM_W)),
                  const((MOBA_W, D_MODEL)), const((DIL_W, D_MODEL)), const((MEM_W, D_MODEL)),
                  const((D_MODEL, D_MODEL))],
        out_specs=rows(D_MODEL),
        out_shape=jax.ShapeDtypeStruct((S, D_MODEL), F32),
        scratch_shapes=[pltpu.VMEM((2 * len(DIL_PATTERNS) * DIL_W // LANES, tm, LANES), F32)],
        compiler_params=_params("parallel"),
        name="gated_merge",
    )(x, g_mix, w_gate, o_a, *o_ds, *lses, qm, mk, mv, w_ba, w_bd, w_bm, w_out)


def _ffn_kernel(x_ref, g_ref, wup_ref, cw_ref, cb_ref, wdn_ref, out_ref, halo_ref, act_ref):
    i = pl.program_id(0)
    tm = x_ref.shape[0]

    @pl.when(i == 0)
    def _():
        halo_ref[...] = jnp.zeros_like(halo_ref)

    x = x_ref[...]
    h = _rms_rows(x, g_ref[...]).astype(BF16)
    row8 = lax.broadcasted_iota(jnp.int32, (8, FFN_CHUNK), 0)

    def conv(u, col):
        prev = halo_ref[:, col:col + FFN_CHUNK]
        halo_ref[:, col:col + FFN_CHUNK] = u[tm - 8:, :]
        p6 = jnp.broadcast_to(prev[6:7], (8, FFN_CHUNK))
        p7 = jnp.broadcast_to(prev[7:8], (8, FFN_CHUNK))
        r1 = pltpu.roll(u, 1, axis=0)
        r2 = pltpu.roll(u, 2, axis=0)
        top1 = jnp.where(row8 == 0, p7, r1[:8])
        top2 = jnp.where(row8 == 0, p6, jnp.where(row8 == 1, p7, r2[:8]))
        u1 = jnp.concatenate([top1, r1[8:]], axis=0)
        u2 = jnp.concatenate([top2, r2[8:]], axis=0)
        w = cw_ref[:, col:col + FFN_CHUNK]
        return cb_ref[:, col:col + FFN_CHUNK] + w[0:1] * u2 + w[1:2] * u1 + w[2:3] * u

    for c in range(D_FF // FFN_CHUNK):
        cg = c * FFN_CHUNK
        cv = D_FF + c * FFN_CHUNK
        u_g = conv(_dot(h, wup_ref[:, cg:cg + FFN_CHUNK]), cg)
        u_v = conv(_dot(h, wup_ref[:, cv:cv + FFN_CHUNK]), cv)
        act_ref[:, cg:cg + FFN_CHUNK] = (jax.nn.silu(u_g) * u_v).astype(BF16)
    out_ref[...] = x + _dot(act_ref[...], wdn_ref[...])


def _ffn(x, g_ffn, w_up, conv_w, conv_b, w_down, S):
    tm = TM_FFN
    const = lambda shape: pl.BlockSpec(shape, lambda i: (0,) * len(shape), pipeline_mode=pl.Buffered(1))
    rows = pl.BlockSpec((tm, D_MODEL), lambda i: (i, 0))
    return pl.pallas_call(
        _ffn_kernel,
        grid=(S // tm,),
        in_specs=[rows, const((1, D_MODEL)), const((D_MODEL, 2 * D_FF)),
                  const((CONV_WIDTH, 2 * D_FF)), const((1, 2 * D_FF)), const((D_FF, D_MODEL))],
        out_specs=rows,
        out_shape=jax.ShapeDtypeStruct((S, D_MODEL), F32),
        scratch_shapes=[pltpu.VMEM((8, 2 * D_FF), F32), pltpu.VMEM((tm, D_FF), BF16)],
        compiler_params=_params("arbitrary"),
        name="conv_ffn",
    )(x, g_ffn, w_up, conv_w, conv_b, w_down)


def _group_mean_matrix():
    g = np.arange(MXU_TILE) // HEAD_DIM
    return jnp.asarray((g[:, None] == g[None, :]).astype(np.float32) / HEAD_DIM, dtype=BF16)


def _layer(x, mem, positions, p):
    S = x.shape[0]
    assert S % (max(d for _, d in DIL_PATTERNS) * DIL_BLOCK) == 0 and S % TM_PROJ == 0
    row = lambda v: v.reshape(1, -1).astype(F32)
    c = np.cumsum([0, MOBA_W, MOBA_W, MOBA_W, DIL_W, DIL_W, DIL_W, MEM_W])
    w_in = p["w_in"]
    seg = lambda k: w_in[:, c[k]:c[k + 1]]
    wt = jnp.concatenate([seg(0), seg(2)], axis=1).T.astype(BF16)
    wn = jnp.concatenate([seg(1), seg(3), seg(4), seg(6), seg(5)], axis=1).astype(BF16)
    w_gate = w_in[:, QKV_COLS:].astype(BF16)
    gnat = jnp.concatenate([jnp.tile(p["moba_k_norm_g"], N_MOBA_HEADS), jnp.tile(p["dil_q_norm_g"], N_DIL_HEADS),
                            jnp.tile(p["dil_k_norm_g"], N_DIL_HEADS), jnp.tile(p["mem_q_norm_g"], N_MEM_HEADS),
                            jnp.ones((_NAT_NORM_PAD - _NAT_NORM,), F32)]).reshape(1, _NAT_NORM_PAD)
    gmat = _group_mean_matrix()

    cos_t, sin_t, cos_n, sin_n = _rope_tables(positions, S)
    mk, mv = _mem_kv(mem, row(p["mem_norm_g"]), p["w_mem_kv"].astype(BF16),
                     row(jnp.tile(p["mem_k_norm_g"], N_MEM_HEADS)), gmat)
    (qat, vat, ka, kmean, qm), dil_qkv = _proj(
        x, row(p["mix_norm_g"]), wt, wn, p["moba_q_norm_g"].reshape(HEAD_DIM, 1).astype(F32), gnat, gmat,
        cos_t, sin_t, cos_n, sin_n, S)
    def score_bound(gq, gk):
        return (1.02 * HEAD_DIM * SCALE * jnp.max(jnp.abs(gq)) * jnp.max(jnp.abs(gk))).reshape(1).astype(F32)

    o_a = _moba(score_bound(p["moba_q_norm_g"], p["moba_k_norm_g"]), qat, ka, vat,
                kmean.reshape(S // MOBA_BLOCK, MOBA_W), S)
    bound_d = score_bound(p["dil_q_norm_g"], p["dil_k_norm_g"])
    o_ds, lses = zip(*[_dilated(bound_d, *qkv, w, d, S) for qkv, (w, d) in zip(dil_qkv, DIL_PATTERNS)])
    x1 = _merge(x, row(p["mix_norm_g"]), w_gate, o_a, o_ds, lses, qm, mk, mv,
                p["w_branch_moba"].astype(BF16), p["w_branch_dil"].astype(BF16),
                p["w_branch_mem"].astype(BF16), p["w_out"].astype(BF16), S)
    return _ffn(x1, row(p["ffn_norm_g"]), p["w_ffn_up"].astype(BF16), p["ffn_conv_w"].astype(F32),
                row(p["ffn_conv_b"]), p["w_ffn_down"].astype(BF16), S)


def kernel(x, mem, positions, mix_norm_g, mem_norm_g, w_in, moba_q_norm_g, moba_k_norm_g, dil_q_norm_g, dil_k_norm_g, mem_q_norm_g, mem_k_norm_g, w_mem_kv, w_branch_moba, w_branch_dil, w_branch_mem, w_out, ffn_norm_g, w_ffn_up, ffn_conv_w, ffn_conv_b, w_ffn_down):
    params = dict(mix_norm_g=mix_norm_g, mem_norm_g=mem_norm_g, w_in=w_in, moba_q_norm_g=moba_q_norm_g,
                  moba_k_norm_g=moba_k_norm_g, dil_q_norm_g=dil_q_norm_g, dil_k_norm_g=dil_k_norm_g,
                  mem_q_norm_g=mem_q_norm_g, mem_k_norm_g=mem_k_norm_g, w_mem_kv=w_mem_kv,
                  w_branch_moba=w_branch_moba, w_branch_dil=w_branch_dil, w_branch_mem=w_branch_mem,
                  w_out=w_out, ffn_norm_g=ffn_norm_g, w_ffn_up=w_ffn_up, ffn_conv_w=ffn_conv_w,
                  ffn_conv_b=ffn_conv_b, w_ffn_down=w_ffn_down)
    B = x.shape[0]
    depth = w_in.shape[0]
    outs = []
    for b in range(B):
        xb = x[b]
        for l in range(depth):
            xb = _layer(xb, mem[b], positions[b], {k: v[l] for k, v in params.items()})
        outs.append(xb)
    return jnp.stack(outs, axis=0)
```

```python
import functools

import numpy as np
import jax
import jax.numpy as jnp
from jax import lax
from jax.experimental import pallas as pl
from jax.experimental.pallas import tpu as pltpu

D_MODEL = 1024
HEAD_DIM = 64
HALF = HEAD_DIM // 2
N_MOBA_HEADS = 6
N_DIL_HEADS = 6
N_MEM_HEADS = 4
N_MEM = 256
MOBA_BLOCK = 256
MOBA_TOPK = 3
DIL_PATTERNS = ((128, 1), (512, 4), (2048, 16))
DIL_BLOCK = 128
D_FF = 2816
CONV_WIDTH = 3
ROPE_THETA = 10000.0
EPS = 1e-6
MOBA_W = N_MOBA_HEADS * HEAD_DIM
DIL_W = N_DIL_HEADS * HEAD_DIM
MEM_W = N_MEM_HEADS * HEAD_DIM
QKV_COLS = 3 * MOBA_W + 3 * DIL_W + MEM_W
SCALE = HEAD_DIM ** -0.5

LANES = 128
MXU_TILE = 256
VMEM_LIMIT = 56 * 1024 * 1024

TM_PROJ = 512
TM_MERGE = 512
TM_FFN = 512
FFN_CHUNK = 256
DIL_ROWS = 1024

F32 = jnp.float32
BF16 = jnp.bfloat16
NEG_INF = float("-inf")


def _dot(a, b):
    return jnp.dot(a, b, preferred_element_type=F32)


def _dot_nt(a, b):
    return lax.dot_general(a, b, (((1,), (1,)), ((), ())), preferred_element_type=F32)


def _rms_rows(x, g):
    ms = jnp.mean(x * x, axis=-1, keepdims=True)
    return x * lax.rsqrt(ms + EPS) * g


def _params(*sem):
    return pltpu.CompilerParams(dimension_semantics=sem, vmem_limit_bytes=VMEM_LIMIT)


def _rope_kernel(pos_ref, inv_ref, cos_t_ref, sin_t_ref, cos_n_ref, sin_n_ref):
    pos = pos_ref[...].astype(F32)
    ang = inv_ref[...] * pos
    c = jnp.cos(ang)
    s = jnp.sin(ang)
    cos_t_ref[...] = jnp.concatenate([c, c], axis=0)
    sin_t_ref[...] = jnp.concatenate([-s, s], axis=0)
    cos_n_ref[...] = jnp.concatenate([c, c, c, c], axis=0).T
    sin_n_ref[...] = jnp.concatenate([-s, s, -s, s], axis=0).T


def _rope_tables(positions, S):
    tm = 512
    inv = (ROPE_THETA ** (-jnp.arange(HALF, dtype=F32) / HALF)).reshape(HALF, 1)
    return pl.pallas_call(
        _rope_kernel,
        grid=(S // tm,),
        in_specs=[pl.BlockSpec((1, tm), lambda i: (0, i)),
                  pl.BlockSpec((HALF, 1), lambda i: (0, 0))],
        out_specs=[pl.BlockSpec((HEAD_DIM, tm), lambda i: (0, i)),
                   pl.BlockSpec((HEAD_DIM, tm), lambda i: (0, i)),
                   pl.BlockSpec((tm, LANES), lambda i: (i, 0)),
                   pl.BlockSpec((tm, LANES), lambda i: (i, 0))],
        out_shape=[jax.ShapeDtypeStruct((HEAD_DIM, S), F32),
                   jax.ShapeDtypeStruct((HEAD_DIM, S), F32),
                   jax.ShapeDtypeStruct((S, LANES), F32),
                   jax.ShapeDtypeStruct((S, LANES), F32)],
        compiler_params=_params("parallel"),
        name="rope_tables",
    )(positions.reshape(1, S), inv)


def _memkv_kernel(mem_ref, g_ref, w_ref, gk_ref, gmat_ref, mk_ref, mv_ref):
    h = _rms_rows(mem_ref[...], g_ref[...]).astype(BF16)
    kv = _dot(h, w_ref[...])
    k = kv[:, :MEM_W]
    ms = _dot((k * k).astype(BF16), gmat_ref[...])
    mk_ref[...] = (k * lax.rsqrt(ms + EPS) * gk_ref[...]).astype(BF16)
    mv_ref[...] = kv[:, MEM_W:].astype(BF16)


def _mem_kv(mem, g, w_kv, gk_tiled, gmat):
    full = lambda shape: pl.BlockSpec(shape, lambda i: (0,) * len(shape))
    return pl.pallas_call(
        _memkv_kernel,
        grid=(1,),
        in_specs=[full((N_MEM, D_MODEL)), full((1, D_MODEL)), full((D_MODEL, 2 * MEM_W)),
                  full((1, MEM_W)), full((MXU_TILE, MXU_TILE))],
        out_specs=[full((N_MEM, MEM_W)), full((N_MEM, MEM_W))],
        out_shape=[jax.ShapeDtypeStruct((N_MEM, MEM_W), BF16)] * 2,
        compiler_params=_params("arbitrary"),
        name="mem_kv",
    )(mem, g, w_kv, gk_tiled, gmat)


_NAT_NORM = 3 * 384 + MEM_W
_NAT_NORM_PAD = 1536
_NAT_ROPE = 3 * 384
_NAT_COLS = _NAT_NORM + DIL_W


def _proj_kernel(x_ref, g_ref, wt_ref, wn_ref, gq_ref, gnat_ref, gmat_ref,
                 cos_t_ref, sin_t_ref, cos_n_ref, sin_n_ref,
                 qat_ref, vat_ref, ka_ref, km_ref, qm_ref, *rest):
    dil_refs, stage_ref = rest[:-1], rest[-1]
    tm = x_ref.shape[0]
    h = _rms_rows(x_ref[...], g_ref[...]).astype(BF16)

    yt = _dot_nt(wt_ref[...], h)
    cos_t = cos_t_ref[...]
    sin_t = sin_t_ref[...]
    gq = gq_ref[...]
    for hd in range(N_MOBA_HEADS):
        q = yt[hd * HEAD_DIM:(hd + 1) * HEAD_DIM]
        ms = jnp.mean(q * q, axis=0, keepdims=True)
        q = q * lax.rsqrt(ms + EPS) * gq
        rot = jnp.concatenate([q[HALF:], q[:HALF]], axis=0)
        q = (q * cos_t + rot * sin_t) * SCALE
        for b in range(tm // MOBA_BLOCK):
            qat_ref[b, hd * HEAD_DIM:(hd + 1) * HEAD_DIM, :] = (
                q[:, b * MOBA_BLOCK:(b + 1) * MOBA_BLOCK].astype(BF16))
    for b in range(tm // MOBA_BLOCK):
        vat_ref[b] = yt[MOBA_W:, b * MOBA_BLOCK:(b + 1) * MOBA_BLOCK].astype(BF16)

    gmat = gmat_ref[...]
    lane = lax.broadcasted_iota(jnp.int32, (tm, LANES), 1)
    first_half = (lane % HEAD_DIM) < HALF
    cos_n = cos_n_ref[...]
    sin_n = sin_n_ref[...]
    pairs = DIL_W // LANES

    def emit_dilated(ti, tiles):
        for lt in range(pairs):
            stage_ref[ti * pairs + lt] = tiles[lt]
        for pi, (_, d) in enumerate(DIL_PATTERNS):
            ref = dil_refs[3 * pi + ti]
            for r in range(d):
                for lt in range(pairs):
                    col = r * DIL_W + lt * LANES
                    ref[:, col:col + LANES] = stage_ref[
                        ti * pairs + lt, pl.ds(r, tm // d, stride=d), :].astype(BF16)

    chunks = []
    for c in range(_NAT_NORM_PAD // MXU_TILE):
        if c % 2 == 0:
            wide = _dot(h, wn_ref[:, c * MXU_TILE:(c + 2) * MXU_TILE])
        raw = wide[:, (c % 2) * MXU_TILE:(c % 2 + 1) * MXU_TILE]
        ms = _dot((raw * raw).astype(BF16), gmat)
        blk = raw * lax.rsqrt(ms + EPS) * gnat_ref[:, c * MXU_TILE:(c + 1) * MXU_TILE]
        for hc in range(MXU_TILE // LANES):
            col = c * MXU_TILE + hc * LANES
            if col >= _NAT_NORM:
                chunks.append(raw[:, hc * LANES:(hc + 1) * LANES])
                continue
            v = blk[:, hc * LANES:(hc + 1) * LANES]
            if col < _NAT_ROPE:
                rot = jnp.where(first_half, pltpu.roll(v, LANES - HALF, axis=1),
                                pltpu.roll(v, HALF, axis=1))
                v = v * cos_n + rot * sin_n
            chunks.append(v)
        if len(chunks) >= pairs and c == 1:
            ka = jnp.concatenate(chunks[0:pairs], axis=1)
            for b in range(tm // MOBA_BLOCK):
                km_ref[b] = jnp.mean(ka[b * MOBA_BLOCK:(b + 1) * MOBA_BLOCK], axis=0, keepdims=True)
            ka_ref[...] = ka.astype(BF16)
        if c == 2:
            emit_dilated(0, [t * SCALE for t in chunks[pairs:2 * pairs]])
        if c == 4:
            emit_dilated(1, chunks[2 * pairs:3 * pairs])
    qm_ref[...] = (jnp.concatenate(chunks[3 * pairs:3 * pairs + MEM_W // LANES], axis=1) * SCALE).astype(BF16)
    vd_rest = _dot(h, wn_ref[:, _NAT_NORM_PAD:])
    emit_dilated(2, [chunks[-1]] + [vd_rest[:, t * LANES:(t + 1) * LANES] for t in range(pairs - 1)])


def _proj(x, g_mix, wt, wn, gq_col, gnat, gmat, cos_t, sin_t, cos_n, sin_n, S):
    tm = TM_PROJ
    nb = S // MOBA_BLOCK
    bpt = tm // MOBA_BLOCK
    const = lambda shape: pl.BlockSpec(shape, lambda i: (0,) * len(shape))
    rows = lambda w: pl.BlockSpec((tm, w), lambda i: (i, 0))
    dils = [d for _, d in DIL_PATTERNS for _ in range(3)]
    outs = pl.pallas_call(
        _proj_kernel,
        grid=(S // tm,),
        in_specs=[rows(D_MODEL), const((1, D_MODEL)), const((2 * MOBA_W, D_MODEL)),
                  const((D_MODEL, _NAT_COLS)), const((HEAD_DIM, 1)), const((1, _NAT_NORM_PAD)),
                  const((MXU_TILE, MXU_TILE)),
                  pl.BlockSpec((HEAD_DIM, tm), lambda i: (0, i)),
                  pl.BlockSpec((HEAD_DIM, tm), lambda i: (0, i)),
                  rows(LANES), rows(LANES)],
        out_specs=[pl.BlockSpec((bpt, MOBA_W, MOBA_BLOCK), lambda i: (i, 0, 0)),
                   pl.BlockSpec((bpt, MOBA_W, MOBA_BLOCK), lambda i: (i, 0, 0)),
                   rows(MOBA_W),
                   pl.BlockSpec((bpt, 1, MOBA_W), lambda i: (i, 0, 0)),
                   rows(MEM_W)]
                  + [pl.BlockSpec((tm // d, d * DIL_W), lambda i: (i, 0)) for d in dils],
        out_shape=[jax.ShapeDtypeStruct((nb, MOBA_W, MOBA_BLOCK), BF16),
                   jax.ShapeDtypeStruct((nb, MOBA_W, MOBA_BLOCK), BF16),
                   jax.ShapeDtypeStruct((S, MOBA_W), BF16),
                   jax.ShapeDtypeStruct((nb, 1, MOBA_W), F32),
                   jax.ShapeDtypeStruct((S, MEM_W), BF16)]
                  + [jax.ShapeDtypeStruct((S // d, d * DIL_W), BF16) for d in dils],
        scratch_shapes=[pltpu.VMEM((3 * DIL_W // LANES, tm, LANES), F32)],
        compiler_params=_params("parallel"),
        name="in_proj",
    )(x, g_mix, wt, wn, gq_col, gnat, gmat, cos_t, sin_t, cos_n, sin_n)
    return outs[:5], [outs[5 + 3 * k:8 + 3 * k] for k in range(len(DIL_PATTERNS))]


_SHIFT_HEADROOM = 30.0
_FAST_BOUND = 50.0
_MOBA_UNROLL = 4


def _moba_kernel(bound_ref, qt_ref, k_ref, vt_ref, km_ref, o_ref, bias_ref, ot_ref):
    i = pl.program_id(0)
    tq = MOBA_BLOCK
    unroll = _MOBA_UNROLL
    heads = N_MOBA_HEADS
    nb = km_ref.shape[0]
    row = lax.broadcasted_iota(jnp.int32, (LANES, tq), 0)
    blk = lax.broadcasted_iota(jnp.int32, (nb, tq), 0)

    bound = bound_ref[0]
    fast = bound <= _FAST_BOUND
    sel_bias = jnp.where(fast, _SHIFT_HEADROOM - bound, 0.0)

    def pair_cols(hd):
        return slice((hd // 2) * LANES, (hd // 2 + 1) * LANES)

    qs = []
    for hd in range(heads):
        qt = qt_ref[0, pair_cols(hd), :]
        q_h = jnp.where((row // HEAD_DIM) == hd % 2, qt, jnp.zeros_like(qt))
        qs.append(q_h)
        km = km_ref[:, pair_cols(hd)]
        km1 = km.astype(BF16)
        r1 = km - km1.astype(F32)
        km2 = r1.astype(BF16)
        km3 = (r1 - km2.astype(F32)).astype(BF16)
        g3 = _dot(jnp.concatenate([km1, km2, km3], axis=0), q_h)
        gate = g3[:nb] + g3[nb:2 * nb] + g3[2 * nb:]
        gate = jnp.where(blk < i, gate, NEG_INF)
        bias = jnp.full((nb, tq), NEG_INF, F32)
        for _ in range(MOBA_TOPK):
            m = jnp.max(gate, axis=0, keepdims=True)
            idx = jnp.min(jnp.where(gate == m, blk, nb), axis=0, keepdims=True)
            hit = blk == idx
            bias = jnp.where(hit & (m > NEG_INF), sel_bias, bias)
            gate = jnp.where(hit, NEG_INF, gate)
        bias_ref[hd] = bias

    kpos = lax.broadcasted_iota(jnp.int32, (tq, tq), 0)
    qpos = lax.broadcasted_iota(jnp.int32, (tq, tq), 1)
    causal = kpos <= qpos

    def k_rows(j, hd):
        return k_ref[pl.ds(pl.multiple_of(j * tq, tq), tq), pair_cols(hd)]

    def v_rows(j, hd):
        return vt_ref[j, hd * HEAD_DIM:(hd + 1) * HEAD_DIM, :]

    ones = jnp.ones((16, tq), BF16)

    def values(j, hd, p):
        return _dot(jnp.concatenate([v_rows(j, hd), ones], axis=0), p)

    own_accs = []
    for hd in range(heads):
        s = jnp.where(causal, _dot(k_rows(i, hd), qs[hd]) + sel_bias, NEG_INF)
        own_accs.append(values(i, hd, jnp.exp(s).astype(BF16)))

    @pl.when(fast)
    def _():
        def body(t, accs):
            out = list(accs)
            chains = [(unroll * t + u, hd) for u in range(unroll) for hd in range(heads)]
            ss = [_dot(k_rows(j, hd), qs[hd]) for j, hd in chains]
            ps = [jnp.exp(s + bias_ref[hd, pl.ds(j, 1), :]).astype(BF16) for s, (j, hd) in zip(ss, chains)]
            for p, (j, hd) in zip(ps, chains):
                out[hd] = out[hd] + values(j, hd, p)
            return tuple(out)

        accs = lax.fori_loop(0, (i + unroll - 1) // unroll, body, tuple(own_accs))
        for hd, a in enumerate(accs):
            ot_ref[hd * HEAD_DIM:(hd + 1) * HEAD_DIM, :] = a[:HEAD_DIM] / a[HEAD_DIM:HEAD_DIM + 1]

    @pl.when(jnp.logical_not(fast))
    def _():
        for pr in range(heads // 2):
            pair = (2 * pr, 2 * pr + 1)
            carry = []
            for hd in pair:
                s = jnp.where(causal, _dot(k_rows(i, hd), qs[hd]), NEG_INF)
                m = jnp.max(s, axis=0, keepdims=True)
                p = jnp.exp(s - m)
                l = jnp.sum(p, axis=0, keepdims=True)
                carry += [m, l, _dot(v_rows(i, hd), p.astype(BF16))]

            def body(j, carry):
                out = []
                for c, hd in enumerate(pair):
                    m, l, acc = carry[3 * c:3 * c + 3]
                    s = _dot(k_rows(j, hd), qs[hd]) + bias_ref[hd, pl.ds(j, 1), :]
                    m_new = jnp.maximum(m, jnp.max(s, axis=0, keepdims=True))
                    alpha = jnp.exp(m - m_new)
                    p = jnp.exp(s - m_new)
                    l = alpha * l + jnp.sum(p, axis=0, keepdims=True)
                    acc = alpha * acc + _dot(v_rows(j, hd), p.astype(BF16))
                    out += [m_new, l, acc]
                return tuple(out)

            carry = lax.fori_loop(0, i, body, tuple(carry))
            for c, hd in enumerate(pair):
                ot_ref[hd * HEAD_DIM:(hd + 1) * HEAD_DIM, :] = carry[3 * c + 2] / carry[3 * c + 1]

    o_ref[...] = ot_ref[...].T.astype(BF16)


def _moba(bound, qat, ka, vat, kmean, S):
    nb = S // MOBA_BLOCK
    assert nb % _MOBA_UNROLL == 0
    once = lambda shape: pl.BlockSpec(shape, lambda i: (0,) * len(shape), pipeline_mode=pl.Buffered(1))
    return pl.pallas_call(
        _moba_kernel,
        grid=(nb,),
        in_specs=[pl.BlockSpec(memory_space=pltpu.SMEM),
                  pl.BlockSpec((1, MOBA_W, MOBA_BLOCK), lambda i: (i, 0, 0)),
                  once((S, MOBA_W)), once((nb, MOBA_W, MOBA_BLOCK)), once((nb, MOBA_W))],
        out_specs=pl.BlockSpec((MOBA_BLOCK, MOBA_W), lambda i: (i, 0)),
        out_shape=jax.ShapeDtypeStruct((S, MOBA_W), BF16),
        scratch_shapes=[pltpu.VMEM((N_MOBA_HEADS, nb, MOBA_BLOCK), F32),
                        pltpu.VMEM((MOBA_W, MOBA_BLOCK), F32)],
        compiler_params=_params("arbitrary"),
        name="moba_attn",
    )(bound, qat, ka, vat, kmean)


def _dil_kernel(span, bound_ref, q_ref, kp_ref, kc_ref, vp_ref, vc_ref, o_ref, lse_ref):
    n = pl.program_id(1)
    blk = DIL_BLOCK
    nsub = q_ref.shape[0] // blk
    lane = lax.broadcasted_iota(jnp.int32, (blk, LANES), 1)
    lo_half = lane < HEAD_DIM
    qi = lax.broadcasted_iota(jnp.int32, (2 * blk, 2 * blk), 0) % blk + blk
    kj = lax.broadcasted_iota(jnp.int32, (2 * blk, 2 * blk), 1)
    dist = qi - kj
    band = jnp.where((dist >= 0) & (dist <= span), 0.0, NEG_INF)
    band_first = jnp.where(kj >= blk, band, NEG_INF)
    bound = bound_ref[0]
    fast = bound <= _FAST_BOUND
    shift = bound - _SHIFT_HEADROOM

    def blocks(fixed_shift):
        for pr in range(DIL_W // LANES):
            for b in range(nsub):
                block(fixed_shift, b, slice(pr * LANES, (pr + 1) * LANES))

    def block(fixed_shift, b, cols):
        q = q_ref[b * blk:(b + 1) * blk, cols]
        if b == 0:
            kk = jnp.concatenate([kp_ref[:, cols], kc_ref[0:blk, cols]], axis=0)
            vv = jnp.concatenate([vp_ref[:, cols], vc_ref[0:blk, cols]], axis=0)
            mask = jnp.where(n > 0, band, band_first)
        else:
            kk = kc_ref[(b - 1) * blk:(b + 1) * blk, cols]
            vv = vc_ref[(b - 1) * blk:(b + 1) * blk, cols]
            mask = band
        zero = jnp.zeros_like(q)
        q2 = jnp.concatenate([jnp.where(lo_half, q, zero), jnp.where(lo_half, zero, q)], axis=0)
        s = _dot_nt(q2, kk) + mask
        if fixed_shift:
            p = jnp.exp(s - shift).astype(BF16)
            ov = _dot(p, jnp.concatenate([vv, jnp.ones_like(vv)], axis=1))
            den = ov[:, LANES:]
            o2 = ov[:, :LANES] / den
            lse2 = shift + jnp.log(den)
        else:
            m = jnp.max(s, axis=-1, keepdims=True)
            p = jnp.exp(s - m)
            den = jnp.sum(p, axis=-1, keepdims=True)
            o2 = _dot(p.astype(BF16), vv) / den
            lse2 = jnp.broadcast_to(m + jnp.log(den), (2 * blk, LANES))
        o_ref[b * blk:(b + 1) * blk, cols] = jnp.where(lo_half, o2[:blk], o2[blk:]).astype(BF16)
        lse_ref[b * blk:(b + 1) * blk, cols] = jnp.where(lo_half, lse2[:blk], lse2[blk:])

    pl.when(fast)(lambda: blocks(True))
    pl.when(jnp.logical_not(fast))(lambda: blocks(False))


def _dilated(bound, qd, kd, vd, window, dil, S):
    span = window // dil
    L = S // dil
    rows = min(DIL_ROWS, L)
    sub = rows // DIL_BLOCK
    cur = pl.BlockSpec((rows, DIL_W), lambda r, n: (n, r))
    prev = pl.BlockSpec((DIL_BLOCK, DIL_W), lambda r, n: (jnp.maximum(n * sub - 1, 0), r))
    return pl.pallas_call(
        functools.partial(_dil_kernel, span),
        grid=(dil, L // rows),
        in_specs=[pl.BlockSpec(memory_space=pltpu.SMEM), cur, prev, cur, prev, cur],
        out_specs=[cur, cur],
        out_shape=[jax.ShapeDtypeStruct((L, dil * DIL_W), BF16),
                   jax.ShapeDtypeStruct((L, dil * DIL_W), F32)],
        compiler_params=_params("parallel", "arbitrary"),
        name=f"dilated_attn_d{dil}",
    )(bound, qd, kd, kd, vd, vd)


def _merge_kernel(x_ref, g_ref, wg_ref, oa_ref, od1_ref, od2_ref, od3_ref, l1_ref, l2_ref, l3_ref,
                  qm_ref, mk_ref, mv_ref, wba_ref, wbd_ref, wbm_ref, wo_ref, out_ref, nat_ref):
    x = x_ref[...]
    tm = x.shape[0]
    h = _rms_rows(x, g_ref[...]).astype(BF16)

    def natural(ref, dil, slot):
        if dil == 1:
            return ref[...].astype(F32)
        tiles = DIL_W // LANES
        for r in range(dil):
            for lt in range(tiles):
                col = r * DIL_W + lt * LANES
                nat_ref[slot * tiles + lt, pl.ds(r, tm // dil, stride=dil), :] = (
                    ref[:, col:col + LANES].astype(F32))
        return jnp.concatenate([nat_ref[slot * tiles + lt] for lt in range(tiles)], axis=1)

    dils = [d for _, d in DIL_PATTERNS]
    l1, l2, l3 = [natural(ref, d, k) for k, (ref, d) in enumerate(zip((l1_ref, l2_ref, l3_ref), dils))]
    o1, o2, o3 = [natural(ref, d, 3 + k) for k, (ref, d) in enumerate(zip((od1_ref, od2_ref, od3_ref), dils))]
    lmax = jnp.maximum(jnp.maximum(l1, l2), l3)
    e1, e2, e3 = jnp.exp(l1 - lmax), jnp.exp(l2 - lmax), jnp.exp(l3 - lmax)
    o_d = (e1 * o1 + e2 * o2 + e3 * o3) / (e1 + e2 + e3)

    lane = lax.broadcasted_iota(jnp.int32, (tm, LANES), 1)
    lo_half = lane < HEAD_DIM
    o_m = []
    for pr in range(MEM_W // LANES):
        q = qm_ref[:, pr * LANES:(pr + 1) * LANES]
        mk = mk_ref[:, pr * LANES:(pr + 1) * LANES]
        mv = mv_ref[:, pr * LANES:(pr + 1) * LANES]
        outs = []
        for hh in range(2):
            q_h = jnp.where(lo_half if hh == 0 else ~lo_half, q, jnp.zeros_like(q))
            s = _dot_nt(q_h, mk)
            p = jnp.exp(s - jnp.max(s, axis=-1, keepdims=True))
            p = p / jnp.sum(p, axis=-1, keepdims=True)
            outs.append(_dot(p.astype(BF16), mv))
        o_m.append(jnp.where(lo_half, outs[0], outs[1]))
    o_m = jnp.concatenate(o_m, axis=1)

    merged = None
    branches = ((oa_ref[...], wba_ref), (o_d.astype(BF16), wbd_ref), (o_m.astype(BF16), wbm_ref))
    for bi, (o_b, w_ref) in enumerate(branches):
        gate = jax.nn.sigmoid(_dot(h, wg_ref[:, bi * D_MODEL:(bi + 1) * D_MODEL]))
        term = gate * _dot(o_b, w_ref[...])
        merged = term if merged is None else merged + term
    out_ref[...] = x + _dot(merged.astype(BF16), wo_ref[...])


def _merge(x, g_mix, w_gate, o_a, o_ds, lses, qm, mk, mv, w_ba, w_bd, w_bm, w_out, S):
    tm = TM_MERGE
    const = lambda shape: pl.BlockSpec(shape, lambda i: (0,) * len(shape), pipeline_mode=pl.Buffered(1))
    rows = lambda w: pl.BlockSpec((tm, w), lambda i: (i, 0))
    dil_rows = [pl.BlockSpec((tm // d, d * DIL_W), lambda i: (i, 0)) for _, d in DIL_PATTERNS]
    return pl.pallas_call(
        _merge_kernel,
        grid=(S // tm,),
        in_specs=[rows(D_MODEL), const((1, D_MODEL)), const((D_MODEL, 3 * D_MODEL)),
                  rows(MOBA_W), *dil_rows, *dil_rows,
                  rows(MEM_W), const((N_MEM, MEM_W)), const((N_MEM, MEM_W)),
                  const((MOBA_W, D_MODEL)), const((DIL_W, D_MODEL)), const((MEM_W, D_MODEL)),
                  const((D_MODEL, D_MODEL))],
        out_specs=rows(D_MODEL),
        out_shape=jax.ShapeDtypeStruct((S, D_MODEL), F32),
        scratch_shapes=[pltpu.VMEM((2 * len(DIL_PATTERNS) * DIL_W // LANES, tm, LANES), F32)],
        compiler_params=_params("parallel"),
        name="gated_merge",
    )(x, g_mix, w_gate, o_a, *o_ds, *lses, qm, mk, mv, w_ba, w_bd, w_bm, w_out)


def _ffn_kernel(x_ref, g_ref, wup_ref, cw_ref, cb_ref, wdn_ref, out_ref, halo_ref, act_ref):
    i = pl.program_id(0)
    tm = x_ref.shape[0]

    @pl.when(i == 0)
    def _():
        halo_ref[...] = jnp.zeros_like(halo_ref)

    x = x_ref[...]
    h = _rms_rows(x, g_ref[...]).astype(BF16)
    row8 = lax.broadcasted_iota(jnp.int32, (8, FFN_CHUNK), 0)

    def conv(u, col):
        prev = halo_ref[:, col:col + FFN_CHUNK]
        halo_ref[:, col:col + FFN_CHUNK] = u[tm - 8:, :]
        p6 = jnp.broadcast_to(prev[6:7], (8, FFN_CHUNK))
        p7 = jnp.broadcast_to(prev[7:8], (8, FFN_CHUNK))
        r1 = pltpu.roll(u, 1, axis=0)
        r2 = pltpu.roll(u, 2, axis=0)
        top1 = jnp.where(row8 == 0, p7, r1[:8])
        top2 = jnp.where(row8 == 0, p6, jnp.where(row8 == 1, p7, r2[:8]))
        u1 = jnp.concatenate([top1, r1[8:]], axis=0)
        u2 = jnp.concatenate([top2, r2[8:]], axis=0)
        w = cw_ref[:, col:col + FFN_CHUNK]
        return cb_ref[:, col:col + FFN_CHUNK] + w[0:1] * u2 + w[1:2] * u1 + w[2:3] * u

    for c in range(D_FF // FFN_CHUNK):
        cg = c * FFN_CHUNK
        cv = D_FF + c * FFN_CHUNK
        u_g = conv(_dot(h, wup_ref[:, cg:cg + FFN_CHUNK]), cg)
        u_v = conv(_dot(h, wup_ref[:, cv:cv + FFN_CHUNK]), cv)
        act_ref[:, cg:cg + FFN_CHUNK] = (jax.nn.silu(u_g) * u_v).astype(BF16)
    out_ref[...] = x + _dot(act_ref[...], wdn_ref[...])


def _ffn(x, g_ffn, w_up, conv_w, conv_b, w_down, S):
    tm = TM_FFN
    const = lambda shape: pl.BlockSpec(shape, lambda i: (0,) * len(shape), pipeline_mode=pl.Buffered(1))
    rows = pl.BlockSpec((tm, D_MODEL), lambda i: (i, 0))
    return pl.pallas_call(
        _ffn_kernel,
        grid=(S // tm,),
        in_specs=[rows, const((1, D_MODEL)), const((D_MODEL, 2 * D_FF)),
                  const((CONV_WIDTH, 2 * D_FF)), const((1, 2 * D_FF)), const((D_FF, D_MODEL))],
        out_specs=rows,
        out_shape=jax.ShapeDtypeStruct((S, D_MODEL), F32),
        scratch_shapes=[pltpu.VMEM((8, 2 * D_FF), F32), pltpu.VMEM((tm, D_FF), BF16)],
        compiler_params=_params("arbitrary"),
        name="conv_ffn",
    )(x, g_ffn, w_up, conv_w, conv_b, w_down)


def _group_mean_matrix():
    g = np.arange(MXU_TILE) // HEAD_DIM
    return jnp.asarray((g[:, None] == g[None, :]).astype(np.float32) / HEAD_DIM, dtype=BF16)


def _layer(x, mem, positions, p):
    S = x.shape[0]
    assert S % (max(d for _, d in DIL_PATTERNS) * DIL_BLOCK) == 0 and S % TM_PROJ == 0
    row = lambda v: v.reshape(1, -1).astype(F32)
    c = np.cumsum([0, MOBA_W, MOBA_W, MOBA_W, DIL_W, DIL_W, DIL_W, MEM_W])
    w_in = p["w_in"]
    seg = lambda k: w_in[:, c[k]:c[k + 1]]
    wt = jnp.concatenate([seg(0), seg(2)], axis=1).T.astype(BF16)
    wn = jnp.concatenate([seg(1), seg(3), seg(4), seg(6), seg(5)], axis=1).astype(BF16)
    w_gate = w_in[:, QKV_COLS:].astype(BF16)
    gnat = jnp.concatenate([jnp.tile(p["moba_k_norm_g"], N_MOBA_HEADS), jnp.tile(p["dil_q_norm_g"], N_DIL_HEADS),
                            jnp.tile(p["dil_k_norm_g"], N_DIL_HEADS), jnp.tile(p["mem_q_norm_g"], N_MEM_HEADS),
                            jnp.ones((_NAT_NORM_PAD - _NAT_NORM,), F32)]).reshape(1, _NAT_NORM_PAD)
    gmat = _group_mean_matrix()

    cos_t, sin_t, cos_n, sin_n = _rope_tables(positions, S)
    mk, mv = _mem_kv(mem, row(p["mem_norm_g"]), p["w_mem_kv"].astype(BF16),
                     row(jnp.tile(p["mem_k_norm_g"], N_MEM_HEADS)), gmat)
    (qat, vat, ka, kmean, qm), dil_qkv = _proj(
        x, row(p["mix_norm_g"]), wt, wn, p["moba_q_norm_g"].reshape(HEAD_DIM, 1).astype(F32), gnat, gmat,
        cos_t, sin_t, cos_n, sin_n, S)
    def score_bound(gq, gk):
        return (1.02 * HEAD_DIM * SCALE * jnp.max(jnp.abs(gq)) * jnp.max(jnp.abs(gk))).reshape(1).astype(F32)

    o_a = _moba(score_bound(p["moba_q_norm_g"], p["moba_k_norm_g"]), qat, ka, vat,
                kmean.reshape(S // MOBA_BLOCK, MOBA_W), S)
    bound_d = score_bound(p["dil_q_norm_g"], p["dil_k_norm_g"])
    o_ds, lses = zip(*[_dilated(bound_d, *qkv, w, d, S) for qkv, (w, d) in zip(dil_qkv, DIL_PATTERNS)])
    x1 = _merge(x, row(p["mix_norm_g"]), w_gate, o_a, o_ds, lses, qm, mk, mv,
                p["w_branch_moba"].astype(BF16), p["w_branch_dil"].astype(BF16),
                p["w_branch_mem"].astype(BF16), p["w_out"].astype(BF16), S)
    return _ffn(x1, row(p["ffn_norm_g"]), p["w_ffn_up"].astype(BF16), p["ffn_conv_w"].astype(F32),
                row(p["ffn_conv_b"]), p["w_ffn_down"].astype(BF16), S)


def kernel(x, mem, positions, mix_norm_g, mem_norm_g, w_in, moba_q_norm_g, moba_k_norm_g, dil_q_norm_g, dil_k_norm_g, mem_q_norm_g, mem_k_norm_g, w_mem_kv, w_branch_moba, w_branch_dil, w_branch_mem, w_out, ffn_norm_g, w_ffn_up, ffn_conv_w, ffn_conv_b, w_ffn_down):
    params = dict(mix_norm_g=mix_norm_g, mem_norm_g=mem_norm_g, w_in=w_in, moba_q_norm_g=moba_q_norm_g,
                  moba_k_norm_g=moba_k_norm_g, dil_q_norm_g=dil_q_norm_g, dil_k_norm_g=dil_k_norm_g,
                  mem_q_norm_g=mem_q_norm_g, mem_k_norm_g=mem_k_norm_g, w_mem_kv=w_mem_kv,
                  w_branch_moba=w_branch_moba, w_branch_dil=w_branch_dil, w_branch_mem=w_branch_mem,
                  w_out=w_out, ffn_norm_g=ffn_norm_g, w_ffn_up=w_ffn_up, ffn_conv_w=ffn_conv_w,
                  ffn_conv_b=ffn_conv_b, w_ffn_down=w_ffn_down)
    B = x.shape[0]
    depth = w_in.shape[0]
    outs = []
    for b in range(B):
        xb = x[b]
        for l in range(depth):
            xb = _layer(xb, mem[b], positions[b], {k: v[l] for k, v in params.items()})
        outs.append(xb)
    return jnp.stack(outs, axis=0)
```

```python
import functools

import numpy as np
import jax
import jax.numpy as jnp
from jax import lax
from jax.experimental import pallas as pl
from jax.experimental.pallas import tpu as pltpu

D_MODEL = 1024
HEAD_DIM = 64
HALF = HEAD_DIM // 2
N_MOBA_HEADS = 6
N_DIL_HEADS = 6
N_MEM_HEADS = 4
N_MEM = 256
MOBA_BLOCK = 256
MOBA_TOPK = 3
DIL_PATTERNS = ((128, 1), (512, 4), (2048, 16))
DIL_BLOCK = 128
D_FF = 2816
CONV_WIDTH = 3
ROPE_THETA = 10000.0
EPS = 1e-6
MOBA_W = N_MOBA_HEADS * HEAD_DIM
DIL_W = N_DIL_HEADS * HEAD_DIM
MEM_W = N_MEM_HEADS * HEAD_DIM
QKV_COLS = 3 * MOBA_W + 3 * DIL_W + MEM_W
SCALE = HEAD_DIM ** -0.5

LANES = 128
MXU_TILE = 256
VMEM_LIMIT = 56 * 1024 * 1024

TM_PROJ = 512
TM_MERGE = 512
TM_FFN = 512
FFN_CHUNK = 256
DIL_ROWS = 1024

F32 = jnp.float32
BF16 = jnp.bfloat16
NEG_INF = float("-inf")


def _dot(a, b):
    return jnp.dot(a, b, preferred_element_type=F32)


def _dot_nt(a, b):
    return lax.dot_general(a, b, (((1,), (1,)), ((), ())), preferred_element_type=F32)


def _rms_rows(x, g):
    ms = jnp.mean(x * x, axis=-1, keepdims=True)
    return x * lax.rsqrt(ms + EPS) * g


def _params(*sem):
    return pltpu.CompilerParams(dimension_semantics=sem, vmem_limit_bytes=VMEM_LIMIT)


def _rope_kernel(pos_ref, inv_ref, cos_t_ref, sin_t_ref, cos_n_ref, sin_n_ref):
    pos = pos_ref[...].astype(F32)
    ang = inv_ref[...] * pos
    c = jnp.cos(ang)
    s = jnp.sin(ang)
    cos_t_ref[...] = jnp.concatenate([c, c], axis=0)
    sin_t_ref[...] = jnp.concatenate([-s, s], axis=0)
    cos_n_ref[...] = jnp.concatenate([c, c, c, c], axis=0).T
    sin_n_ref[...] = jnp.concatenate([-s, s, -s, s], axis=0).T


def _rope_tables(positions, S):
    tm = 512
    inv = (ROPE_THETA ** (-jnp.arange(HALF, dtype=F32) / HALF)).reshape(HALF, 1)
    return pl.pallas_call(
        _rope_kernel,
        grid=(S // tm,),
        in_specs=[pl.BlockSpec((1, tm), lambda i: (0, i)),
                  pl.BlockSpec((HALF, 1), lambda i: (0, 0))],
        out_specs=[pl.BlockSpec((HEAD_DIM, tm), lambda i: (0, i)),
                   pl.BlockSpec((HEAD_DIM, tm), lambda i: (0, i)),
                   pl.BlockSpec((tm, LANES), lambda i: (i, 0)),
                   pl.BlockSpec((tm, LANES), lambda i: (i, 0))],
        out_shape=[jax.ShapeDtypeStruct((HEAD_DIM, S), F32),
                   jax.ShapeDtypeStruct((HEAD_DIM, S), F32),
                   jax.ShapeDtypeStruct((S, LANES), F32),
                   jax.ShapeDtypeStruct((S, LANES), F32)],
        compiler_params=_params("parallel"),
        name="rope_tables",
    )(positions.reshape(1, S), inv)


def _memkv_kernel(mem_ref, g_ref, w_ref, gk_ref, gmat_ref, mk_ref, mv_ref):
    h = _rms_rows(mem_ref[...], g_ref[...]).astype(BF16)
    kv = _dot(h, w_ref[...])
    k = kv[:, :MEM_W]
    ms = _dot((k * k).astype(BF16), gmat_ref[...])
    mk_ref[...] = (k * lax.rsqrt(ms + EPS) * gk_ref[...]).astype(BF16)
    mv_ref[...] = kv[:, MEM_W:].astype(BF16)


def _mem_kv(mem, g, w_kv, gk_tiled, gmat):
    full = lambda shape: pl.BlockSpec(shape, lambda i: (0,) * len(shape))
    return pl.pallas_call(
        _memkv_kernel,
        grid=(1,),
        in_specs=[full((N_MEM, D_MODEL)), full((1, D_MODEL)), full((D_MODEL, 2 * MEM_W)),
                  full((1, MEM_W)), full((MXU_TILE, MXU_TILE))],
        out_specs=[full((N_MEM, MEM_W)), full((N_MEM, MEM_W))],
        out_shape=[jax.ShapeDtypeStruct((N_MEM, MEM_W), BF16)] * 2,
        compiler_params=_params("arbitrary"),
        name="mem_kv",
    )(mem, g, w_kv, gk_tiled, gmat)


_NAT_NORM = 3 * 384 + MEM_W
_NAT_NORM_PAD = 1536
_NAT_ROPE = 3 * 384
_NAT_COLS = _NAT_NORM + DIL_W


def _proj_kernel(x_ref, g_ref, wt_ref, wn_ref, gq_ref, gnat_ref, gmat_ref,
                 cos_t_ref, sin_t_ref, cos_n_ref, sin_n_ref,
                 qat_ref, vat_ref, ka_ref, km_ref, qm_ref, *rest):
    dil_refs, stage_ref = rest[:-1], rest[-1]
    tm = x_ref.shape[0]
    h = _rms_rows(x_ref[...], g_ref[...]).astype(BF16)

    yt = _dot_nt(wt_ref[...], h)
    cos_t = cos_t_ref[...]
    sin_t = sin_t_ref[...]
    gq = gq_ref[...]
    for hd in range(N_MOBA_HEADS):
        q = yt[hd * HEAD_DIM:(hd + 1) * HEAD_DIM]
        ms = jnp.mean(q * q, axis=0, keepdims=True)
        q = q * lax.rsqrt(ms + EPS) * gq
        rot = jnp.concatenate([q[HALF:], q[:HALF]], axis=0)
        q = (q * cos_t + rot * sin_t) * SCALE
        for b in range(tm // MOBA_BLOCK):
            qat_ref[b, hd * HEAD_DIM:(hd + 1) * HEAD_DIM, :] = (
                q[:, b * MOBA_BLOCK:(b + 1) * MOBA_BLOCK].astype(BF16))
    for b in range(tm // MOBA_BLOCK):
        vat_ref[b] = yt[MOBA_W:, b * MOBA_BLOCK:(b + 1) * MOBA_BLOCK].astype(BF16)

    gmat = gmat_ref[...]
    lane = lax.broadcasted_iota(jnp.int32, (tm, LANES), 1)
    first_half = (lane % HEAD_DIM) < HALF
    cos_n = cos_n_ref[...]
    sin_n = sin_n_ref[...]
    pairs = DIL_W // LANES

    def emit_dilated(ti, tiles):
        for lt in range(pairs):
            stage_ref[ti * pairs + lt] = tiles[lt]
        for pi, (_, d) in enumerate(DIL_PATTERNS):
            ref = dil_refs[3 * pi + ti]
            for r in range(d):
                for lt in range(pairs):
                    col = r * DIL_W + lt * LANES
                    ref[:, col:col + LANES] = stage_ref[
                        ti * pairs + lt, pl.ds(r, tm // d, stride=d), :].astype(BF16)

    chunks = []
    for c in range(_NAT_NORM_PAD // MXU_TILE):
        if c % 2 == 0:
            wide = _dot(h, wn_ref[:, c * MXU_TILE:(c + 2) * MXU_TILE])
        raw = wide[:, (c % 2) * MXU_TILE:(c % 2 + 1) * MXU_TILE]
        ms = _dot((raw * raw).astype(BF16), gmat)
        blk = raw * lax.rsqrt(ms + EPS) * gnat_ref[:, c * MXU_TILE:(c + 1) * MXU_TILE]
        for hc in range(MXU_TILE // LANES):
            col = c * MXU_TILE + hc * LANES
            if col >= _NAT_NORM:
                chunks.append(raw[:, hc * LANES:(hc + 1) * LANES])
                continue
            v = blk[:, hc * LANES:(hc + 1) * LANES]
            if col < _NAT_ROPE:
                rot = jnp.where(first_half, pltpu.roll(v, LANES - HALF, axis=1),
                                pltpu.roll(v, HALF, axis=1))
                v = v * cos_n + rot * sin_n
            chunks.append(v)
        if len(chunks) >= pairs and c == 1:
            ka = jnp.concatenate(chunks[0:pairs], axis=1)
            for b in range(tm // MOBA_BLOCK):
                km_ref[b] = jnp.mean(ka[b * MOBA_BLOCK:(b + 1) * MOBA_BLOCK], axis=0, keepdims=True)
            ka_ref[...] = ka.astype(BF16)
        if c == 2:
            emit_dilated(0, [t * SCALE for t in chunks[pairs:2 * pairs]])
        if c == 4:
            emit_dilated(1, chunks[2 * pairs:3 * pairs])
    qm_ref[...] = (jnp.concatenate(chunks[3 * pairs:3 * pairs + MEM_W // LANES], axis=1) * SCALE).astype(BF16)
    vd_rest = _dot(h, wn_ref[:, _NAT_NORM_PAD:])
    emit_dilated(2, [chunks[-1]] + [vd_rest[:, t * LANES:(t + 1) * LANES] for t in range(pairs - 1)])


def _proj(x, g_mix, wt, wn, gq_col, gnat, gmat, cos_t, sin_t, cos_n, sin_n, S):
    tm = TM_PROJ
    nb = S // MOBA_BLOCK
    bpt = tm // MOBA_BLOCK
    const = lambda shape: pl.BlockSpec(shape, lambda i: (0,) * len(shape))
    rows = lambda w: pl.BlockSpec((tm, w), lambda i: (i, 0))
    dils = [d for _, d in DIL_PATTERNS for _ in range(3)]
    outs = pl.pallas_call(
        _proj_kernel,
        grid=(S // tm,),
        in_specs=[rows(D_MODEL), const((1, D_MODEL)), const((2 * MOBA_W, D_MODEL)),
                  const((D_MODEL, _NAT_COLS)), const((HEAD_DIM, 1)), const((1, _NAT_NORM_PAD)),
                  const((MXU_TILE, MXU_TILE)),
                  pl.BlockSpec((HEAD_DIM, tm), lambda i: (0, i)),
                  pl.BlockSpec((HEAD_DIM, tm), lambda i: (0, i)),
                  rows(LANES), rows(LANES)],
        out_specs=[pl.BlockSpec((bpt, MOBA_W, MOBA_BLOCK), lambda i: (i, 0, 0)),
                   pl.BlockSpec((bpt, MOBA_W, MOBA_BLOCK), lambda i: (i, 0, 0)),
                   rows(MOBA_W),
                   pl.BlockSpec((bpt, 1, MOBA_W), lambda i: (i, 0, 0)),
                   rows(MEM_W)]
                  + [pl.BlockSpec((tm // d, d * DIL_W), lambda i: (i, 0)) for d in dils],
        out_shape=[jax.ShapeDtypeStruct((nb, MOBA_W, MOBA_BLOCK), BF16),
                   jax.ShapeDtypeStruct((nb, MOBA_W, MOBA_BLOCK), BF16),
                   jax.ShapeDtypeStruct((S, MOBA_W), BF16),
                   jax.ShapeDtypeStruct((nb, 1, MOBA_W), F32),
                   jax.ShapeDtypeStruct((S, MEM_W), BF16)]
                  + [jax.ShapeDtypeStruct((S // d, d * DIL_W), BF16) for d in dils],
        scratch_shapes=[pltpu.VMEM((3 * DIL_W // LANES, tm, LANES), F32)],
        compiler_params=_params("parallel"),
        name="in_proj",
    )(x, g_mix, wt, wn, gq_col, gnat, gmat, cos_t, sin_t, cos_n, sin_n)
    return outs[:5], [outs[5 + 3 * k:8 + 3 * k] for k in range(len(DIL_PATTERNS))]


_SHIFT_HEADROOM = 30.0
_FAST_BOUND = 50.0
_MOBA_UNROLL = 4


def _moba_kernel(bound_ref, qt_ref, k_ref, vt_ref, km_ref, o_ref, bias_ref, ot_ref):
    i = pl.program_id(0)
    tq = MOBA_BLOCK
    unroll = _MOBA_UNROLL
    heads = N_MOBA_HEADS
    nb = km_ref.shape[0]
    row = lax.broadcasted_iota(jnp.int32, (LANES, tq), 0)
    blk = lax.broadcasted_iota(jnp.int32, (nb, tq), 0)

    bound = bound_ref[0]
    fast = bound <= _FAST_BOUND
    sel_bias = jnp.where(fast, _SHIFT_HEADROOM - bound, 0.0)

    def pair_cols(hd):
        return slice((hd // 2) * LANES, (hd // 2 + 1) * LANES)

    qs = []
    for hd in range(heads):
        qt = qt_ref[0, pair_cols(hd), :]
        q_h = jnp.where((row // HEAD_DIM) == hd % 2, qt, jnp.zeros_like(qt))
        qs.append(q_h)
        km = km_ref[:, pair_cols(hd)]
        km1 = km.astype(BF16)
        r1 = km - km1.astype(F32)
        km2 = r1.astype(BF16)
        km3 = (r1 - km2.astype(F32)).astype(BF16)
        g3 = _dot(jnp.concatenate([km1, km2, km3], axis=0), q_h)
        gate = g3[:nb] + g3[nb:2 * nb] + g3[2 * nb:]
        gate = jnp.where(blk < i, gate, NEG_INF)
        bias = jnp.full((nb, tq), NEG_INF, F32)
        for _ in range(MOBA_TOPK):
            m = jnp.max(gate, axis=0, keepdims=True)
            idx = jnp.min(jnp.where(gate == m, blk, nb), axis=0, keepdims=True)
            hit = blk == idx
            bias = jnp.where(hit & (m > NEG_INF), sel_bias, bias)
            gate = jnp.where(hit, NEG_INF, gate)
        bias_ref[hd] = bias

    kpos = lax.broadcasted_iota(jnp.int32, (tq, tq), 0)
    qpos = lax.broadcasted_iota(jnp.int32, (tq, tq), 1)
    causal = kpos <= qpos

    def k_rows(j, hd):
        return k_ref[pl.ds(pl.multiple_of(j * tq, tq), tq), pair_cols(hd)]

    def v_rows(j, hd):
        return vt_ref[j, hd * HEAD_DIM:(hd + 1) * HEAD_DIM, :]

    ones = jnp.ones((16, tq), BF16)

    def values(j, hd, p):
        return _dot(jnp.concatenate([v_rows(j, hd), ones], axis=0), p)

    own_s = [_dot(k_rows(i, hd), qs[hd]) for hd in range(heads)]
    own_p = [jnp.exp(jnp.where(causal, s + sel_bias, NEG_INF)).astype(BF16) for s in own_s]
    own_accs = [values(i, hd, p) for hd, p in enumerate(own_p)]

    @pl.when(fast)
    def _():
        def body(t, accs):
            out = list(accs)
            chains = [(unroll * t + u, hd) for u in range(unroll) for hd in range(heads)]
            ss = [_dot(k_rows(j, hd), qs[hd]) for j, hd in chains]
            ps = [jnp.exp(s + bias_ref[hd, pl.ds(j, 1), :]).astype(BF16) for s, (j, hd) in zip(ss, chains)]
            for p, (j, hd) in zip(ps, chains):
                out[hd] = out[hd] + values(j, hd, p)
            return tuple(out)

        accs = lax.fori_loop(0, (i + unroll - 1) // unroll, body, tuple(own_accs))
        for hd, a in enumerate(accs):
            ot_ref[hd * HEAD_DIM:(hd + 1) * HEAD_DIM, :] = a[:HEAD_DIM] / a[HEAD_DIM:HEAD_DIM + 1]

    @pl.when(jnp.logical_not(fast))
    def _():
        for pr in range(heads // 2):
            pair = (2 * pr, 2 * pr + 1)
            carry = []
            for hd in pair:
                s = jnp.where(causal, _dot(k_rows(i, hd), qs[hd]), NEG_INF)
                m = jnp.max(s, axis=0, keepdims=True)
                p = jnp.exp(s - m)
                l = jnp.sum(p, axis=0, keepdims=True)
                carry += [m, l, _dot(v_rows(i, hd), p.astype(BF16))]

            def body(j, carry):
                out = []
                for c, hd in enumerate(pair):
                    m, l, acc = carry[3 * c:3 * c + 3]
                    s = _dot(k_rows(j, hd), qs[hd]) + bias_ref[hd, pl.ds(j, 1), :]
                    m_new = jnp.maximum(m, jnp.max(s, axis=0, keepdims=True))
                    alpha = jnp.exp(m - m_new)
                    p = jnp.exp(s - m_new)
                    l = alpha * l + jnp.sum(p, axis=0, keepdims=True)
                    acc = alpha * acc + _dot(v_rows(j, hd), p.astype(BF16))
                    out += [m_new, l, acc]
                return tuple(out)

            carry = lax.fori_loop(0, i, body, tuple(carry))
            for c, hd in enumerate(pair):
                ot_ref[hd * HEAD_DIM:(hd + 1) * HEAD_DIM, :] = carry[3 * c + 2] / carry[3 * c + 1]

    o_ref[...] = ot_ref[...].T.astype(BF16)


def _moba(bound, qat, ka, vat, kmean, S):
    nb = S // MOBA_BLOCK
    assert nb % _MOBA_UNROLL == 0
    once = lambda shape: pl.BlockSpec(shape, lambda i: (0,) * len(shape), pipeline_mode=pl.Buffered(1))
    return pl.pallas_call(
        _moba_kernel,
        grid=(nb,),
        in_specs=[pl.BlockSpec(memory_space=pltpu.SMEM),
                  pl.BlockSpec((1, MOBA_W, MOBA_BLOCK), lambda i: (i, 0, 0)),
                  once((S, MOBA_W)), once((nb, MOBA_W, MOBA_BLOCK)), once((nb, MOBA_W))],
        out_specs=pl.BlockSpec((MOBA_BLOCK, MOBA_W), lambda i: (i, 0)),
        out_shape=jax.ShapeDtypeStruct((S, MOBA_W), BF16),
        scratch_shapes=[pltpu.VMEM((N_MOBA_HEADS, nb, MOBA_BLOCK), F32),
                        pltpu.VMEM((MOBA_W, MOBA_BLOCK), F32)],
        compiler_params=_params("arbitrary"),
        name="moba_attn",
    )(bound, qat, ka, vat, kmean)


def _dil_kernel(span, bound_ref, q_ref, kp_ref, kc_ref, vp_ref, vc_ref, o_ref, lse_ref):
    n = pl.program_id(1)
    blk = DIL_BLOCK
    nsub = q_ref.shape[0] // blk
    lane = lax.broadcasted_iota(jnp.int32, (blk, LANES), 1)
    lo_half = lane < HEAD_DIM
    qi = lax.broadcasted_iota(jnp.int32, (2 * blk, 2 * blk), 0) % blk + blk
    kj = lax.broadcasted_iota(jnp.int32, (2 * blk, 2 * blk), 1)
    dist = qi - kj
    band = jnp.where((dist >= 0) & (dist <= span), 0.0, NEG_INF)
    band_first = jnp.where(kj >= blk, band, NEG_INF)
    bound = bound_ref[0]
    fast = bound <= _FAST_BOUND
    shift = bound - _SHIFT_HEADROOM

    def blocks(fixed_shift):
        for pr in range(DIL_W // LANES):
            for b in range(nsub):
                block(fixed_shift, b, slice(pr * LANES, (pr + 1) * LANES))

    def block(fixed_shift, b, cols):
        q = q_ref[b * blk:(b + 1) * blk, cols]
        if b == 0:
            kk = jnp.concatenate([kp_ref[:, cols], kc_ref[0:blk, cols]], axis=0)
            vv = jnp.concatenate([vp_ref[:, cols], vc_ref[0:blk, cols]], axis=0)
            mask = jnp.where(n > 0, band, band_first)
        else:
            kk = kc_ref[(b - 1) * blk:(b + 1) * blk, cols]
            vv = vc_ref[(b - 1) * blk:(b + 1) * blk, cols]
            mask = band
        zero = jnp.zeros_like(q)
        q2 = jnp.concatenate([jnp.where(lo_half, q, zero), jnp.where(lo_half, zero, q)], axis=0)
        s = _dot_nt(q2, kk) + mask
        if fixed_shift:
            p = jnp.exp(s - shift).astype(BF16)
            ov = _dot(p, jnp.concatenate([vv, jnp.ones_like(vv)], axis=1))
            den = ov[:, LANES:]
            o2 = ov[:, :LANES] / den
            lse2 = shift + jnp.log(den)
        else:
            m = jnp.max(s, axis=-1, keepdims=True)
            p = jnp.exp(s - m)
            den = jnp.sum(p, axis=-1, keepdims=True)
            o2 = _dot(p.astype(BF16), vv) / den
            lse2 = jnp.broadcast_to(m + jnp.log(den), (2 * blk, LANES))
        o_ref[b * blk:(b + 1) * blk, cols] = jnp.where(lo_half, o2[:blk], o2[blk:]).astype(BF16)
        lse_ref[b * blk:(b + 1) * blk, cols] = jnp.where(lo_half, lse2[:blk], lse2[blk:])

    pl.when(fast)(lambda: blocks(True))
    pl.when(jnp.logical_not(fast))(lambda: blocks(False))


def _dilated(bound, qd, kd, vd, window, dil, S):
    span = window // dil
    L = S // dil
    rows = min(DIL_ROWS, L)
    sub = rows // DIL_BLOCK
    cur = pl.BlockSpec((rows, DIL_W), lambda r, n: (n, r))
    prev = pl.BlockSpec((DIL_BLOCK, DIL_W), lambda r, n: (jnp.maximum(n * sub - 1, 0), r))
    return pl.pallas_call(
        functools.partial(_dil_kernel, span),
        grid=(dil, L // rows),
        in_specs=[pl.BlockSpec(memory_space=pltpu.SMEM), cur, prev, cur, prev, cur],
        out_specs=[cur, cur],
        out_shape=[jax.ShapeDtypeStruct((L, dil * DIL_W), BF16),
                   jax.ShapeDtypeStruct((L, dil * DIL_W), F32)],
        compiler_params=_params("parallel", "arbitrary"),
        name=f"dilated_attn_d{dil}",
    )(bound, qd, kd, kd, vd, vd)


def _merge_kernel(x_ref, g_ref, wg_ref, oa_ref, od1_ref, od2_ref, od3_ref, l1_ref, l2_ref, l3_ref,
                  qm_ref, mk_ref, mv_ref, wba_ref, wbd_ref, wbm_ref, wo_ref, out_ref, nat_ref):
    x = x_ref[...]
    tm = x.shape[0]
    h = _rms_rows(x, g_ref[...]).astype(BF16)

    def natural(ref, dil, slot):
        if dil == 1:
            return ref[...].astype(F32)
        tiles = DIL_W // LANES
        for r in range(dil):
            for lt in range(tiles):
                col = r * DIL_W + lt * LANES
                nat_ref[slot * tiles + lt, pl.ds(r, tm // dil, stride=dil), :] = (
                    ref[:, col:col + LANES].astype(F32))
        return jnp.concatenate([nat_ref[slot * tiles + lt] for lt in range(tiles)], axis=1)

    lane = lax.broadcasted_iota(jnp.int32, (tm, LANES), 1)
    lo_half = lane < HEAD_DIM
    mem_pairs = MEM_W // LANES
    scores = []
    for pr in range(mem_pairs):
        q = qm_ref[:, pr * LANES:(pr + 1) * LANES]
        mk = mk_ref[:, pr * LANES:(pr + 1) * LANES]
        for hh in range(2):
            q_h = jnp.where(lo_half if hh == 0 else ~lo_half, q, jnp.zeros_like(q))
            scores.append(_dot_nt(q_h, mk))
    graw = [_dot(h, wg_ref[:, bi * D_MODEL:(bi + 1) * D_MODEL]) for bi in range(3)]

    dils = [d for _, d in DIL_PATTERNS]
    l1, l2, l3 = [natural(ref, d, k) for k, (ref, d) in enumerate(zip((l1_ref, l2_ref, l3_ref), dils))]
    o1, o2, o3 = [natural(ref, d, 3 + k) for k, (ref, d) in enumerate(zip((od1_ref, od2_ref, od3_ref), dils))]
    lmax = jnp.maximum(jnp.maximum(l1, l2), l3)
    e1, e2, e3 = jnp.exp(l1 - lmax), jnp.exp(l2 - lmax), jnp.exp(l3 - lmax)
    o_d = (e1 * o1 + e2 * o2 + e3 * o3) / (e1 + e2 + e3)

    probs = []
    for s in scores:
        p = jnp.exp(s - jnp.max(s, axis=-1, keepdims=True))
        probs.append((p / jnp.sum(p, axis=-1, keepdims=True)).astype(BF16))
    o_m = []
    for pr in range(mem_pairs):
        mv = mv_ref[:, pr * LANES:(pr + 1) * LANES]
        outs = [_dot(probs[2 * pr + hh], mv) for hh in range(2)]
        o_m.append(jnp.where(lo_half, outs[0], outs[1]))
    o_m = jnp.concatenate(o_m, axis=1)

    merged = None
    branches = ((oa_ref[...], wba_ref), (o_d.astype(BF16), wbd_ref), (o_m.astype(BF16), wbm_ref))
    for bi, (o_b, w_ref) in enumerate(branches):
        term = jax.nn.sigmoid(graw[bi]) * _dot(o_b, w_ref[...])
        merged = term if merged is None else merged + term
    out_ref[...] = x + _dot(merged.astype(BF16), wo_ref[...])


def _merge(x, g_mix, w_gate, o_a, o_ds, lses, qm, mk, mv, w_ba, w_bd, w_bm, w_out, S):
    tm = TM_MERGE
    const = lambda shape: pl.BlockSpec(shape, lambda i: (0,) * len(shape), pipeline_mode=pl.Buffered(1))
    rows = lambda w: pl.BlockSpec((tm, w), lambda i: (i, 0))
    dil_rows = [pl.BlockSpec((tm // d, d * DIL_W), lambda i: (i, 0)) for _, d in DIL_PATTERNS]
    return pl.pallas_call(
        _merge_kernel,
        grid=(S // tm,),
        in_specs=[rows(D_MODEL), const((1, D_MODEL)), const((D_MODEL, 3 * D_MODEL)),
                  rows(MOBA_W), *dil_rows, *dil_rows,
                  rows(MEM_W), const((N_MEM, MEM_W)), const((N_MEM, MEM_W)),
                  const((MOBA_W, D_MODEL)), const((DIL_W, D_MODEL)), const((MEM_W, D_MODEL)),
                  const((D_MODEL, D_MODEL))],
        out_specs=rows(D_MODEL),
        out_shape=jax.ShapeDtypeStruct((S, D_MODEL), F32),
        scratch_shapes=[pltpu.VMEM((2 * len(DIL_PATTERNS) * DIL_W // LANES, tm, LANES), F32)],
        compiler_params=_params("parallel"),
        name="gated_merge",
    )(x, g_mix, w_gate, o_a, *o_ds, *lses, qm, mk, mv, w_ba, w_bd, w_bm, w_out)


def _ffn_kernel(x_ref, g_ref, wup_ref, cw_ref, cb_ref, wdn_ref, out_ref, halo_ref, act_ref):
    i = pl.program_id(0)
    tm = x_ref.shape[0]

    @pl.when(i == 0)
    def _():
        halo_ref[...] = jnp.zeros_like(halo_ref)

    x = x_ref[...]
    h = _rms_rows(x, g_ref[...]).astype(BF16)
    row8 = lax.broadcasted_iota(jnp.int32, (8, FFN_CHUNK), 0)

    def conv(u, col):
        prev = halo_ref[:, col:col + FFN_CHUNK]
        halo_ref[:, col:col + FFN_CHUNK] = u[tm - 8:, :]
        p6 = jnp.broadcast_to(prev[6:7], (8, FFN_CHUNK))
        p7 = jnp.broadcast_to(prev[7:8], (8, FFN_CHUNK))
        r1 = pltpu.roll(u, 1, axis=0)
        r2 = pltpu.roll(u, 2, axis=0)
        top1 = jnp.where(row8 == 0, p7, r1[:8])
        top2 = jnp.where(row8 == 0, p6, jnp.where(row8 == 1, p7, r2[:8]))
        u1 = jnp.concatenate([top1, r1[8:]], axis=0)
        u2 = jnp.concatenate([top2, r2[8:]], axis=0)
        w = cw_ref[:, col:col + FFN_CHUNK]
        return cb_ref[:, col:col + FFN_CHUNK] + w[0:1] * u2 + w[1:2] * u1 + w[2:3] * u

    for c in range(D_FF // FFN_CHUNK):
        cg = c * FFN_CHUNK
        cv = D_FF + c * FFN_CHUNK
        u_g = conv(_dot(h, wup_ref[:, cg:cg + FFN_CHUNK]), cg)
        u_v = conv(_dot(h, wup_ref[:, cv:cv + FFN_CHUNK]), cv)
        act_ref[:, cg:cg + FFN_CHUNK] = (jax.nn.silu(u_g) * u_v).astype(BF16)
    out_ref[...] = x + _dot(act_ref[...], wdn_ref[...])


def _ffn(x, g_ffn, w_up, conv_w, conv_b, w_down, S):
    tm = TM_FFN
    const = lambda shape: pl.BlockSpec(shape, lambda i: (0,) * len(shape), pipeline_mode=pl.Buffered(1))
    rows = pl.BlockSpec((tm, D_MODEL), lambda i: (i, 0))
    return pl.pallas_call(
        _ffn_kernel,
        grid=(S // tm,),
        in_specs=[rows, const((1, D_MODEL)), const((D_MODEL, 2 * D_FF)),
                  const((CONV_WIDTH, 2 * D_FF)), const((1, 2 * D_FF)), const((D_FF, D_MODEL))],
        out_specs=rows,
        out_shape=jax.ShapeDtypeStruct((S, D_MODEL), F32),
        scratch_shapes=[pltpu.VMEM((8, 2 * D_FF), F32), pltpu.VMEM((tm, D_FF), BF16)],
        compiler_params=_params("arbitrary"),
        name="conv_ffn",
    )(x, g_ffn, w_up, conv_w, conv_b, w_down)


def _group_mean_matrix():
    g = np.arange(MXU_TILE) // HEAD_DIM
    return jnp.asarray((g[:, None] == g[None, :]).astype(np.float32) / HEAD_DIM, dtype=BF16)


def _layer(x, mem, positions, p):
    S = x.shape[0]
    assert S % (max(d for _, d in DIL_PATTERNS) * DIL_BLOCK) == 0 and S % TM_PROJ == 0
    row = lambda v: v.reshape(1, -1).astype(F32)
    c = np.cumsum([0, MOBA_W, MOBA_W, MOBA_W, DIL_W, DIL_W, DIL_W, MEM_W])
    w_in = p["w_in"]
    seg = lambda k: w_in[:, c[k]:c[k + 1]]
    wt = jnp.concatenate([seg(0), seg(2)], axis=1).T.astype(BF16)
    wn = jnp.concatenate([seg(1), seg(3), seg(4), seg(6), seg(5)], axis=1).astype(BF16)
    w_gate = w_in[:, QKV_COLS:].astype(BF16)
    gnat = jnp.concatenate([jnp.tile(p["moba_k_norm_g"], N_MOBA_HEADS), jnp.tile(p["dil_q_norm_g"], N_DIL_HEADS),
                            jnp.tile(p["dil_k_norm_g"], N_DIL_HEADS), jnp.tile(p["mem_q_norm_g"], N_MEM_HEADS),
                            jnp.ones((_NAT_NORM_PAD - _NAT_NORM,), F32)]).reshape(1, _NAT_NORM_PAD)
    gmat = _group_mean_matrix()

    cos_t, sin_t, cos_n, sin_n = _rope_tables(positions, S)
    mk, mv = _mem_kv(mem, row(p["mem_norm_g"]), p["w_mem_kv"].astype(BF16),
                     row(jnp.tile(p["mem_k_norm_g"], N_MEM_HEADS)), gmat)
    (qat, vat, ka, kmean, qm), dil_qkv = _proj(
        x, row(p["mix_norm_g"]), wt, wn, p["moba_q_norm_g"].reshape(HEAD_DIM, 1).astype(F32), gnat, gmat,
        cos_t, sin_t, cos_n, sin_n, S)
    def score_bound(gq, gk):
        return (1.02 * HEAD_DIM * SCALE * jnp.max(jnp.abs(gq)) * jnp.max(jnp.abs(gk))).reshape(1).astype(F32)

    o_a = _moba(score_bound(p["moba_q_norm_g"], p["moba_k_norm_g"]), qat, ka, vat,
                kmean.reshape(S // MOBA_BLOCK, MOBA_W), S)
    bound_d = score_bound(p["dil_q_norm_g"], p["dil_k_norm_g"])
    o_ds, lses = zip(*[_dilated(bound_d, *qkv, w, d, S) for qkv, (w, d) in zip(dil_qkv, DIL_PATTERNS)])
    x1 = _merge(x, row(p["mix_norm_g"]), w_gate, o_a, o_ds, lses, qm, mk, mv,
                p["w_branch_moba"].astype(BF16), p["w_branch_dil"].astype(BF16),
                p["w_branch_mem"].astype(BF16), p["w_out"].astype(BF16), S)
    return _ffn(x1, row(p["ffn_norm_g"]), p["w_ffn_up"].astype(BF16), p["ffn_conv_w"].astype(F32),
                row(p["ffn_conv_b"]), p["w_ffn_down"].astype(BF16), S)


def kernel(x, mem, positions, mix_norm_g, mem_norm_g, w_in, moba_q_norm_g, moba_k_norm_g, dil_q_norm_g, dil_k_norm_g, mem_q_norm_g, mem_k_norm_g, w_mem_kv, w_branch_moba, w_branch_dil, w_branch_mem, w_out, ffn_norm_g, w_ffn_up, ffn_conv_w, ffn_conv_b, w_ffn_down):
    params = dict(mix_norm_g=mix_norm_g, mem_norm_g=mem_norm_g, w_in=w_in, moba_q_norm_g=moba_q_norm_g,
                  moba_k_norm_g=moba_k_norm_g, dil_q_norm_g=dil_q_norm_g, dil_k_norm_g=dil_k_norm_g,
                  mem_q_norm_g=mem_q_norm_g, mem_k_norm_g=mem_k_norm_g, w_mem_kv=w_mem_kv,
                  w_branch_moba=w_branch_moba, w_branch_dil=w_branch_dil, w_branch_mem=w_branch_mem,
                  w_out=w_out, ffn_norm_g=ffn_norm_g, w_ffn_up=w_ffn_up, ffn_conv_w=ffn_conv_w,
                  ffn_conv_b=ffn_conv_b, w_ffn_down=w_ffn_down)
    B = x.shape[0]
    depth = w_in.shape[0]
    outs = []
    for b in range(B):
        xb = x[b]
        for l in range(depth):
            xb = _layer(xb, mem[b], positions[b], {k: v[l] for k, v in params.items()})
        outs.append(xb)
    return jnp.stack(outs, axis=0)
```

```python
import functools

import numpy as np
import jax
import jax.numpy as jnp
from jax import lax
from jax.experimental import pallas as pl
from jax.experimental.pallas import tpu as pltpu

D_MODEL = 1024
HEAD_DIM = 64
HALF = HEAD_DIM // 2
N_MOBA_HEADS = 6
N_DIL_HEADS = 6
N_MEM_HEADS = 4
N_MEM = 256
MOBA_BLOCK = 256
MOBA_TOPK = 3
DIL_PATTERNS = ((128, 1), (512, 4), (2048, 16))
DIL_BLOCK = 128
D_FF = 2816
CONV_WIDTH = 3
ROPE_THETA = 10000.0
EPS = 1e-6
MOBA_W = N_MOBA_HEADS * HEAD_DIM
DIL_W = N_DIL_HEADS * HEAD_DIM
MEM_W = N_MEM_HEADS * HEAD_DIM
QKV_COLS = 3 * MOBA_W + 3 * DIL_W + MEM_W
SCALE = HEAD_DIM ** -0.5

LANES = 128
MXU_TILE = 256
VMEM_LIMIT = 56 * 1024 * 1024

TM_PROJ = 512
TM_MERGE = 512
TM_FFN = 512
FFN_CHUNK = 256
DIL_ROWS = 1024

F32 = jnp.float32
BF16 = jnp.bfloat16
NEG_INF = float("-inf")


def _dot(a, b):
    return jnp.dot(a, b, preferred_element_type=F32)


def _dot_nt(a, b):
    return lax.dot_general(a, b, (((1,), (1,)), ((), ())), preferred_element_type=F32)


def _rms_rows(x, g):
    ms = jnp.mean(x * x, axis=-1, keepdims=True)
    return x * lax.rsqrt(ms + EPS) * g


def _params(*sem):
    return pltpu.CompilerParams(dimension_semantics=sem, vmem_limit_bytes=VMEM_LIMIT)


def _rope_kernel(pos_ref, inv_ref, cos_t_ref, sin_t_ref, cos_n_ref, sin_n_ref):
    pos = pos_ref[...].astype(F32)
    ang = inv_ref[...] * pos
    c = jnp.cos(ang)
    s = jnp.sin(ang)
    cos_t_ref[...] = jnp.concatenate([c, c], axis=0)
    sin_t_ref[...] = jnp.concatenate([-s, s], axis=0)
    cos_n_ref[...] = jnp.concatenate([c, c, c, c], axis=0).T
    sin_n_ref[...] = jnp.concatenate([-s, s, -s, s], axis=0).T


def _rope_tables(positions, S):
    tm = 512
    inv = (ROPE_THETA ** (-jnp.arange(HALF, dtype=F32) / HALF)).reshape(HALF, 1)
    return pl.pallas_call(
        _rope_kernel,
        grid=(S // tm,),
        in_specs=[pl.BlockSpec((1, tm), lambda i: (0, i)),
                  pl.BlockSpec((HALF, 1), lambda i: (0, 0))],
        out_specs=[pl.BlockSpec((HEAD_DIM, tm), lambda i: (0, i)),
                   pl.BlockSpec((HEAD_DIM, tm), lambda i: (0, i)),
                   pl.BlockSpec((tm, LANES), lambda i: (i, 0)),
                   pl.BlockSpec((tm, LANES), lambda i: (i, 0))],
        out_shape=[jax.ShapeDtypeStruct((HEAD_DIM, S), F32),
                   jax.ShapeDtypeStruct((HEAD_DIM, S), F32),
                   jax.ShapeDtypeStruct((S, LANES), F32),
                   jax.ShapeDtypeStruct((S, LANES), F32)],
        compiler_params=_params("parallel"),
        name="rope_tables",
    )(positions.reshape(1, S), inv)


def _memkv_kernel(mem_ref, g_ref, w_ref, gk_ref, gmat_ref, mk_ref, mv_ref):
    h = _rms_rows(mem_ref[...], g_ref[...]).astype(BF16)
    kv = _dot(h, w_ref[...])
    k = kv[:, :MEM_W]
    ms = _dot((k * k).astype(BF16), gmat_ref[...])
    mk_ref[...] = (k * lax.rsqrt(ms + EPS) * gk_ref[...]).astype(BF16)
    mv_ref[...] = kv[:, MEM_W:].astype(BF16)


def _mem_kv(mem, g, w_kv, gk_tiled, gmat):
    full = lambda shape: pl.BlockSpec(shape, lambda i: (0,) * len(shape))
    return pl.pallas_call(
        _memkv_kernel,
        grid=(1,),
        in_specs=[full((N_MEM, D_MODEL)), full((1, D_MODEL)), full((D_MODEL, 2 * MEM_W)),
                  full((1, MEM_W)), full((MXU_TILE, MXU_TILE))],
        out_specs=[full((N_MEM, MEM_W)), full((N_MEM, MEM_W))],
        out_shape=[jax.ShapeDtypeStruct((N_MEM, MEM_W), BF16)] * 2,
        compiler_params=_params("arbitrary"),
        name="mem_kv",
    )(mem, g, w_kv, gk_tiled, gmat)


_NAT_NORM = 3 * 384 + MEM_W
_NAT_NORM_PAD = 1536
_NAT_ROPE = 3 * 384
_NAT_COLS = _NAT_NORM + DIL_W


def _proj_kernel(x_ref, g_ref, wt_ref, wn_ref, gq_ref, gnat_ref, gmat_ref,
                 cos_t_ref, sin_t_ref, cos_n_ref, sin_n_ref,
                 qat_ref, vat_ref, ka_ref, km_ref, qm_ref, *rest):
    dil_refs, stage_ref = rest[:-1], rest[-1]
    tm = x_ref.shape[0]
    h = _rms_rows(x_ref[...], g_ref[...]).astype(BF16)

    yt = _dot_nt(wt_ref[...], h)
    cos_t = cos_t_ref[...]
    sin_t = sin_t_ref[...]
    gq = gq_ref[...]
    for hd in range(N_MOBA_HEADS):
        q = yt[hd * HEAD_DIM:(hd + 1) * HEAD_DIM]
        ms = jnp.mean(q * q, axis=0, keepdims=True)
        q = q * lax.rsqrt(ms + EPS) * gq
        rot = jnp.concatenate([q[HALF:], q[:HALF]], axis=0)
        q = (q * cos_t + rot * sin_t) * SCALE
        for b in range(tm // MOBA_BLOCK):
            qat_ref[b, hd * HEAD_DIM:(hd + 1) * HEAD_DIM, :] = (
                q[:, b * MOBA_BLOCK:(b + 1) * MOBA_BLOCK].astype(BF16))
    for b in range(tm // MOBA_BLOCK):
        vat_ref[b] = yt[MOBA_W:, b * MOBA_BLOCK:(b + 1) * MOBA_BLOCK].astype(BF16)

    gmat = gmat_ref[...]
    lane = lax.broadcasted_iota(jnp.int32, (tm, LANES), 1)
    first_half = (lane % HEAD_DIM) < HALF
    cos_n = cos_n_ref[...]
    sin_n = sin_n_ref[...]
    pairs = DIL_W // LANES

    def emit_dilated(ti, tiles):
        for lt in range(pairs):
            stage_ref[ti * pairs + lt] = tiles[lt]
        for pi, (_, d) in enumerate(DIL_PATTERNS):
            ref = dil_refs[3 * pi + ti]
            for r in range(d):
                for lt in range(pairs):
                    col = r * DIL_W + lt * LANES
                    ref[:, col:col + LANES] = stage_ref[
                        ti * pairs + lt, pl.ds(r, tm // d, stride=d), :].astype(BF16)

    chunks = []
    for c in range(_NAT_NORM_PAD // MXU_TILE):
        if c % 2 == 0:
            wide = _dot(h, wn_ref[:, c * MXU_TILE:(c + 2) * MXU_TILE])
        raw = wide[:, (c % 2) * MXU_TILE:(c % 2 + 1) * MXU_TILE]
        ms = _dot((raw * raw).astype(BF16), gmat)
        blk = raw * lax.rsqrt(ms + EPS) * gnat_ref[:, c * MXU_TILE:(c + 1) * MXU_TILE]
        for hc in range(MXU_TILE // LANES):
            col = c * MXU_TILE + hc * LANES
            if col >= _NAT_NORM:
                chunks.append(raw[:, hc * LANES:(hc + 1) * LANES])
                continue
            v = blk[:, hc * LANES:(hc + 1) * LANES]
            if col < _NAT_ROPE:
                rot = jnp.where(first_half, pltpu.roll(v, LANES - HALF, axis=1),
                                pltpu.roll(v, HALF, axis=1))
                v = v * cos_n + rot * sin_n
            chunks.append(v)
        if len(chunks) >= pairs and c == 1:
            ka = jnp.concatenate(chunks[0:pairs], axis=1)
            for b in range(tm // MOBA_BLOCK):
                km_ref[b] = jnp.mean(ka[b * MOBA_BLOCK:(b + 1) * MOBA_BLOCK], axis=0, keepdims=True)
            ka_ref[...] = ka.astype(BF16)
        if c == 2:
            emit_dilated(0, [t * SCALE for t in chunks[pairs:2 * pairs]])
        if c == 4:
            emit_dilated(1, chunks[2 * pairs:3 * pairs])
    qm_ref[...] = (jnp.concatenate(chunks[3 * pairs:3 * pairs + MEM_W // LANES], axis=1) * SCALE).astype(BF16)
    vd_rest = _dot(h, wn_ref[:, _NAT_NORM_PAD:])
    emit_dilated(2, [chunks[-1]] + [vd_rest[:, t * LANES:(t + 1) * LANES] for t in range(pairs - 1)])


def _proj(x, g_mix, wt, wn, gq_col, gnat, gmat, cos_t, sin_t, cos_n, sin_n, S):
    tm = TM_PROJ
    nb = S // MOBA_BLOCK
    bpt = tm // MOBA_BLOCK
    const = lambda shape: pl.BlockSpec(shape, lambda i: (0,) * len(shape))
    rows = lambda w: pl.BlockSpec((tm, w), lambda i: (i, 0))
    dils = [d for _, d in DIL_PATTERNS for _ in range(3)]
    outs = pl.pallas_call(
        _proj_kernel,
        grid=(S // tm,),
        in_specs=[rows(D_MODEL), const((1, D_MODEL)), const((2 * MOBA_W, D_MODEL)),
                  const((D_MODEL, _NAT_COLS)), const((HEAD_DIM, 1)), const((1, _NAT_NORM_PAD)),
                  const((MXU_TILE, MXU_TILE)),
                  pl.BlockSpec((HEAD_DIM, tm), lambda i: (0, i)),
                  pl.BlockSpec((HEAD_DIM, tm), lambda i: (0, i)),
                  rows(LANES), rows(LANES)],
        out_specs=[pl.BlockSpec((bpt, MOBA_W, MOBA_BLOCK), lambda i: (i, 0, 0)),
                   pl.BlockSpec((bpt, MOBA_W, MOBA_BLOCK), lambda i: (i, 0, 0)),
                   rows(MOBA_W),
                   pl.BlockSpec((bpt, 1, MOBA_W), lambda i: (i, 0, 0)),
                   rows(MEM_W)]
                  + [pl.BlockSpec((tm // d, d * DIL_W), lambda i: (i, 0)) for d in dils],
        out_shape=[jax.ShapeDtypeStruct((nb, MOBA_W, MOBA_BLOCK), BF16),
                   jax.ShapeDtypeStruct((nb, MOBA_W, MOBA_BLOCK), BF16),
                   jax.ShapeDtypeStruct((S, MOBA_W), BF16),
                   jax.ShapeDtypeStruct((nb, 1, MOBA_W), F32),
                   jax.ShapeDtypeStruct((S, MEM_W), BF16)]
                  + [jax.ShapeDtypeStruct((S // d, d * DIL_W), BF16) for d in dils],
        scratch_shapes=[pltpu.VMEM((3 * DIL_W // LANES, tm, LANES), F32)],
        compiler_params=_params("parallel"),
        name="in_proj",
    )(x, g_mix, wt, wn, gq_col, gnat, gmat, cos_t, sin_t, cos_n, sin_n)
    return outs[:5], [outs[5 + 3 * k:8 + 3 * k] for k in range(len(DIL_PATTERNS))]


_SHIFT_HEADROOM = 30.0
_FAST_BOUND = 50.0
_MOBA_UNROLL = 2


def _moba_kernel(bound_ref, qt_ref, k_ref, vt_ref, km_ref, o_ref, bias_ref, ot_ref):
    i = pl.program_id(0)
    tq = MOBA_BLOCK
    unroll = _MOBA_UNROLL
    heads = N_MOBA_HEADS
    nb = km_ref.shape[0]
    row = lax.broadcasted_iota(jnp.int32, (LANES, tq), 0)
    blk = lax.broadcasted_iota(jnp.int32, (nb, tq), 0)

    bound = bound_ref[0]
    fast = bound <= _FAST_BOUND
    sel_bias = jnp.where(fast, _SHIFT_HEADROOM - bound, 0.0)

    def pair_cols(hd):
        return slice((hd // 2) * LANES, (hd // 2 + 1) * LANES)

    qs = []
    for hd in range(heads):
        qt = qt_ref[0, pair_cols(hd), :]
        q_h = jnp.where((row // HEAD_DIM) == hd % 2, qt, jnp.zeros_like(qt))
        qs.append(q_h)
        km = km_ref[:, pair_cols(hd)]
        km1 = km.astype(BF16)
        r1 = km - km1.astype(F32)
        km2 = r1.astype(BF16)
        km3 = (r1 - km2.astype(F32)).astype(BF16)
        g3 = _dot(jnp.concatenate([km1, km2, km3], axis=0), q_h)
        gate = g3[:nb] + g3[nb:2 * nb] + g3[2 * nb:]
        gate = jnp.where(blk < i, gate, NEG_INF)
        bias = jnp.full((nb, tq), NEG_INF, F32)
        for _ in range(MOBA_TOPK):
            m = jnp.max(gate, axis=0, keepdims=True)
            idx = jnp.min(jnp.where(gate == m, blk, nb), axis=0, keepdims=True)
            hit = blk == idx
            bias = jnp.where(hit & (m > NEG_INF), sel_bias, bias)
            gate = jnp.where(hit, NEG_INF, gate)
        bias_ref[hd] = bias

    kpos = lax.broadcasted_iota(jnp.int32, (tq, tq), 0)
    qpos = lax.broadcasted_iota(jnp.int32, (tq, tq), 1)
    causal = kpos <= qpos

    def k_rows(j, hd):
        return k_ref[pl.ds(pl.multiple_of(j * tq, tq), tq), pair_cols(hd)]

    def v_rows(j, hd):
        return vt_ref[j, hd * HEAD_DIM:(hd + 1) * HEAD_DIM, :]

    ones = jnp.ones((16, tq), BF16)

    def values(j, hd, p):
        return _dot(jnp.concatenate([v_rows(j, hd), ones], axis=0), p)

    own_s = [_dot(k_rows(i, hd), qs[hd]) for hd in range(heads)]
    own_p = [jnp.exp(jnp.where(causal, s + sel_bias, NEG_INF)).astype(BF16) for s in own_s]
    own_accs = [values(i, hd, p) for hd, p in enumerate(own_p)]

    @pl.when(fast)
    def _():
        def body(t, accs):
            out = list(accs)
            chains = [(unroll * t + u, hd) for u in range(unroll) for hd in range(heads)]
            ss = [_dot(k_rows(j, hd), qs[hd]) for j, hd in chains]
            ps = [jnp.exp(s + bias_ref[hd, pl.ds(j, 1), :]).astype(BF16) for s, (j, hd) in zip(ss, chains)]
            for p, (j, hd) in zip(ps, chains):
                out[hd] = out[hd] + values(j, hd, p)
            return tuple(out)

        accs = lax.fori_loop(0, (i + unroll - 1) // unroll, body, tuple(own_accs))
        for hd, a in enumerate(accs):
            ot_ref[hd * HEAD_DIM:(hd + 1) * HEAD_DIM, :] = a[:HEAD_DIM] / a[HEAD_DIM:HEAD_DIM + 1]

    @pl.when(jnp.logical_not(fast))
    def _():
        for pr in range(heads // 2):
            pair = (2 * pr, 2 * pr + 1)
            carry = []
            for hd in pair:
                s = jnp.where(causal, _dot(k_rows(i, hd), qs[hd]), NEG_INF)
                m = jnp.max(s, axis=0, keepdims=True)
                p = jnp.exp(s - m)
                l = jnp.sum(p, axis=0, keepdims=True)
                carry += [m, l, _dot(v_rows(i, hd), p.astype(BF16))]

            def body(j, carry):
                out = []
                for c, hd in enumerate(pair):
                    m, l, acc = carry[3 * c:3 * c + 3]
                    s = _dot(k_rows(j, hd), qs[hd]) + bias_ref[hd, pl.ds(j, 1), :]
                    m_new = jnp.maximum(m, jnp.max(s, axis=0, keepdims=True))
                    alpha = jnp.exp(m - m_new)
                    p = jnp.exp(s - m_new)
                    l = alpha * l + jnp.sum(p, axis=0, keepdims=True)
                    acc = alpha * acc + _dot(v_rows(j, hd), p.astype(BF16))
                    out += [m_new, l, acc]
                return tuple(out)

            carry = lax.fori_loop(0, i, body, tuple(carry))
            for c, hd in enumerate(pair):
                ot_ref[hd * HEAD_DIM:(hd + 1) * HEAD_DIM, :] = carry[3 * c + 2] / carry[3 * c + 1]

    o_ref[...] = ot_ref[...].T.astype(BF16)


def _moba(bound, qat, ka, vat, kmean, S):
    nb = S // MOBA_BLOCK
    assert nb % _MOBA_UNROLL == 0
    once = lambda shape: pl.BlockSpec(shape, lambda i: (0,) * len(shape), pipeline_mode=pl.Buffered(1))
    return pl.pallas_call(
        _moba_kernel,
        grid=(nb,),
        in_specs=[pl.BlockSpec(memory_space=pltpu.SMEM),
                  pl.BlockSpec((1, MOBA_W, MOBA_BLOCK), lambda i: (i, 0, 0)),
                  once((S, MOBA_W)), once((nb, MOBA_W, MOBA_BLOCK)), once((nb, MOBA_W))],
        out_specs=pl.BlockSpec((MOBA_BLOCK, MOBA_W), lambda i: (i, 0)),
        out_shape=jax.ShapeDtypeStruct((S, MOBA_W), BF16),
        scratch_shapes=[pltpu.VMEM((N_MOBA_HEADS, nb, MOBA_BLOCK), F32),
                        pltpu.VMEM((MOBA_W, MOBA_BLOCK), F32)],
        compiler_params=_params("arbitrary"),
        name="moba_attn",
    )(bound, qat, ka, vat, kmean)


def _dil_kernel(span, bound_ref, q_ref, kp_ref, kc_ref, vp_ref, vc_ref, o_ref, lse_ref):
    n = pl.program_id(1)
    blk = DIL_BLOCK
    nsub = q_ref.shape[0] // blk
    lane = lax.broadcasted_iota(jnp.int32, (blk, LANES), 1)
    lo_half = lane < HEAD_DIM
    qi = lax.broadcasted_iota(jnp.int32, (2 * blk, 2 * blk), 0) % blk + blk
    kj = lax.broadcasted_iota(jnp.int32, (2 * blk, 2 * blk), 1)
    dist = qi - kj
    band = jnp.where((dist >= 0) & (dist <= span), 0.0, NEG_INF)
    band_first = jnp.where(kj >= blk, band, NEG_INF)
    bound = bound_ref[0]
    fast = bound <= _FAST_BOUND
    shift = bound - _SHIFT_HEADROOM

    def blocks(fixed_shift):
        for pr in range(DIL_W // LANES):
            for b in range(nsub):
                block(fixed_shift, b, slice(pr * LANES, (pr + 1) * LANES))

    def block(fixed_shift, b, cols):
        q = q_ref[b * blk:(b + 1) * blk, cols]
        if b == 0:
            kk = jnp.concatenate([kp_ref[:, cols], kc_ref[0:blk, cols]], axis=0)
            vv = jnp.concatenate([vp_ref[:, cols], vc_ref[0:blk, cols]], axis=0)
            mask = jnp.where(n > 0, band, band_first)
        else:
            kk = kc_ref[(b - 1) * blk:(b + 1) * blk, cols]
            vv = vc_ref[(b - 1) * blk:(b + 1) * blk, cols]
            mask = band
        zero = jnp.zeros_like(q)
        q2 = jnp.concatenate([jnp.where(lo_half, q, zero), jnp.where(lo_half, zero, q)], axis=0)
        s = _dot_nt(q2, kk) + mask
        if fixed_shift:
            p = jnp.exp(s - shift).astype(BF16)
            ov = _dot(p, jnp.concatenate([vv, jnp.ones_like(vv)], axis=1))
            den = ov[:, LANES:]
            o2 = ov[:, :LANES] / den
            lse2 = shift + jnp.log(den)
        else:
            m = jnp.max(s, axis=-1, keepdims=True)
            p = jnp.exp(s - m)
            den = jnp.sum(p, axis=-1, keepdims=True)
            o2 = _dot(p.astype(BF16), vv) / den
            lse2 = jnp.broadcast_to(m + jnp.log(den), (2 * blk, LANES))
        o_ref[b * blk:(b + 1) * blk, cols] = jnp.where(lo_half, o2[:blk], o2[blk:]).astype(BF16)
        lse_ref[b * blk:(b + 1) * blk, cols] = jnp.where(lo_half, lse2[:blk], lse2[blk:])

    pl.when(fast)(lambda: blocks(True))
    pl.when(jnp.logical_not(fast))(lambda: blocks(False))


def _dilated(bound, qd, kd, vd, window, dil, S):
    span = window // dil
    L = S // dil
    rows = min(DIL_ROWS, L)
    sub = rows // DIL_BLOCK
    cur = pl.BlockSpec((rows, DIL_W), lambda r, n: (n, r))
    prev = pl.BlockSpec((DIL_BLOCK, DIL_W), lambda r, n: (jnp.maximum(n * sub - 1, 0), r))
    return pl.pallas_call(
        functools.partial(_dil_kernel, span),
        grid=(dil, L // rows),
        in_specs=[pl.BlockSpec(memory_space=pltpu.SMEM), cur, prev, cur, prev, cur],
        out_specs=[cur, cur],
        out_shape=[jax.ShapeDtypeStruct((L, dil * DIL_W), BF16),
                   jax.ShapeDtypeStruct((L, dil * DIL_W), F32)],
        compiler_params=_params("parallel", "arbitrary"),
        name=f"dilated_attn_d{dil}",
    )(bound, qd, kd, kd, vd, vd)


def _merge_kernel(x_ref, g_ref, wg_ref, oa_ref, od1_ref, od2_ref, od3_ref, l1_ref, l2_ref, l3_ref,
                  qm_ref, mk_ref, mv_ref, wba_ref, wbd_ref, wbm_ref, wo_ref, out_ref, nat_ref):
    x = x_ref[...]
    tm = x.shape[0]
    h = _rms_rows(x, g_ref[...]).astype(BF16)

    def natural(ref, dil, slot):
        if dil == 1:
            return ref[...].astype(F32)
        tiles = DIL_W // LANES
        for r in range(dil):
            for lt in range(tiles):
                col = r * DIL_W + lt * LANES
                nat_ref[slot * tiles + lt, pl.ds(r, tm // dil, stride=dil), :] = (
                    ref[:, col:col + LANES].astype(F32))
        return jnp.concatenate([nat_ref[slot * tiles + lt] for lt in range(tiles)], axis=1)

    lane = lax.broadcasted_iota(jnp.int32, (tm, LANES), 1)
    lo_half = lane < HEAD_DIM
    mem_pairs = MEM_W // LANES
    scores = []
    for pr in range(mem_pairs):
        q = qm_ref[:, pr * LANES:(pr + 1) * LANES]
        mk = mk_ref[:, pr * LANES:(pr + 1) * LANES]
        for hh in range(2):
            q_h = jnp.where(lo_half if hh == 0 else ~lo_half, q, jnp.zeros_like(q))
            scores.append(_dot_nt(q_h, mk))
    graw = [_dot(h, wg_ref[:, bi * D_MODEL:(bi + 1) * D_MODEL]) for bi in range(3)]

    dils = [d for _, d in DIL_PATTERNS]
    l1, l2, l3 = [natural(ref, d, k) for k, (ref, d) in enumerate(zip((l1_ref, l2_ref, l3_ref), dils))]
    o1, o2, o3 = [natural(ref, d, 3 + k) for k, (ref, d) in enumerate(zip((od1_ref, od2_ref, od3_ref), dils))]
    lmax = jnp.maximum(jnp.maximum(l1, l2), l3)
    e1, e2, e3 = jnp.exp(l1 - lmax), jnp.exp(l2 - lmax), jnp.exp(l3 - lmax)
    o_d = (e1 * o1 + e2 * o2 + e3 * o3) / (e1 + e2 + e3)

    probs = []
    for s in scores:
        p = jnp.exp(s - jnp.max(s, axis=-1, keepdims=True))
        probs.append((p / jnp.sum(p, axis=-1, keepdims=True)).astype(BF16))
    o_m = []
    for pr in range(mem_pairs):
        mv = mv_ref[:, pr * LANES:(pr + 1) * LANES]
        outs = [_dot(probs[2 * pr + hh], mv) for hh in range(2)]
        o_m.append(jnp.where(lo_half, outs[0], outs[1]))
    o_m = jnp.concatenate(o_m, axis=1)

    merged = None
    branches = ((oa_ref[...], wba_ref), (o_d.astype(BF16), wbd_ref), (o_m.astype(BF16), wbm_ref))
    for bi, (o_b, w_ref) in enumerate(branches):
        term = jax.nn.sigmoid(graw[bi]) * _dot(o_b, w_ref[...])
        merged = term if merged is None else merged + term
    out_ref[...] = x + _dot(merged.astype(BF16), wo_ref[...])


def _merge(x, g_mix, w_gate, o_a, o_ds, lses, qm, mk, mv, w_ba, w_bd, w_bm, w_out, S):
    tm = TM_MERGE
    const = lambda shape: pl.BlockSpec(shape, lambda i: (0,) * len(shape), pipeline_mode=pl.Buffered(1))
    rows = lambda w: pl.BlockSpec((tm, w), lambda i: (i, 0))
    dil_rows = [pl.BlockSpec((tm // d, d * DIL_W), lambda i: (i, 0)) for _, d in DIL_PATTERNS]
    return pl.pallas_call(
        _merge_kernel,
        grid=(S // tm,),
        in_specs=[rows(D_MODEL), const((1, D_MODEL)), const((D_MODEL, 3 * D_MODEL)),
                  rows(MOBA_W), *dil_rows, *dil_rows,
                  rows(MEM_W), const((N_MEM, MEM_W)), const((N_MEM, MEM_W)),
                  const((MOBA_W, D_MODEL)), const((DIL_W, D_MODEL)), const((MEM_W, D_MODEL)),
                  const((D_MODEL, D_MODEL))],
        out_specs=rows(D_MODEL),
        out_shape=jax.ShapeDtypeStruct((S, D_MODEL), F32),
        scratch_shapes=[pltpu.VMEM((2 * len(DIL_PATTERNS) * DIL_W // LANES, tm, LANES), F32)],
        compiler_params=_params("parallel"),
        name="gated_merge",
    )(x, g_mix, w_gate, o_a, *o_ds, *lses, qm, mk, mv, w_ba, w_bd, w_bm, w_out)


def _ffn_kernel(x_ref, g_ref, wup_ref, cw_ref, cb_ref, wdn_ref, out_ref, halo_ref, act_ref):
    i = pl.program_id(0)
    tm = x_ref.shape[0]

    @pl.when(i == 0)
    def _():
        halo_ref[...] = jnp.zeros_like(halo_ref)

    x = x_ref[...]
    h = _rms_rows(x, g_ref[...]).astype(BF16)
    row8 = lax.broadcasted_iota(jnp.int32, (8, FFN_CHUNK), 0)

    def conv(u, col):
        prev = halo_ref[:, col:col + FFN_CHUNK]
        halo_ref[:, col:col + FFN_CHUNK] = u[tm - 8:, :]
        p6 = jnp.broadcast_to(prev[6:7], (8, FFN_CHUNK))
        p7 = jnp.broadcast_to(prev[7:8], (8, FFN_CHUNK))
        r1 = pltpu.roll(u, 1, axis=0)
        r2 = pltpu.roll(u, 2, axis=0)
        top1 = jnp.where(row8 == 0, p7, r1[:8])
        top2 = jnp.where(row8 == 0, p6, jnp.where(row8 == 1, p7, r2[:8]))
        u1 = jnp.concatenate([top1, r1[8:]], axis=0)
        u2 = jnp.concatenate([top2, r2[8:]], axis=0)
        w = cw_ref[:, col:col + FFN_CHUNK]
        return cb_ref[:, col:col + FFN_CHUNK] + w[0:1] * u2 + w[1:2] * u1 + w[2:3] * u

    for c in range(D_FF // FFN_CHUNK):
        cg = c * FFN_CHUNK
        cv = D_FF + c * FFN_CHUNK
        u_g = conv(_dot(h, wup_ref[:, cg:cg + FFN_CHUNK]), cg)
        u_v = conv(_dot(h, wup_ref[:, cv:cv + FFN_CHUNK]), cv)
        act_ref[:, cg:cg + FFN_CHUNK] = (jax.nn.silu(u_g) * u_v).astype(BF16)
    out_ref[...] = x + _dot(act_ref[...], wdn_ref[...])


def _ffn(x, g_ffn, w_up, conv_w, conv_b, w_down, S):
    tm = TM_FFN
    const = lambda shape: pl.BlockSpec(shape, lambda i: (0,) * len(shape), pipeline_mode=pl.Buffered(1))
    rows = pl.BlockSpec((tm, D_MODEL), lambda i: (i, 0))
    return pl.pallas_call(
        _ffn_kernel,
        grid=(S // tm,),
        in_specs=[rows, const((1, D_MODEL)), const((D_MODEL, 2 * D_FF)),
                  const((CONV_WIDTH, 2 * D_FF)), const((1, 2 * D_FF)), const((D_FF, D_MODEL))],
        out_specs=rows,
        out_shape=jax.ShapeDtypeStruct((S, D_MODEL), F32),
        scratch_shapes=[pltpu.VMEM((8, 2 * D_FF), F32), pltpu.VMEM((tm, D_FF), BF16)],
        compiler_params=_params("arbitrary"),
        name="conv_ffn",
    )(x, g_ffn, w_up, conv_w, conv_b, w_down)


def _group_mean_matrix():
    g = np.arange(MXU_TILE) // HEAD_DIM
    return jnp.asarray((g[:, None] == g[None, :]).astype(np.float32) / HEAD_DIM, dtype=BF16)


def _layer(x, mem, positions, p):
    S = x.shape[0]
    assert S % (max(d for _, d in DIL_PATTERNS) * DIL_BLOCK) == 0 and S % TM_PROJ == 0
    row = lambda v: v.reshape(1, -1).astype(F32)
    c = np.cumsum([0, MOBA_W, MOBA_W, MOBA_W, DIL_W, DIL_W, DIL_W, MEM_W])
    w_in = p["w_in"]
    seg = lambda k: w_in[:, c[k]:c[k + 1]]
    wt = jnp.concatenate([seg(0), seg(2)], axis=1).T.astype(BF16)
    wn = jnp.concatenate([seg(1), seg(3), seg(4), seg(6), seg(5)], axis=1).astype(BF16)
    w_gate = w_in[:, QKV_COLS:].astype(BF16)
    gnat = jnp.concatenate([jnp.tile(p["moba_k_norm_g"], N_MOBA_HEADS), jnp.tile(p["dil_q_norm_g"], N_DIL_HEADS),
                            jnp.tile(p["dil_k_norm_g"], N_DIL_HEADS), jnp.tile(p["mem_q_norm_g"], N_MEM_HEADS),
                            jnp.ones((_NAT_NORM_PAD - _NAT_NORM,), F32)]).reshape(1, _NAT_NORM_PAD)
    gmat = _group_mean_matrix()

    cos_t, sin_t, cos_n, sin_n = _rope_tables(positions, S)
    mk, mv = _mem_kv(mem, row(p["mem_norm_g"]), p["w_mem_kv"].astype(BF16),
                     row(jnp.tile(p["mem_k_norm_g"], N_MEM_HEADS)), gmat)
    (qat, vat, ka, kmean, qm), dil_qkv = _proj(
        x, row(p["mix_norm_g"]), wt, wn, p["moba_q_norm_g"].reshape(HEAD_DIM, 1).astype(F32), gnat, gmat,
        cos_t, sin_t, cos_n, sin_n, S)
    def score_bound(gq, gk):
        return (1.02 * HEAD_DIM * SCALE * jnp.max(jnp.abs(gq)) * jnp.max(jnp.abs(gk))).reshape(1).astype(F32)

    o_a = _moba(score_bound(p["moba_q_norm_g"], p["moba_k_norm_g"]), qat, ka, vat,
                kmean.reshape(S // MOBA_BLOCK, MOBA_W), S)
    bound_d = score_bound(p["dil_q_norm_g"], p["dil_k_norm_g"])
    o_ds, lses = zip(*[_dilated(bound_d, *qkv, w, d, S) for qkv, (w, d) in zip(dil_qkv, DIL_PATTERNS)])
    x1 = _merge(x, row(p["mix_norm_g"]), w_gate, o_a, o_ds, lses, qm, mk, mv,
                p["w_branch_moba"].astype(BF16), p["w_branch_dil"].astype(BF16),
                p["w_branch_mem"].astype(BF16), p["w_out"].astype(BF16), S)
    return _ffn(x1, row(p["ffn_norm_g"]), p["w_ffn_up"].astype(BF16), p["ffn_conv_w"].astype(F32),
                row(p["ffn_conv_b"]), p["w_ffn_down"].astype(BF16), S)


def kernel(x, mem, positions, mix_norm_g, mem_norm_g, w_in, moba_q_norm_g, moba_k_norm_g, dil_q_norm_g, dil_k_norm_g, mem_q_norm_g, mem_k_norm_g, w_mem_kv, w_branch_moba, w_branch_dil, w_branch_mem, w_out, ffn_norm_g, w_ffn_up, ffn_conv_w, ffn_conv_b, w_ffn_down):
    params = dict(mix_norm_g=mix_norm_g, mem_norm_g=mem_norm_g, w_in=w_in, moba_q_norm_g=moba_q_norm_g,
                  moba_k_norm_g=moba_k_norm_g, dil_q_norm_g=dil_q_norm_g, dil_k_norm_g=dil_k_norm_g,
                  mem_q_norm_g=mem_q_norm_g, mem_k_norm_g=mem_k_norm_g, w_mem_kv=w_mem_kv,
                  w_branch_moba=w_branch_moba, w_branch_dil=w_branch_dil, w_branch_mem=w_branch_mem,
                  w_out=w_out, ffn_norm_g=ffn_norm_g, w_ffn_up=w_ffn_up, ffn_conv_w=ffn_conv_w,
                  ffn_conv_b=ffn_conv_b, w_ffn_down=w_ffn_down)
    B = x.shape[0]
    depth = w_in.shape[0]
    outs = []
    for b in range(B):
        xb = x[b]
        for l in range(depth):
            xb = _layer(xb, mem[b], positions[b], {k: v[l] for k, v in params.items()})
        outs.append(xb)
    return jnp.stack(outs, axis=0)
```

```python
import functools

import numpy as np
import jax
import jax.numpy as jnp
from jax import lax
from jax.experimental import pallas as pl
from jax.experimental.pallas import tpu as pltpu

D_MODEL = 1024
HEAD_DIM = 64
HALF = HEAD_DIM // 2
N_MOBA_HEADS = 6
N_DIL_HEADS = 6
N_MEM_HEADS = 4
N_MEM = 256
MOBA_BLOCK = 256
MOBA_TOPK = 3
DIL_PATTERNS = ((128, 1), (512, 4), (2048, 16))
DIL_BLOCK = 128
D_FF = 2816
CONV_WIDTH = 3
ROPE_THETA = 10000.0
EPS = 1e-6
MOBA_W = N_MOBA_HEADS * HEAD_DIM
DIL_W = N_DIL_HEADS * HEAD_DIM
MEM_W = N_MEM_HEADS * HEAD_DIM
QKV_COLS = 3 * MOBA_W + 3 * DIL_W + MEM_W
SCALE = HEAD_DIM ** -0.5

LANES = 128
MXU_TILE = 256
VMEM_LIMIT = 56 * 1024 * 1024

TM_ROPE = 2048
TM_PROJ = 512
TM_MERGE = 512
TM_FFN = 512
FFN_CHUNK = 256
DIL_ROWS = 1024

F32 = jnp.float32
BF16 = jnp.bfloat16
NEG_INF = float("-inf")


def _dot(a, b):
    return jnp.dot(a, b, preferred_element_type=F32)


def _dot_nt(a, b):
    return lax.dot_general(a, b, (((1,), (1,)), ((), ())), preferred_element_type=F32)


def _rms_rows(x, g):
    ms = jnp.mean(x * x, axis=-1, keepdims=True)
    return x * lax.rsqrt(ms + EPS) * g


def _params(*sem):
    return pltpu.CompilerParams(dimension_semantics=sem, vmem_limit_bytes=VMEM_LIMIT)


def _rope_kernel(pos_ref, inv_ref, cos_t_ref, sin_t_ref, cos_n_ref, sin_n_ref):
    pos = pos_ref[...].astype(F32)
    ang = inv_ref[...] * pos
    c = jnp.cos(ang)
    s = jnp.sin(ang)
    cos_t_ref[...] = jnp.concatenate([c, c], axis=0)
    sin_t_ref[...] = jnp.concatenate([-s, s], axis=0)
    cos_n_ref[...] = jnp.concatenate([c, c, c, c], axis=0).T
    sin_n_ref[...] = jnp.concatenate([-s, s, -s, s], axis=0).T


def _rope_tables(positions, S):
    tm = min(TM_ROPE, S)
    inv = (ROPE_THETA ** (-jnp.arange(HALF, dtype=F32) / HALF)).reshape(HALF, 1)
    return pl.pallas_call(
        _rope_kernel,
        grid=(S // tm,),
        in_specs=[pl.BlockSpec((1, tm), lambda i: (0, i)),
                  pl.BlockSpec((HALF, 1), lambda i: (0, 0))],
        out_specs=[pl.BlockSpec((HEAD_DIM, tm), lambda i: (0, i)),
                   pl.BlockSpec((HEAD_DIM, tm), lambda i: (0, i)),
                   pl.BlockSpec((tm, LANES), lambda i: (i, 0)),
                   pl.BlockSpec((tm, LANES), lambda i: (i, 0))],
        out_shape=[jax.ShapeDtypeStruct((HEAD_DIM, S), F32),
                   jax.ShapeDtypeStruct((HEAD_DIM, S), F32),
                   jax.ShapeDtypeStruct((S, LANES), F32),
                   jax.ShapeDtypeStruct((S, LANES), F32)],
        compiler_params=_params("parallel"),
        name="rope_tables",
    )(positions.reshape(1, S), inv)


def _memkv_kernel(mem_ref, g_ref, w_ref, gk_ref, gmat_ref, mk_ref, mv_ref):
    h = _rms_rows(mem_ref[...], g_ref[...]).astype(BF16)
    kv = _dot(h, w_ref[...])
    k = kv[:, :MEM_W]
    ms = _dot((k * k).astype(BF16), gmat_ref[...])
    mk_ref[...] = (k * lax.rsqrt(ms + EPS) * gk_ref[...]).astype(BF16)
    mv_ref[...] = kv[:, MEM_W:].astype(BF16)


def _mem_kv(mem, g, w_kv, gk_tiled, gmat):
    full = lambda shape: pl.BlockSpec(shape, lambda i: (0,) * len(shape))
    return pl.pallas_call(
        _memkv_kernel,
        grid=(1,),
        in_specs=[full((N_MEM, D_MODEL)), full((1, D_MODEL)), full((D_MODEL, 2 * MEM_W)),
                  full((1, MEM_W)), full((MXU_TILE, MXU_TILE))],
        out_specs=[full((N_MEM, MEM_W)), full((N_MEM, MEM_W))],
        out_shape=[jax.ShapeDtypeStruct((N_MEM, MEM_W), BF16)] * 2,
        compiler_params=_params("arbitrary"),
        name="mem_kv",
    )(mem, g, w_kv, gk_tiled, gmat)


_NAT_NORM = 3 * 384 + MEM_W
_NAT_NORM_PAD = 1536
_NAT_ROPE = 3 * 384
_NAT_COLS = _NAT_NORM + DIL_W


def _proj_kernel(x_ref, g_ref, wt_ref, wn_ref, gq_ref, gnat_ref, gmat_ref,
                 cos_t_ref, sin_t_ref, cos_n_ref, sin_n_ref,
                 qat_ref, vat_ref, ka_ref, km_ref, qm_ref, *rest):
    dil_refs, stage_ref = rest[:-1], rest[-1]
    tm = x_ref.shape[0]
    h = _rms_rows(x_ref[...], g_ref[...]).astype(BF16)

    yt = _dot_nt(wt_ref[...], h)
    cos_t = cos_t_ref[...]
    sin_t = sin_t_ref[...]
    gq = gq_ref[...]
    for hd in range(N_MOBA_HEADS):
        q = yt[hd * HEAD_DIM:(hd + 1) * HEAD_DIM]
        ms = jnp.mean(q * q, axis=0, keepdims=True)
        q = q * lax.rsqrt(ms + EPS) * gq
        rot = jnp.concatenate([q[HALF:], q[:HALF]], axis=0)
        q = (q * cos_t + rot * sin_t) * SCALE
        for b in range(tm // MOBA_BLOCK):
            qat_ref[b, hd * HEAD_DIM:(hd + 1) * HEAD_DIM, :] = (
                q[:, b * MOBA_BLOCK:(b + 1) * MOBA_BLOCK].astype(BF16))
    for b in range(tm // MOBA_BLOCK):
        vat_ref[b] = yt[MOBA_W:, b * MOBA_BLOCK:(b + 1) * MOBA_BLOCK].astype(BF16)

    gmat = gmat_ref[...]
    lane = lax.broadcasted_iota(jnp.int32, (tm, LANES), 1)
    first_half = (lane % HEAD_DIM) < HALF
    cos_n = cos_n_ref[...]
    sin_n = sin_n_ref[...]
    pairs = DIL_W // LANES

    def emit_dilated(ti, tiles):
        for lt in range(pairs):
            stage_ref[ti * pairs + lt] = tiles[lt]
        for pi, (_, d) in enumerate(DIL_PATTERNS):
            ref = dil_refs[3 * pi + ti]
            for r in range(d):
                for lt in range(pairs):
                    col = r * DIL_W + lt * LANES
                    ref[:, col:col + LANES] = stage_ref[
                        ti * pairs + lt, pl.ds(r, tm // d, stride=d), :].astype(BF16)

    chunks = []
    for c in range(_NAT_NORM_PAD // MXU_TILE):
        if c % 2 == 0:
            wide = _dot(h, wn_ref[:, c * MXU_TILE:(c + 2) * MXU_TILE])
        raw = wide[:, (c % 2) * MXU_TILE:(c % 2 + 1) * MXU_TILE]
        ms = _dot((raw * raw).astype(BF16), gmat)
        blk = raw * lax.rsqrt(ms + EPS) * gnat_ref[:, c * MXU_TILE:(c + 1) * MXU_TILE]
        for hc in range(MXU_TILE // LANES):
            col = c * MXU_TILE + hc * LANES
            if col >= _NAT_NORM:
                chunks.append(raw[:, hc * LANES:(hc + 1) * LANES])
                continue
            v = blk[:, hc * LANES:(hc + 1) * LANES]
            if col < _NAT_ROPE:
                rot = jnp.where(first_half, pltpu.roll(v, LANES - HALF, axis=1),
                                pltpu.roll(v, HALF, axis=1))
                v = v * cos_n + rot * sin_n
            chunks.append(v)
        if len(chunks) >= pairs and c == 1:
            ka = jnp.concatenate(chunks[0:pairs], axis=1)
            for b in range(tm // MOBA_BLOCK):
                km_ref[b] = jnp.mean(ka[b * MOBA_BLOCK:(b + 1) * MOBA_BLOCK], axis=0, keepdims=True)
            ka_ref[...] = ka.astype(BF16)
        if c == 2:
            emit_dilated(0, [t * SCALE for t in chunks[pairs:2 * pairs]])
        if c == 4:
            emit_dilated(1, chunks[2 * pairs:3 * pairs])
    qm_ref[...] = (jnp.concatenate(chunks[3 * pairs:3 * pairs + MEM_W // LANES], axis=1) * SCALE).astype(BF16)
    vd_rest = _dot(h, wn_ref[:, _NAT_NORM_PAD:])
    emit_dilated(2, [chunks[-1]] + [vd_rest[:, t * LANES:(t + 1) * LANES] for t in range(pairs - 1)])


def _proj(x, g_mix, wt, wn, gq_col, gnat, gmat, cos_t, sin_t, cos_n, sin_n, S):
    tm = TM_PROJ
    nb = S // MOBA_BLOCK
    bpt = tm // MOBA_BLOCK
    const = lambda shape: pl.BlockSpec(shape, lambda i: (0,) * len(shape))
    rows = lambda w: pl.BlockSpec((tm, w), lambda i: (i, 0))
    dils = [d for _, d in DIL_PATTERNS for _ in range(3)]
    outs = pl.pallas_call(
        _proj_kernel,
        grid=(S // tm,),
        in_specs=[rows(D_MODEL), const((1, D_MODEL)), const((2 * MOBA_W, D_MODEL)),
                  const((D_MODEL, _NAT_COLS)), const((HEAD_DIM, 1)), const((1, _NAT_NORM_PAD)),
                  const((MXU_TILE, MXU_TILE)),
                  pl.BlockSpec((HEAD_DIM, tm), lambda i: (0, i)),
                  pl.BlockSpec((HEAD_DIM, tm), lambda i: (0, i)),
                  rows(LANES), rows(LANES)],
        out_specs=[pl.BlockSpec((bpt, MOBA_W, MOBA_BLOCK), lambda i: (i, 0, 0)),
                   pl.BlockSpec((bpt, MOBA_W, MOBA_BLOCK), lambda i: (i, 0, 0)),
                   rows(MOBA_W),
                   pl.BlockSpec((bpt, 1, MOBA_W), lambda i: (i, 0, 0)),
                   rows(MEM_W)]
                  + [pl.BlockSpec((tm // d, d * DIL_W), lambda i: (i, 0)) for d in dils],
        out_shape=[jax.ShapeDtypeStruct((nb, MOBA_W, MOBA_BLOCK), BF16),
                   jax.ShapeDtypeStruct((nb, MOBA_W, MOBA_BLOCK), BF16),
                   jax.ShapeDtypeStruct((S, MOBA_W), BF16),
                   jax.ShapeDtypeStruct((nb, 1, MOBA_W), F32),
                   jax.ShapeDtypeStruct((S, MEM_W), BF16)]
                  + [jax.ShapeDtypeStruct((S // d, d * DIL_W), BF16) for d in dils],
        scratch_shapes=[pltpu.VMEM((3 * DIL_W // LANES, tm, LANES), F32)],
        compiler_params=_params("parallel"),
        name="in_proj",
    )(x, g_mix, wt, wn, gq_col, gnat, gmat, cos_t, sin_t, cos_n, sin_n)
    return outs[:5], [outs[5 + 3 * k:8 + 3 * k] for k in range(len(DIL_PATTERNS))]


_SHIFT_HEADROOM = 30.0
_FAST_BOUND = 50.0
_MOBA_UNROLL = 4


def _moba_kernel(bound_ref, qt_ref, k_ref, vt_ref, km_ref, o_ref, bias_ref, ot_ref):
    i = pl.program_id(0)
    tq = MOBA_BLOCK
    unroll = _MOBA_UNROLL
    heads = N_MOBA_HEADS
    nb = km_ref.shape[0]
    row = lax.broadcasted_iota(jnp.int32, (LANES, tq), 0)
    blk = lax.broadcasted_iota(jnp.int32, (nb, tq), 0)

    bound = bound_ref[0]
    fast = bound <= _FAST_BOUND
    sel_bias = jnp.where(fast, _SHIFT_HEADROOM - bound, 0.0)

    def pair_cols(hd):
        return slice((hd // 2) * LANES, (hd // 2 + 1) * LANES)

    qs = []
    for hd in range(heads):
        qt = qt_ref[0, pair_cols(hd), :]
        q_h = jnp.where((row // HEAD_DIM) == hd % 2, qt, jnp.zeros_like(qt))
        qs.append(q_h)
        km = km_ref[:, pair_cols(hd)]
        km1 = km.astype(BF16)
        r1 = km - km1.astype(F32)
        km2 = r1.astype(BF16)
        km3 = (r1 - km2.astype(F32)).astype(BF16)
        g3 = _dot(jnp.concatenate([km1, km2, km3], axis=0), q_h)
        gate = g3[:nb] + g3[nb:2 * nb] + g3[2 * nb:]
        gate = jnp.where(blk < i, gate, NEG_INF)
        bias = jnp.full((nb, tq), NEG_INF, F32)
        for _ in range(MOBA_TOPK):
            m = jnp.max(gate, axis=0, keepdims=True)
            idx = jnp.min(jnp.where(gate == m, blk, nb), axis=0, keepdims=True)
            hit = blk == idx
            bias = jnp.where(hit & (m > NEG_INF), sel_bias, bias)
            gate = jnp.where(hit, NEG_INF, gate)
        bias_ref[hd] = bias

    kpos = lax.broadcasted_iota(jnp.int32, (tq, tq), 0)
    qpos = lax.broadcasted_iota(jnp.int32, (tq, tq), 1)
    causal = kpos <= qpos

    def k_rows(j, hd):
        return k_ref[pl.ds(pl.multiple_of(j * tq, tq), tq), pair_cols(hd)]

    def v_rows(j, hd):
        return vt_ref[j, hd * HEAD_DIM:(hd + 1) * HEAD_DIM, :]

    ones = jnp.ones((16, tq), BF16)

    def values(j, hd, p):
        return _dot(jnp.concatenate([v_rows(j, hd), ones], axis=0), p)

    own_s = [_dot(k_rows(i, hd), qs[hd]) for hd in range(heads)]
    own_p = [jnp.exp(jnp.where(causal, s + sel_bias, NEG_INF)).astype(BF16) for s in own_s]
    own_accs = [values(i, hd, p) for hd, p in enumerate(own_p)]

    @pl.when(fast)
    def _():
        def body(t, accs):
            out = list(accs)
            chains = [(unroll * t + u, hd) for u in range(unroll) for hd in range(heads)]
            ss = [_dot(k_rows(j, hd), qs[hd]) for j, hd in chains]
            ps = [jnp.exp(s + bias_ref[hd, pl.ds(j, 1), :]).astype(BF16) for s, (j, hd) in zip(ss, chains)]
            for p, (j, hd) in zip(ps, chains):
                out[hd] = out[hd] + values(j, hd, p)
            return tuple(out)

        accs = lax.fori_loop(0, (i + unroll - 1) // unroll, body, tuple(own_accs))
        for hd, a in enumerate(accs):
            ot_ref[hd * HEAD_DIM:(hd + 1) * HEAD_DIM, :] = a[:HEAD_DIM] / a[HEAD_DIM:HEAD_DIM + 1]

    @pl.when(jnp.logical_not(fast))
    def _():
        for pr in range(heads // 2):
            pair = (2 * pr, 2 * pr + 1)
            carry = []
            for hd in pair:
                s = jnp.where(causal, _dot(k_rows(i, hd), qs[hd]), NEG_INF)
                m = jnp.max(s, axis=0, keepdims=True)
                p = jnp.exp(s - m)
                l = jnp.sum(p, axis=0, keepdims=True)
                carry += [m, l, _dot(v_rows(i, hd), p.astype(BF16))]

            def body(j, carry):
                out = []
                for c, hd in enumerate(pair):
                    m, l, acc = carry[3 * c:3 * c + 3]
                    s = _dot(k_rows(j, hd), qs[hd]) + bias_ref[hd, pl.ds(j, 1), :]
                    m_new = jnp.maximum(m, jnp.max(s, axis=0, keepdims=True))
                    alpha = jnp.exp(m - m_new)
                    p = jnp.exp(s - m_new)
                    l = alpha * l + jnp.sum(p, axis=0, keepdims=True)
                    acc = alpha * acc + _dot(v_rows(j, hd), p.astype(BF16))
                    out += [m_new, l, acc]
                return tuple(out)

            carry = lax.fori_loop(0, i, body, tuple(carry))
            for c, hd in enumerate(pair):
                ot_ref[hd * HEAD_DIM:(hd + 1) * HEAD_DIM, :] = carry[3 * c + 2] / carry[3 * c + 1]

    o_ref[...] = ot_ref[...].T.astype(BF16)


def _moba(bound, qat, ka, vat, kmean, S):
    nb = S // MOBA_BLOCK
    assert nb % _MOBA_UNROLL == 0
    once = lambda shape: pl.BlockSpec(shape, lambda i: (0,) * len(shape), pipeline_mode=pl.Buffered(1))
    return pl.pallas_call(
        _moba_kernel,
        grid=(nb,),
        in_specs=[pl.BlockSpec(memory_space=pltpu.SMEM),
                  pl.BlockSpec((1, MOBA_W, MOBA_BLOCK), lambda i: (i, 0, 0)),
                  once((S, MOBA_W)), once((nb, MOBA_W, MOBA_BLOCK)), once((nb, MOBA_W))],
        out_specs=pl.BlockSpec((MOBA_BLOCK, MOBA_W), lambda i: (i, 0)),
        out_shape=jax.ShapeDtypeStruct((S, MOBA_W), BF16),
        scratch_shapes=[pltpu.VMEM((N_MOBA_HEADS, nb, MOBA_BLOCK), F32),
                        pltpu.VMEM((MOBA_W, MOBA_BLOCK), F32)],
        compiler_params=_params("arbitrary"),
        name="moba_attn",
    )(bound, qat, ka, vat, kmean)


def _dil_kernel(span, bound_ref, q_ref, kp_ref, kc_ref, vp_ref, vc_ref, o_ref, lse_ref):
    n = pl.program_id(1)
    blk = DIL_BLOCK
    nsub = q_ref.shape[0] // blk
    lane = lax.broadcasted_iota(jnp.int32, (blk, LANES), 1)
    lo_half = lane < HEAD_DIM
    qi = lax.broadcasted_iota(jnp.int32, (2 * blk, 2 * blk), 0) % blk + blk
    kj = lax.broadcasted_iota(jnp.int32, (2 * blk, 2 * blk), 1)
    dist = qi - kj
    band = jnp.where((dist >= 0) & (dist <= span), 0.0, NEG_INF)
    band_first = jnp.where(kj >= blk, band, NEG_INF)
    bound = bound_ref[0]
    fast = bound <= _FAST_BOUND
    shift = bound - _SHIFT_HEADROOM

    def blocks(fixed_shift):
        sub = shift if fixed_shift else 0.0
        mask_first = jnp.where(n > 0, band, band_first) - sub
        mask_rest = band - sub
        for pr in range(DIL_W // LANES):
            for b in range(nsub):
                block(fixed_shift, b, slice(pr * LANES, (pr + 1) * LANES), mask_first if b == 0 else mask_rest)

    def block(fixed_shift, b, cols, mask):
        q = q_ref[b * blk:(b + 1) * blk, cols]
        if b == 0:
            kk = jnp.concatenate([kp_ref[:, cols], kc_ref[0:blk, cols]], axis=0)
            vv = jnp.concatenate([vp_ref[:, cols], vc_ref[0:blk, cols]], axis=0)
        else:
            kk = kc_ref[(b - 1) * blk:(b + 1) * blk, cols]
            vv = vc_ref[(b - 1) * blk:(b + 1) * blk, cols]
        zero = jnp.zeros_like(q)
        q2 = jnp.concatenate([jnp.where(lo_half, q, zero), jnp.where(lo_half, zero, q)], axis=0)
        s = _dot_nt(q2, kk) + mask
        if fixed_shift:
            p = jnp.exp(s).astype(BF16)
            ov = _dot(p, jnp.concatenate([vv, jnp.ones_like(vv)], axis=1))
            den = ov[:, LANES:]
            o2 = ov[:, :LANES] / den
            lse2 = shift + jnp.log(den)
        else:
            m = jnp.max(s, axis=-1, keepdims=True)
            p = jnp.exp(s - m)
            den = jnp.sum(p, axis=-1, keepdims=True)
            o2 = _dot(p.astype(BF16), vv) / den
            lse2 = jnp.broadcast_to(m + jnp.log(den), (2 * blk, LANES))
        o_ref[b * blk:(b + 1) * blk, cols] = jnp.where(lo_half, o2[:blk], o2[blk:]).astype(BF16)
        lse_ref[b * blk:(b + 1) * blk, cols] = jnp.where(lo_half, lse2[:blk], lse2[blk:])

    pl.when(fast)(lambda: blocks(True))
    pl.when(jnp.logical_not(fast))(lambda: blocks(False))


def _dilated(bound, qd, kd, vd, window, dil, S):
    span = window // dil
    L = S // dil
    rows = min(DIL_ROWS, L)
    sub = rows // DIL_BLOCK
    cur = pl.BlockSpec((rows, DIL_W), lambda r, n: (n, r))
    prev = pl.BlockSpec((DIL_BLOCK, DIL_W), lambda r, n: (jnp.maximum(n * sub - 1, 0), r))
    return pl.pallas_call(
        functools.partial(_dil_kernel, span),
        grid=(dil, L // rows),
        in_specs=[pl.BlockSpec(memory_space=pltpu.SMEM), cur, prev, cur, prev, cur],
        out_specs=[cur, cur],
        out_shape=[jax.ShapeDtypeStruct((L, dil * DIL_W), BF16),
                   jax.ShapeDtypeStruct((L, dil * DIL_W), F32)],
        compiler_params=_params("parallel", "arbitrary"),
        name=f"dilated_attn_d{dil}",
    )(bound, qd, kd, kd, vd, vd)


def _merge_kernel(x_ref, g_ref, wg_ref, oa_ref, od1_ref, od2_ref, od3_ref, l1_ref, l2_ref, l3_ref,
                  qm_ref, mk_ref, mv_ref, wba_ref, wbd_ref, wbm_ref, wo_ref, out_ref, nat_ref):
    x = x_ref[...]
    tm = x.shape[0]
    h = _rms_rows(x, g_ref[...]).astype(BF16)

    def natural(ref, dil, slot):
        if dil == 1:
            return ref[...].astype(F32)
        tiles = DIL_W // LANES
        for r in range(dil):
            for lt in range(tiles):
                col = r * DIL_W + lt * LANES
                nat_ref[slot * tiles + lt, pl.ds(r, tm // dil, stride=dil), :] = (
                    ref[:, col:col + LANES].astype(F32))
        return jnp.concatenate([nat_ref[slot * tiles + lt] for lt in range(tiles)], axis=1)

    lane = lax.broadcasted_iota(jnp.int32, (tm, LANES), 1)
    lo_half = lane < HEAD_DIM
    mem_pairs = MEM_W // LANES
    scores = []
    for pr in range(mem_pairs):
        q = qm_ref[:, pr * LANES:(pr + 1) * LANES]
        zero = jnp.zeros_like(q)
        q2 = jnp.concatenate([jnp.where(lo_half, q, zero), jnp.where(lo_half, zero, q)], axis=0)
        scores.append(_dot_nt(q2, mk_ref[:, pr * LANES:(pr + 1) * LANES]))
    graw = [_dot(h, wg_ref[:, bi * D_MODEL:(bi + 1) * D_MODEL]) for bi in range(3)]

    dils = [d for _, d in DIL_PATTERNS]
    l1, l2, l3 = [natural(ref, d, k) for k, (ref, d) in enumerate(zip((l1_ref, l2_ref, l3_ref), dils))]
    o1, o2, o3 = [natural(ref, d, 3 + k) for k, (ref, d) in enumerate(zip((od1_ref, od2_ref, od3_ref), dils))]
    lmax = jnp.maximum(jnp.maximum(l1, l2), l3)
    e1, e2, e3 = jnp.exp(l1 - lmax), jnp.exp(l2 - lmax), jnp.exp(l3 - lmax)
    o_d = (e1 * o1 + e2 * o2 + e3 * o3) / (e1 + e2 + e3)

    probs = []
    for s in scores:
        p = jnp.exp(s - jnp.max(s, axis=-1, keepdims=True))
        probs.append((p / jnp.sum(p, axis=-1, keepdims=True)).astype(BF16))
    o_m = []
    for pr in range(mem_pairs):
        o2 = _dot(probs[pr], mv_ref[:, pr * LANES:(pr + 1) * LANES])
        o_m.append(jnp.where(lo_half, o2[:tm], o2[tm:]))
    o_m = jnp.concatenate(o_m, axis=1)

    merged = None
    branches = ((oa_ref[...], wba_ref), (o_d.astype(BF16), wbd_ref), (o_m.astype(BF16), wbm_ref))
    for bi, (o_b, w_ref) in enumerate(branches):
        term = jax.nn.sigmoid(graw[bi]) * _dot(o_b, w_ref[...])
        merged = term if merged is None else merged + term
    out_ref[...] = x + _dot(merged.astype(BF16), wo_ref[...])


def _merge(x, g_mix, w_gate, o_a, o_ds, lses, qm, mk, mv, w_ba, w_bd, w_bm, w_out, S):
    tm = TM_MERGE
    const = lambda shape: pl.BlockSpec(shape, lambda i: (0,) * len(shape), pipeline_mode=pl.Buffered(1))
    rows = lambda w: pl.BlockSpec((tm, w), lambda i: (i, 0))
    dil_rows = [pl.BlockSpec((tm // d, d * DIL_W), lambda i: (i, 0)) for _, d in DIL_PATTERNS]
    return pl.pallas_call(
        _merge_kernel,
        grid=(S // tm,),
        in_specs=[rows(D_MODEL), const((1, D_MODEL)), const((D_MODEL, 3 * D_MODEL)),
                  rows(MOBA_W), *dil_rows, *dil_rows,
                  rows(MEM_W), const((N_MEM, MEM_W)), const((N_MEM, MEM_W)),
                  const((MOBA_W, D_MODEL)), const((DIL_W, D_MODEL)), const((MEM_W, D_MODEL)),
                  const((D_MODEL, D_MODEL))],
        out_specs=rows(D_MODEL),
        out_shape=jax.ShapeDtypeStruct((S, D_MODEL), F32),
        scratch_shapes=[pltpu.VMEM((2 * len(DIL_PATTERNS) * DIL_W // LANES, tm, LANES), F32)],
        compiler_params=_params("parallel"),
        name="gated_merge",
    )(x, g_mix, w_gate, o_a, *o_ds, *lses, qm, mk, mv, w_ba, w_bd, w_bm, w_out)


def _ffn_kernel(x_ref, g_ref, wup_ref, cw_ref, cb_ref, wdn_ref, out_ref, halo_ref, act_ref):
    i = pl.program_id(0)
    tm = x_ref.shape[0]

    @pl.when(i == 0)
    def _():
        halo_ref[...] = jnp.zeros_like(halo_ref)

    x = x_ref[...]
    h = _rms_rows(x, g_ref[...]).astype(BF16)

    row8 = lax.broadcasted_iota(jnp.int32, (8, FFN_CHUNK), 0)

    def conv(u, col):
        prev = halo_ref[:, col:col + FFN_CHUNK]
        halo_ref[:, col:col + FFN_CHUNK] = u[tm - 8:, :]
        p6 = jnp.broadcast_to(prev[6:7], (8, FFN_CHUNK))
        p7 = jnp.broadcast_to(prev[7:8], (8, FFN_CHUNK))
        r1 = pltpu.roll(u, 1, axis=0)
        r2 = pltpu.roll(u, 2, axis=0)
        top1 = jnp.where(row8 == 0, p7, r1[:8])
        top2 = jnp.where(row8 == 0, p6, jnp.where(row8 == 1, p7, r2[:8]))
        u1 = jnp.concatenate([top1, r1[8:]], axis=0)
        u2 = jnp.concatenate([top2, r2[8:]], axis=0)
        w = cw_ref[:, col:col + FFN_CHUNK]
        return cb_ref[:, col:col + FFN_CHUNK] + w[0:1] * u2 + w[1:2] * u1 + w[2:3] * u

    for c in range(D_FF // FFN_CHUNK):
        cg = c * FFN_CHUNK
        cv = D_FF + c * FFN_CHUNK
        u_g = conv(_dot(h, wup_ref[:, cg:cg + FFN_CHUNK]), cg)
        u_v = conv(_dot(h, wup_ref[:, cv:cv + FFN_CHUNK]), cv)
        act_ref[:, cg:cg + FFN_CHUNK] = (jax.nn.silu(u_g) * u_v).astype(BF16)
    out_ref[...] = x + _dot(act_ref[...], wdn_ref[...])


def _ffn(x, g_ffn, w_up, conv_w, conv_b, w_down, S):
    tm = TM_FFN
    const = lambda shape: pl.BlockSpec(shape, lambda i: (0,) * len(shape), pipeline_mode=pl.Buffered(1))
    rows = pl.BlockSpec((tm, D_MODEL), lambda i: (i, 0))
    return pl.pallas_call(
        _ffn_kernel,
        grid=(S // tm,),
        in_specs=[rows, const((1, D_MODEL)), const((D_MODEL, 2 * D_FF)),
                  const((CONV_WIDTH, 2 * D_FF)), const((1, 2 * D_FF)), const((D_FF, D_MODEL))],
        out_specs=rows,
        out_shape=jax.ShapeDtypeStruct((S, D_MODEL), F32),
        scratch_shapes=[pltpu.VMEM((8, 2 * D_FF), F32), pltpu.VMEM((tm, D_FF), BF16)],
        compiler_params=_params("arbitrary"),
        name="conv_ffn",
    )(x, g_ffn, w_up, conv_w, conv_b, w_down)


def _group_mean_matrix():
    g = np.arange(MXU_TILE) // HEAD_DIM
    return jnp.asarray((g[:, None] == g[None, :]).astype(np.float32) / HEAD_DIM, dtype=BF16)


def _layer(x, mem, positions, p):
    S = x.shape[0]
    assert S % (max(d for _, d in DIL_PATTERNS) * DIL_BLOCK) == 0 and S % TM_PROJ == 0
    row = lambda v: v.reshape(1, -1).astype(F32)
    c = np.cumsum([0, MOBA_W, MOBA_W, MOBA_W, DIL_W, DIL_W, DIL_W, MEM_W])
    w_in = p["w_in"]
    seg = lambda k: w_in[:, c[k]:c[k + 1]]
    wt = jnp.concatenate([seg(0), seg(2)], axis=1).T.astype(BF16)
    wn = jnp.concatenate([seg(1), seg(3), seg(4), seg(6), seg(5)], axis=1).astype(BF16)
    w_gate = w_in[:, QKV_COLS:].astype(BF16)
    gnat = jnp.concatenate([jnp.tile(p["moba_k_norm_g"], N_MOBA_HEADS), jnp.tile(p["dil_q_norm_g"], N_DIL_HEADS),
                            jnp.tile(p["dil_k_norm_g"], N_DIL_HEADS), jnp.tile(p["mem_q_norm_g"], N_MEM_HEADS),
                            jnp.ones((_NAT_NORM_PAD - _NAT_NORM,), F32)]).reshape(1, _NAT_NORM_PAD)
    gmat = _group_mean_matrix()

    cos_t, sin_t, cos_n, sin_n = _rope_tables(positions, S)
    mk, mv = _mem_kv(mem, row(p["mem_norm_g"]), p["w_mem_kv"].astype(BF16),
                     row(jnp.tile(p["mem_k_norm_g"], N_MEM_HEADS)), gmat)
    (qat, vat, ka, kmean, qm), dil_qkv = _proj(
        x, row(p["mix_norm_g"]), wt, wn, p["moba_q_norm_g"].reshape(HEAD_DIM, 1).astype(F32), gnat, gmat,
        cos_t, sin_t, cos_n, sin_n, S)
    def score_bound(gq, gk):
        return (1.02 * HEAD_DIM * SCALE * jnp.max(jnp.abs(gq)) * jnp.max(jnp.abs(gk))).reshape(1).astype(F32)

    o_a = _moba(score_bound(p["moba_q_norm_g"], p["moba_k_norm_g"]), qat, ka, vat,
                kmean.reshape(S // MOBA_BLOCK, MOBA_W), S)
    bound_d = score_bound(p["dil_q_norm_g"], p["dil_k_norm_g"])
    o_ds, lses = zip(*[_dilated(bound_d, *qkv, w, d, S) for qkv, (w, d) in zip(dil_qkv, DIL_PATTERNS)])
    x1 = _merge(x, row(p["mix_norm_g"]), w_gate, o_a, o_ds, lses, qm, mk, mv,
                p["w_branch_moba"].astype(BF16), p["w_branch_dil"].astype(BF16),
                p["w_branch_mem"].astype(BF16), p["w_out"].astype(BF16), S)
    return _ffn(x1, row(p["ffn_norm_g"]), p["w_ffn_up"].astype(BF16), p["ffn_conv_w"].astype(F32),
                row(p["ffn_conv_b"]), p["w_ffn_down"].astype(BF16), S)


def kernel(x, mem, positions, mix_norm_g, mem_norm_g, w_in, moba_q_norm_g, moba_k_norm_g, dil_q_norm_g, dil_k_norm_g, mem_q_norm_g, mem_k_norm_g, w_mem_kv, w_branch_moba, w_branch_dil, w_branch_mem, w_out, ffn_norm_g, w_ffn_up, ffn_conv_w, ffn_conv_b, w_ffn_down):
    params = dict(mix_norm_g=mix_norm_g, mem_norm_g=mem_norm_g, w_in=w_in, moba_q_norm_g=moba_q_norm_g,
                  moba_k_norm_g=moba_k_norm_g, dil_q_norm_g=dil_q_norm_g, dil_k_norm_g=dil_k_norm_g,
                  mem_q_norm_g=mem_q_norm_g, mem_k_norm_g=mem_k_norm_g, w_mem_kv=w_mem_kv,
                  w_branch_moba=w_branch_moba, w_branch_dil=w_branch_dil, w_branch_mem=w_branch_mem,
                  w_out=w_out, ffn_norm_g=ffn_norm_g, w_ffn_up=w_ffn_up, ffn_conv_w=ffn_conv_w,
                  ffn_conv_b=ffn_conv_b, w_ffn_down=w_ffn_down)
    B = x.shape[0]
    depth = w_in.shape[0]
    outs = []
    for b in range(B):
        xb = x.reshape(x.shape[1:]) if B == 1 else x[b]
        for l in range(depth):
            xb = _layer(xb, mem[b], positions[b], {k: v[l] for k, v in params.items()})
        outs.append(xb)
    return outs[0].reshape(x.shape) if B == 1 else jnp.stack(outs, axis=0)
```

```python
import functools

import numpy as np
import jax
import jax.numpy as jnp
from jax import lax
from jax.experimental import pallas as pl
from jax.experimental.pallas import tpu as pltpu

D_MODEL = 1024
HEAD_DIM = 64
HALF = HEAD_DIM // 2
N_MOBA_HEADS = 6
N_DIL_HEADS = 6
N_MEM_HEADS = 4
N_MEM = 256
MOBA_BLOCK = 256
MOBA_TOPK = 3
DIL_PATTERNS = ((128, 1), (512, 4), (2048, 16))
DIL_BLOCK = 128
D_FF = 2816
CONV_WIDTH = 3
ROPE_THETA = 10000.0
EPS = 1e-6
MOBA_W = N_MOBA_HEADS * HEAD_DIM
DIL_W = N_DIL_HEADS * HEAD_DIM
MEM_W = N_MEM_HEADS * HEAD_DIM
QKV_COLS = 3 * MOBA_W + 3 * DIL_W + MEM_W
SCALE = HEAD_DIM ** -0.5

LANES = 128
MXU_TILE = 256
VMEM_LIMIT = 56 * 1024 * 1024

TM_ROPE = 2048
TM_PROJ = 512
TM_MERGE = 512
TM_FFN = 512
FFN_CHUNK = 256
DIL_ROWS = 1024

F32 = jnp.float32
BF16 = jnp.bfloat16
NEG_INF = float("-inf")


def _dot(a, b):
    return jnp.dot(a, b, preferred_element_type=F32)


def _dot_nt(a, b):
    return lax.dot_general(a, b, (((1,), (1,)), ((), ())), preferred_element_type=F32)


def _rms_rows(x, g):
    ms = jnp.mean(x * x, axis=-1, keepdims=True)
    return x * lax.rsqrt(ms + EPS) * g


def _params(*sem):
    return pltpu.CompilerParams(dimension_semantics=sem, vmem_limit_bytes=VMEM_LIMIT)


def _rope_kernel(pos_ref, inv_ref, cos_t_ref, sin_t_ref, cos_n_ref, sin_n_ref):
    pos = pos_ref[...].astype(F32)
    ang = inv_ref[...] * pos
    c = jnp.cos(ang)
    s = jnp.sin(ang)
    cos_t_ref[...] = jnp.concatenate([c, c], axis=0)
    sin_t_ref[...] = jnp.concatenate([-s, s], axis=0)
    cos_n_ref[...] = jnp.concatenate([c, c, c, c], axis=0).T
    sin_n_ref[...] = jnp.concatenate([-s, s, -s, s], axis=0).T


def _rope_tables(positions, S):
    tm = min(TM_ROPE, S)
    inv = (ROPE_THETA ** (-jnp.arange(HALF, dtype=F32) / HALF)).reshape(HALF, 1)
    return pl.pallas_call(
        _rope_kernel,
        grid=(S // tm,),
        in_specs=[pl.BlockSpec((1, tm), lambda i: (0, i)),
                  pl.BlockSpec((HALF, 1), lambda i: (0, 0))],
        out_specs=[pl.BlockSpec((HEAD_DIM, tm), lambda i: (0, i)),
                   pl.BlockSpec((HEAD_DIM, tm), lambda i: (0, i)),
                   pl.BlockSpec((tm, LANES), lambda i: (i, 0)),
                   pl.BlockSpec((tm, LANES), lambda i: (i, 0))],
        out_shape=[jax.ShapeDtypeStruct((HEAD_DIM, S), F32),
                   jax.ShapeDtypeStruct((HEAD_DIM, S), F32),
                   jax.ShapeDtypeStruct((S, LANES), F32),
                   jax.ShapeDtypeStruct((S, LANES), F32)],
        compiler_params=_params("parallel"),
        name="rope_tables",
    )(positions.reshape(1, S), inv)


def _memkv_kernel(mem_ref, g_ref, w_ref, gk_ref, gmat_ref, mk_ref, mv_ref):
    h = _rms_rows(mem_ref[...], g_ref[...]).astype(BF16)
    kv = _dot(h, w_ref[...])
    k = kv[:, :MEM_W]
    ms = _dot((k * k).astype(BF16), gmat_ref[...])
    mk_ref[...] = (k * lax.rsqrt(ms + EPS) * gk_ref[...]).astype(BF16)
    mv_ref[...] = kv[:, MEM_W:].astype(BF16)


def _mem_kv(mem, g, w_kv, gk_tiled, gmat):
    full = lambda shape: pl.BlockSpec(shape, lambda i: (0,) * len(shape))
    return pl.pallas_call(
        _memkv_kernel,
        grid=(1,),
        in_specs=[full((N_MEM, D_MODEL)), full((1, D_MODEL)), full((D_MODEL, 2 * MEM_W)),
                  full((1, MEM_W)), full((MXU_TILE, MXU_TILE))],
        out_specs=[full((N_MEM, MEM_W)), full((N_MEM, MEM_W))],
        out_shape=[jax.ShapeDtypeStruct((N_MEM, MEM_W), BF16)] * 2,
        compiler_params=_params("arbitrary"),
        name="mem_kv",
    )(mem, g, w_kv, gk_tiled, gmat)


_NAT_NORM = 3 * 384 + MEM_W
_NAT_NORM_PAD = 1536
_NAT_ROPE = 3 * 384
_NAT_COLS = _NAT_NORM + DIL_W


def _proj_kernel(x_ref, g_ref, wt_ref, wn_ref, gq_ref, gnat_ref, gmat_ref,
                 cos_t_ref, sin_t_ref, cos_n_ref, sin_n_ref,
                 qat_ref, vat_ref, ka_ref, km_ref, qm_ref, *rest):
    dil_refs, stage_ref = rest[:-1], rest[-1]
    tm = x_ref.shape[0]
    h = _rms_rows(x_ref[...], g_ref[...]).astype(BF16)

    yt = _dot_nt(wt_ref[...], h)
    cos_t = cos_t_ref[...]
    sin_t = sin_t_ref[...]
    gq = gq_ref[...]
    for hd in range(N_MOBA_HEADS):
        q = yt[hd * HEAD_DIM:(hd + 1) * HEAD_DIM]
        ms = jnp.mean(q * q, axis=0, keepdims=True)
        q = q * lax.rsqrt(ms + EPS) * gq
        rot = jnp.concatenate([q[HALF:], q[:HALF]], axis=0)
        q = (q * cos_t + rot * sin_t) * SCALE
        for b in range(tm // MOBA_BLOCK):
            qat_ref[b, hd * HEAD_DIM:(hd + 1) * HEAD_DIM, :] = (
                q[:, b * MOBA_BLOCK:(b + 1) * MOBA_BLOCK].astype(BF16))
    for b in range(tm // MOBA_BLOCK):
        vat_ref[b] = yt[MOBA_W:, b * MOBA_BLOCK:(b + 1) * MOBA_BLOCK].astype(BF16)

    gmat = gmat_ref[...]
    lane = lax.broadcasted_iota(jnp.int32, (tm, LANES), 1)
    first_half = (lane % HEAD_DIM) < HALF
    cos_n = cos_n_ref[...]
    sin_n = sin_n_ref[...]
    pairs = DIL_W // LANES

    def emit_dilated(ti, tiles):
        for lt in range(pairs):
            stage_ref[ti * pairs + lt] = tiles[lt]
        for pi, (_, d) in enumerate(DIL_PATTERNS):
            ref = dil_refs[3 * pi + ti]
            for r in range(d):
                for lt in range(pairs):
                    col = r * DIL_W + lt * LANES
                    ref[:, col:col + LANES] = stage_ref[
                        ti * pairs + lt, pl.ds(r, tm // d, stride=d), :].astype(BF16)

    chunks = []
    for c in range(_NAT_NORM_PAD // MXU_TILE):
        if c % 2 == 0:
            wide = _dot(h, wn_ref[:, c * MXU_TILE:(c + 2) * MXU_TILE])
        raw = wide[:, (c % 2) * MXU_TILE:(c % 2 + 1) * MXU_TILE]
        ms = _dot((raw * raw).astype(BF16), gmat)
        blk = raw * lax.rsqrt(ms + EPS) * gnat_ref[:, c * MXU_TILE:(c + 1) * MXU_TILE]
        for hc in range(MXU_TILE // LANES):
            col = c * MXU_TILE + hc * LANES
            if col >= _NAT_NORM:
                chunks.append(raw[:, hc * LANES:(hc + 1) * LANES])
                continue
            v = blk[:, hc * LANES:(hc + 1) * LANES]
            if col < _NAT_ROPE:
                rot = jnp.where(first_half, pltpu.roll(v, LANES - HALF, axis=1),
                                pltpu.roll(v, HALF, axis=1))
                v = v * cos_n + rot * sin_n
            chunks.append(v)
        if len(chunks) >= pairs and c == 1:
            ka = jnp.concatenate(chunks[0:pairs], axis=1)
            for b in range(tm // MOBA_BLOCK):
                km_ref[b] = jnp.mean(ka[b * MOBA_BLOCK:(b + 1) * MOBA_BLOCK], axis=0, keepdims=True)
            ka_ref[...] = ka.astype(BF16)
        if c == 2:
            emit_dilated(0, [t * SCALE for t in chunks[pairs:2 * pairs]])
        if c == 4:
            emit_dilated(1, chunks[2 * pairs:3 * pairs])
    qm_ref[...] = (jnp.concatenate(chunks[3 * pairs:3 * pairs + MEM_W // LANES], axis=1) * SCALE).astype(BF16)
    vd_rest = _dot(h, wn_ref[:, _NAT_NORM_PAD:])
    emit_dilated(2, [chunks[-1]] + [vd_rest[:, t * LANES:(t + 1) * LANES] for t in range(pairs - 1)])


def _proj(x, g_mix, wt, wn, gq_col, gnat, gmat, cos_t, sin_t, cos_n, sin_n, S):
    tm = TM_PROJ
    nb = S // MOBA_BLOCK
    bpt = tm // MOBA_BLOCK
    const = lambda shape: pl.BlockSpec(shape, lambda i: (0,) * len(shape))
    rows = lambda w: pl.BlockSpec((tm, w), lambda i: (i, 0))
    dils = [d for _, d in DIL_PATTERNS for _ in range(3)]
    outs = pl.pallas_call(
        _proj_kernel,
        grid=(S // tm,),
        in_specs=[rows(D_MODEL), const((1, D_MODEL)), const((2 * MOBA_W, D_MODEL)),
                  const((D_MODEL, _NAT_COLS)), const((HEAD_DIM, 1)), const((1, _NAT_NORM_PAD)),
                  const((MXU_TILE, MXU_TILE)),
                  pl.BlockSpec((HEAD_DIM, tm), lambda i: (0, i)),
                  pl.BlockSpec((HEAD_DIM, tm), lambda i: (0, i)),
                  rows(LANES), rows(LANES)],
        out_specs=[pl.BlockSpec((bpt, MOBA_W, MOBA_BLOCK), lambda i: (i, 0, 0)),
                   pl.BlockSpec((bpt, MOBA_W, MOBA_BLOCK), lambda i: (i, 0, 0)),
                   rows(MOBA_W),
                   pl.BlockSpec((bpt, 1, MOBA_W), lambda i: (i, 0, 0)),
                   rows(MEM_W)]
                  + [pl.BlockSpec((tm // d, d * DIL_W), lambda i: (i, 0)) for d in dils],
        out_shape=[jax.ShapeDtypeStruct((nb, MOBA_W, MOBA_BLOCK), BF16),
                   jax.ShapeDtypeStruct((nb, MOBA_W, MOBA_BLOCK), BF16),
                   jax.ShapeDtypeStruct((S, MOBA_W), BF16),
                   jax.ShapeDtypeStruct((nb, 1, MOBA_W), F32),
                   jax.ShapeDtypeStruct((S, MEM_W), BF16)]
                  + [jax.ShapeDtypeStruct((S // d, d * DIL_W), BF16) for d in dils],
        scratch_shapes=[pltpu.VMEM((3 * DIL_W // LANES, tm, LANES), F32)],
        compiler_params=_params("parallel"),
        name="in_proj",
    )(x, g_mix, wt, wn, gq_col, gnat, gmat, cos_t, sin_t, cos_n, sin_n)
    return outs[:5], [outs[5 + 3 * k:8 + 3 * k] for k in range(len(DIL_PATTERNS))]


_SHIFT_HEADROOM = 30.0
_FAST_BOUND = 50.0
_MOBA_UNROLL = 4


def _moba_kernel(bound_ref, qt_ref, k_ref, vt_ref, km_ref, o_ref, bias_ref, ot_ref, acc_ref):
    i = pl.program_id(0)
    tq = MOBA_BLOCK
    unroll = _MOBA_UNROLL
    heads = N_MOBA_HEADS
    nb = km_ref.shape[0]
    row = lax.broadcasted_iota(jnp.int32, (LANES, tq), 0)
    blk = lax.broadcasted_iota(jnp.int32, (nb, tq), 0)

    bound = bound_ref[0]
    fast = bound <= _FAST_BOUND
    sel_bias = jnp.where(fast, _SHIFT_HEADROOM - bound, 0.0)

    def pair_cols(hd):
        return slice((hd // 2) * LANES, (hd // 2 + 1) * LANES)

    qs = []
    for hd in range(heads):
        qt = qt_ref[0, pair_cols(hd), :]
        q_h = jnp.where((row // HEAD_DIM) == hd % 2, qt, jnp.zeros_like(qt))
        qs.append(q_h)
        km = km_ref[:, pair_cols(hd)]
        km1 = km.astype(BF16)
        r1 = km - km1.astype(F32)
        km2 = r1.astype(BF16)
        km3 = (r1 - km2.astype(F32)).astype(BF16)
        g3 = _dot(jnp.concatenate([km1, km2, km3], axis=0), q_h)
        gate = g3[:nb] + g3[nb:2 * nb] + g3[2 * nb:]
        gate = jnp.where(blk < i, gate, NEG_INF)
        bias = jnp.full((nb, tq), NEG_INF, F32)
        for _ in range(MOBA_TOPK):
            m = jnp.max(gate, axis=0, keepdims=True)
            idx = jnp.min(jnp.where(gate == m, blk, nb), axis=0, keepdims=True)
            hit = blk == idx
            bias = jnp.where(hit & (m > NEG_INF), sel_bias, bias)
            gate = jnp.where(hit, NEG_INF, gate)
        bias_ref[hd] = bias

    kpos = lax.broadcasted_iota(jnp.int32, (tq, tq), 0)
    qpos = lax.broadcasted_iota(jnp.int32, (tq, tq), 1)
    causal = kpos <= qpos

    def k_rows(j, hd):
        start = j * tq if isinstance(j, int) else pl.multiple_of(j * tq, tq)
        return k_ref[pl.ds(start, tq), pair_cols(hd)]

    def v_rows(j, hd):
        return vt_ref[j, hd * HEAD_DIM:(hd + 1) * HEAD_DIM, :]

    ones = jnp.ones((16, tq), BF16)

    def values(j, hd, p):
        return _dot(jnp.concatenate([v_rows(j, hd), ones], axis=0), p)

    own_s = [_dot(k_rows(i, hd), qs[hd]) for hd in range(heads)]
    own_p = [jnp.exp(jnp.where(causal, s + sel_bias, NEG_INF)).astype(BF16) for s in own_s]

    @pl.when(fast)
    def _():
        def attend(accs, blocks):
            out = list(accs)
            chains = [(j, hd) for j in blocks for hd in range(heads)]
            ss = [_dot(k_rows(j, hd), qs[hd]) for j, hd in chains]
            ps = [jnp.exp(s + bias_ref[hd, pl.ds(j, 1), :]).astype(BF16) for s, (j, hd) in zip(ss, chains)]
            for p, (j, hd) in zip(ps, chains):
                out[hd] = out[hd] + values(j, hd, p)
            return out

        rem = i % unroll
        for r in range(unroll):
            @pl.when(rem == r)
            def _():
                own = [values(i, hd, p) for hd, p in enumerate(own_p)]
                for hd, a in enumerate(attend(own, range(r))):
                    acc_ref[hd] = a

        def body(t, accs):
            return tuple(attend(accs, [rem + unroll * t + u for u in range(unroll)]))

        accs = lax.fori_loop(0, i // unroll, body, tuple(acc_ref[hd] for hd in range(heads)))
        for hd, a in enumerate(accs):
            ot_ref[hd * HEAD_DIM:(hd + 1) * HEAD_DIM, :] = a[:HEAD_DIM] / a[HEAD_DIM:HEAD_DIM + 1]

    @pl.when(jnp.logical_not(fast))
    def _():
        for pr in range(heads // 2):
            pair = (2 * pr, 2 * pr + 1)
            carry = []
            for hd in pair:
                s = jnp.where(causal, _dot(k_rows(i, hd), qs[hd]), NEG_INF)
                m = jnp.max(s, axis=0, keepdims=True)
                p = jnp.exp(s - m)
                l = jnp.sum(p, axis=0, keepdims=True)
                carry += [m, l, _dot(v_rows(i, hd), p.astype(BF16))]

            def body(j, carry):
                out = []
                for c, hd in enumerate(pair):
                    m, l, acc = carry[3 * c:3 * c + 3]
                    s = _dot(k_rows(j, hd), qs[hd]) + bias_ref[hd, pl.ds(j, 1), :]
                    m_new = jnp.maximum(m, jnp.max(s, axis=0, keepdims=True))
                    alpha = jnp.exp(m - m_new)
                    p = jnp.exp(s - m_new)
                    l = alpha * l + jnp.sum(p, axis=0, keepdims=True)
                    acc = alpha * acc + _dot(v_rows(j, hd), p.astype(BF16))
                    out += [m_new, l, acc]
                return tuple(out)

            carry = lax.fori_loop(0, i, body, tuple(carry))
            for c, hd in enumerate(pair):
                ot_ref[hd * HEAD_DIM:(hd + 1) * HEAD_DIM, :] = carry[3 * c + 2] / carry[3 * c + 1]

    o_ref[...] = ot_ref[...].T.astype(BF16)


def _moba(bound, qat, ka, vat, kmean, S):
    nb = S // MOBA_BLOCK
    once = lambda shape: pl.BlockSpec(shape, lambda i: (0,) * len(shape), pipeline_mode=pl.Buffered(1))
    return pl.pallas_call(
        _moba_kernel,
        grid=(nb,),
        in_specs=[pl.BlockSpec(memory_space=pltpu.SMEM),
                  pl.BlockSpec((1, MOBA_W, MOBA_BLOCK), lambda i: (i, 0, 0)),
                  once((S, MOBA_W)), once((nb, MOBA_W, MOBA_BLOCK)), once((nb, MOBA_W))],
        out_specs=pl.BlockSpec((MOBA_BLOCK, MOBA_W), lambda i: (i, 0)),
        out_shape=jax.ShapeDtypeStruct((S, MOBA_W), BF16),
        scratch_shapes=[pltpu.VMEM((N_MOBA_HEADS, nb, MOBA_BLOCK), F32),
                        pltpu.VMEM((MOBA_W, MOBA_BLOCK), F32),
                        pltpu.VMEM((N_MOBA_HEADS, HEAD_DIM + 16, MOBA_BLOCK), F32)],
        compiler_params=_params("arbitrary"),
        name="moba_attn",
    )(bound, qat, ka, vat, kmean)


def _dil_kernel(span, bound_ref, q_ref, kp_ref, kc_ref, vp_ref, vc_ref, o_ref, lse_ref):
    n = pl.program_id(1)
    blk = DIL_BLOCK
    nsub = q_ref.shape[0] // blk
    lane = lax.broadcasted_iota(jnp.int32, (blk, LANES), 1)
    lo_half = lane < HEAD_DIM
    qi = lax.broadcasted_iota(jnp.int32, (2 * blk, 2 * blk), 0) % blk + blk
    kj = lax.broadcasted_iota(jnp.int32, (2 * blk, 2 * blk), 1)
    dist = qi - kj
    band = jnp.where((dist >= 0) & (dist <= span), 0.0, NEG_INF)
    band_first = jnp.where(kj >= blk, band, NEG_INF)
    bound = bound_ref[0]
    fast = bound <= _FAST_BOUND
    shift = bound - _SHIFT_HEADROOM

    def blocks(fixed_shift):
        sub = shift if fixed_shift else 0.0
        mask_first = jnp.where(n > 0, band, band_first) - sub
        mask_rest = band - sub
        for pr in range(DIL_W // LANES):
            for b in range(nsub):
                block(fixed_shift, b, slice(pr * LANES, (pr + 1) * LANES), mask_first if b == 0 else mask_rest)

    def block(fixed_shift, b, cols, mask):
        q = q_ref[b * blk:(b + 1) * blk, cols]
        if b == 0:
            kk = jnp.concatenate([kp_ref[:, cols], kc_ref[0:blk, cols]], axis=0)
            vv = jnp.concatenate([vp_ref[:, cols], vc_ref[0:blk, cols]], axis=0)
        else:
            kk = kc_ref[(b - 1) * blk:(b + 1) * blk, cols]
            vv = vc_ref[(b - 1) * blk:(b + 1) * blk, cols]
        zero = jnp.zeros_like(q)
        q2 = jnp.concatenate([jnp.where(lo_half, q, zero), jnp.where(lo_half, zero, q)], axis=0)
        s = _dot_nt(q2, kk) + mask
        if fixed_shift:
            p = jnp.exp(s).astype(BF16)
            ov = _dot(p, jnp.concatenate([vv, jnp.ones_like(vv)], axis=1))
            den = ov[:, LANES:]
            o2 = ov[:, :LANES] / den
            lse2 = shift + jnp.log(den)
        else:
            m = jnp.max(s, axis=-1, keepdims=True)
            p = jnp.exp(s - m)
            den = jnp.sum(p, axis=-1, keepdims=True)
            o2 = _dot(p.astype(BF16), vv) / den
            lse2 = jnp.broadcast_to(m + jnp.log(den), (2 * blk, LANES))
        o_ref[b * blk:(b + 1) * blk, cols] = jnp.where(lo_half, o2[:blk], o2[blk:]).astype(BF16)
        lse_ref[b * blk:(b + 1) * blk, cols] = jnp.where(lo_half, lse2[:blk], lse2[blk:])

    pl.when(fast)(lambda: blocks(True))
    pl.when(jnp.logical_not(fast))(lambda: blocks(False))


def _dilated(bound, qd, kd, vd, window, dil, S):
    span = window // dil
    L = S // dil
    rows = min(DIL_ROWS, L)
    sub = rows // DIL_BLOCK
    cur = pl.BlockSpec((rows, DIL_W), lambda r, n: (n, r))
    prev = pl.BlockSpec((DIL_BLOCK, DIL_W), lambda r, n: (jnp.maximum(n * sub - 1, 0), r))
    return pl.pallas_call(
        functools.partial(_dil_kernel, span),
        grid=(dil, L // rows),
        in_specs=[pl.BlockSpec(memory_space=pltpu.SMEM), cur, prev, cur, prev, cur],
        out_specs=[cur, cur],
        out_shape=[jax.ShapeDtypeStruct((L, dil * DIL_W), BF16),
                   jax.ShapeDtypeStruct((L, dil * DIL_W), F32)],
        compiler_params=_params("parallel", "arbitrary"),
        name=f"dilated_attn_d{dil}",
    )(bound, qd, kd, kd, vd, vd)


def _merge_kernel(x_ref, g_ref, wg_ref, oa_ref, od1_ref, od2_ref, od3_ref, l1_ref, l2_ref, l3_ref,
                  qm_ref, mk_ref, mv_ref, wba_ref, wbd_ref, wbm_ref, wo_ref, out_ref, nat_ref):
    x = x_ref[...]
    tm = x.shape[0]
    h = _rms_rows(x, g_ref[...]).astype(BF16)

    def natural(ref, dil, slot):
        if dil == 1:
            return ref[...].astype(F32)
        tiles = DIL_W // LANES
        for r in range(dil):
            for lt in range(tiles):
                col = r * DIL_W + lt * LANES
                nat_ref[slot * tiles + lt, pl.ds(r, tm // dil, stride=dil), :] = (
                    ref[:, col:col + LANES].astype(F32))
        return jnp.concatenate([nat_ref[slot * tiles + lt] for lt in range(tiles)], axis=1)

    lane = lax.broadcasted_iota(jnp.int32, (tm, LANES), 1)
    lo_half = lane < HEAD_DIM
    mem_pairs = MEM_W // LANES
    scores = []
    for pr in range(mem_pairs):
        q = qm_ref[:, pr * LANES:(pr + 1) * LANES]
        zero = jnp.zeros_like(q)
        q2 = jnp.concatenate([jnp.where(lo_half, q, zero), jnp.where(lo_half, zero, q)], axis=0)
        scores.append(_dot_nt(q2, mk_ref[:, pr * LANES:(pr + 1) * LANES]))
    graw = [_dot(h, wg_ref[:, bi * D_MODEL:(bi + 1) * D_MODEL]) for bi in range(3)]

    dils = [d for _, d in DIL_PATTERNS]
    l1, l2, l3 = [natural(ref, d, k) for k, (ref, d) in enumerate(zip((l1_ref, l2_ref, l3_ref), dils))]
    o1, o2, o3 = [natural(ref, d, 3 + k) for k, (ref, d) in enumerate(zip((od1_ref, od2_ref, od3_ref), dils))]
    lmax = jnp.maximum(jnp.maximum(l1, l2), l3)
    e1, e2, e3 = jnp.exp(l1 - lmax), jnp.exp(l2 - lmax), jnp.exp(l3 - lmax)
    o_d = (e1 * o1 + e2 * o2 + e3 * o3) / (e1 + e2 + e3)

    probs = []
    for s in scores:
        p = jnp.exp(s - jnp.max(s, axis=-1, keepdims=True))
        probs.append((p / jnp.sum(p, axis=-1, keepdims=True)).astype(BF16))
    o_m = []
    for pr in range(mem_pairs):
        o2 = _dot(probs[pr], mv_ref[:, pr * LANES:(pr + 1) * LANES])
        o_m.append(jnp.where(lo_half, o2[:tm], o2[tm:]))
    o_m = jnp.concatenate(o_m, axis=1)

    merged = None
    branches = ((oa_ref[...], wba_ref), (o_d.astype(BF16), wbd_ref), (o_m.astype(BF16), wbm_ref))
    for bi, (o_b, w_ref) in enumerate(branches):
        term = jax.nn.sigmoid(graw[bi]) * _dot(o_b, w_ref[...])
        merged = term if merged is None else merged + term
    out_ref[...] = x + _dot(merged.astype(BF16), wo_ref[...])


def _merge(x, g_mix, w_gate, o_a, o_ds, lses, qm, mk, mv, w_ba, w_bd, w_bm, w_out, S):
    tm = TM_MERGE
    const = lambda shape: pl.BlockSpec(shape, lambda i: (0,) * len(shape), pipeline_mode=pl.Buffered(1))
    rows = lambda w: pl.BlockSpec((tm, w), lambda i: (i, 0))
    dil_rows = [pl.BlockSpec((tm // d, d * DIL_W), lambda i: (i, 0)) for _, d in DIL_PATTERNS]
    return pl.pallas_call(
        _merge_kernel,
        grid=(S // tm,),
        in_specs=[rows(D_MODEL), const((1, D_MODEL)), const((D_MODEL, 3 * D_MODEL)),
                  rows(MOBA_W), *dil_rows, *dil_rows,
                  rows(MEM_W), const((N_MEM, MEM_W)), const((N_MEM, MEM_W)),
                  const((MOBA_W, D_MODEL)), const((DIL_W, D_MODEL)), const((MEM_W, D_MODEL)),
                  const((D_MODEL, D_MODEL))],
        out_specs=rows(D_MODEL),
        out_shape=jax.ShapeDtypeStruct((S, D_MODEL), F32),
        scratch_shapes=[pltpu.VMEM((2 * len(DIL_PATTERNS) * DIL_W // LANES, tm, LANES), F32)],
        compiler_params=_params("parallel"),
        name="gated_merge",
    )(x, g_mix, w_gate, o_a, *o_ds, *lses, qm, mk, mv, w_ba, w_bd, w_bm, w_out)


def _ffn_kernel(x_ref, g_ref, wup_ref, cw_ref, cb_ref, wdn_ref, out_ref, halo_ref, act_ref):
    i = pl.program_id(0)
    tm = x_ref.shape[0]

    @pl.when(i == 0)
    def _():
        halo_ref[...] = jnp.zeros_like(halo_ref)

    x = x_ref[...]
    h = _rms_rows(x, g_ref[...]).astype(BF16)

    row8 = lax.broadcasted_iota(jnp.int32, (8, FFN_CHUNK), 0)

    def conv(u, col):
        prev = halo_ref[:, col:col + FFN_CHUNK]
        halo_ref[:, col:col + FFN_CHUNK] = u[tm - 8:, :]
        p6 = jnp.broadcast_to(prev[6:7], (8, FFN_CHUNK))
        p7 = jnp.broadcast_to(prev[7:8], (8, FFN_CHUNK))
        r1 = pltpu.roll(u, 1, axis=0)
        r2 = pltpu.roll(u, 2, axis=0)
        top1 = jnp.where(row8 == 0, p7, r1[:8])
        top2 = jnp.where(row8 == 0, p6, jnp.where(row8 == 1, p7, r2[:8]))
        u1 = jnp.concatenate([top1, r1[8:]], axis=0)
        u2 = jnp.concatenate([top2, r2[8:]], axis=0)
        w = cw_ref[:, col:col + FFN_CHUNK]
        return cb_ref[:, col:col + FFN_CHUNK] + w[0:1] * u2 + w[1:2] * u1 + w[2:3] * u

    for c in range(D_FF // FFN_CHUNK):
        cg = c * FFN_CHUNK
        cv = D_FF + c * FFN_CHUNK
        u_g = conv(_dot(h, wup_ref[:, cg:cg + FFN_CHUNK]), cg)
        u_v = conv(_dot(h, wup_ref[:, cv:cv + FFN_CHUNK]), cv)
        act_ref[:, cg:cg + FFN_CHUNK] = (jax.nn.silu(u_g) * u_v).astype(BF16)
    out_ref[...] = x + _dot(act_ref[...], wdn_ref[...])


def _ffn(x, g_ffn, w_up, conv_w, conv_b, w_down, S):
    tm = TM_FFN
    const = lambda shape: pl.BlockSpec(shape, lambda i: (0,) * len(shape), pipeline_mode=pl.Buffered(1))
    rows = pl.BlockSpec((tm, D_MODEL), lambda i: (i, 0))
    return pl.pallas_call(
        _ffn_kernel,
        grid=(S // tm,),
        in_specs=[rows, const((1, D_MODEL)), const((D_MODEL, 2 * D_FF)),
                  const((CONV_WIDTH, 2 * D_FF)), const((1, 2 * D_FF)), const((D_FF, D_MODEL))],
        out_specs=rows,
        out_shape=jax.ShapeDtypeStruct((S, D_MODEL), F32),
        scratch_shapes=[pltpu.VMEM((8, 2 * D_FF), F32), pltpu.VMEM((tm, D_FF), BF16)],
        compiler_params=_params("arbitrary"),
        name="conv_ffn",
    )(x, g_ffn, w_up, conv_w, conv_b, w_down)


def _group_mean_matrix():
    g = np.arange(MXU_TILE) // HEAD_DIM
    return jnp.asarray((g[:, None] == g[None, :]).astype(np.float32) / HEAD_DIM, dtype=BF16)


def _layer(x, mem, positions, p):
    S = x.shape[0]
    assert S % (max(d for _, d in DIL_PATTERNS) * DIL_BLOCK) == 0 and S % TM_PROJ == 0
    row = lambda v: v.reshape(1, -1).astype(F32)
    c = np.cumsum([0, MOBA_W, MOBA_W, MOBA_W, DIL_W, DIL_W, DIL_W, MEM_W])
    w_in = p["w_in"]
    seg = lambda k: w_in[:, c[k]:c[k + 1]]
    wt = jnp.concatenate([seg(0), seg(2)], axis=1).T.astype(BF16)
    wn = jnp.concatenate([seg(1), seg(3), seg(4), seg(6), seg(5)], axis=1).astype(BF16)
    w_gate = w_in[:, QKV_COLS:].astype(BF16)
    gnat = jnp.concatenate([jnp.tile(p["moba_k_norm_g"], N_MOBA_HEADS), jnp.tile(p["dil_q_norm_g"], N_DIL_HEADS),
                            jnp.tile(p["dil_k_norm_g"], N_DIL_HEADS), jnp.tile(p["mem_q_norm_g"], N_MEM_HEADS),
                            jnp.ones((_NAT_NORM_PAD - _NAT_NORM,), F32)]).reshape(1, _NAT_NORM_PAD)
    gmat = _group_mean_matrix()

    cos_t, sin_t, cos_n, sin_n = _rope_tables(positions, S)
    mk, mv = _mem_kv(mem, row(p["mem_norm_g"]), p["w_mem_kv"].astype(BF16),
                     row(jnp.tile(p["mem_k_norm_g"], N_MEM_HEADS)), gmat)
    (qat, vat, ka, kmean, qm), dil_qkv = _proj(
        x, row(p["mix_norm_g"]), wt, wn, p["moba_q_norm_g"].reshape(HEAD_DIM, 1).astype(F32), gnat, gmat,
        cos_t, sin_t, cos_n, sin_n, S)
    def score_bound(gq, gk):
        return (1.02 * HEAD_DIM * SCALE * jnp.max(jnp.abs(gq)) * jnp.max(jnp.abs(gk))).reshape(1).astype(F32)

    o_a = _moba(score_bound(p["moba_q_norm_g"], p["moba_k_norm_g"]), qat, ka, vat,
                kmean.reshape(S // MOBA_BLOCK, MOBA_W), S)
    bound_d = score_bound(p["dil_q_norm_g"], p["dil_k_norm_g"])
    o_ds, lses = zip(*[_dilated(bound_d, *qkv, w, d, S) for qkv, (w, d) in zip(dil_qkv, DIL_PATTERNS)])
    x1 = _merge(x, row(p["mix_norm_g"]), w_gate, o_a, o_ds, lses, qm, mk, mv,
                p["w_branch_moba"].astype(BF16), p["w_branch_dil"].astype(BF16),
                p["w_branch_mem"].astype(BF16), p["w_out"].astype(BF16), S)
    return _ffn(x1, row(p["ffn_norm_g"]), p["w_ffn_up"].astype(BF16), p["ffn_conv_w"].astype(F32),
                row(p["ffn_conv_b"]), p["w_ffn_down"].astype(BF16), S)


def kernel(x, mem, positions, mix_norm_g, mem_norm_g, w_in, moba_q_norm_g, moba_k_norm_g, dil_q_norm_g, dil_k_norm_g, mem_q_norm_g, mem_k_norm_g, w_mem_kv, w_branch_moba, w_branch_dil, w_branch_mem, w_out, ffn_norm_g, w_ffn_up, ffn_conv_w, ffn_conv_b, w_ffn_down):
    params = dict(mix_norm_g=mix_norm_g, mem_norm_g=mem_norm_g, w_in=w_in, moba_q_norm_g=moba_q_norm_g,
                  moba_k_norm_g=moba_k_norm_g, dil_q_norm_g=dil_q_norm_g, dil_k_norm_g=dil_k_norm_g,
                  mem_q_norm_g=mem_q_norm_g, mem_k_norm_g=mem_k_norm_g, w_mem_kv=w_mem_kv,
                  w_branch_moba=w_branch_moba, w_branch_dil=w_branch_dil, w_branch_mem=w_branch_mem,
                  w_out=w_out, ffn_norm_g=ffn_norm_g, w_ffn_up=w_ffn_up, ffn_conv_w=ffn_conv_w,
                  ffn_conv_b=ffn_conv_b, w_ffn_down=w_ffn_down)
    B = x.shape[0]
    depth = w_in.shape[0]
    outs = []
    for b in range(B):
        xb = x.reshape(x.shape[1:]) if B == 1 else x[b]
        for l in range(depth):
            xb = _layer(xb, mem[b], positions[b], {k: v[l] for k, v in params.items()})
        outs.append(xb)
    return outs[0].reshape(x.shape) if B == 1 else jnp.stack(outs, axis=0)
```

```python
import functools

import numpy as np
import jax
import jax.numpy as jnp
from jax import lax
from jax.experimental import pallas as pl
from jax.experimental.pallas import tpu as pltpu

D_MODEL = 1024
HEAD_DIM = 64
HALF = HEAD_DIM // 2
N_MOBA_HEADS = 6
N_DIL_HEADS = 6
N_MEM_HEADS = 4
N_MEM = 256
MOBA_BLOCK = 256
MOBA_TOPK = 3
DIL_PATTERNS = ((128, 1), (512, 4), (2048, 16))
DIL_BLOCK = 128
D_FF = 2816
CONV_WIDTH = 3
ROPE_THETA = 10000.0
EPS = 1e-6
MOBA_W = N_MOBA_HEADS * HEAD_DIM
DIL_W = N_DIL_HEADS * HEAD_DIM
MEM_W = N_MEM_HEADS * HEAD_DIM
QKV_COLS = 3 * MOBA_W + 3 * DIL_W + MEM_W
SCALE = HEAD_DIM ** -0.5

LANES = 128
MXU_TILE = 256
VMEM_LIMIT = 56 * 1024 * 1024

TM_ROPE = 2048
TM_PROJ = 512
TM_MERGE = 512
TM_FFN = 512
FFN_CHUNK = 256
DIL_ROWS = 1024

F32 = jnp.float32
BF16 = jnp.bfloat16
NEG_INF = float("-inf")


def _dot(a, b):
    return jnp.dot(a, b, preferred_element_type=F32)


def _dot_nt(a, b):
    return lax.dot_general(a, b, (((1,), (1,)), ((), ())), preferred_element_type=F32)


def _rms_rows(x, g):
    ms = jnp.mean(x * x, axis=-1, keepdims=True)
    return x * lax.rsqrt(ms + EPS) * g


def _params(*sem):
    return pltpu.CompilerParams(dimension_semantics=sem, vmem_limit_bytes=VMEM_LIMIT)


def _rope_kernel(pos_ref, inv_ref, cos_t_ref, sin_t_ref, cos_n_ref, sin_n_ref):
    pos = pos_ref[...].astype(F32)
    ang = inv_ref[...] * pos
    c = jnp.cos(ang)
    s = jnp.sin(ang)
    cos_t_ref[...] = jnp.concatenate([c, c], axis=0)
    sin_t_ref[...] = jnp.concatenate([-s, s], axis=0)
    cos_n_ref[...] = jnp.concatenate([c, c, c, c], axis=0).T
    sin_n_ref[...] = jnp.concatenate([-s, s, -s, s], axis=0).T


def _rope_tables(positions, S):
    tm = min(TM_ROPE, S)
    inv = (ROPE_THETA ** (-jnp.arange(HALF, dtype=F32) / HALF)).reshape(HALF, 1)
    return pl.pallas_call(
        _rope_kernel,
        grid=(S // tm,),
        in_specs=[pl.BlockSpec((1, tm), lambda i: (0, i)),
                  pl.BlockSpec((HALF, 1), lambda i: (0, 0))],
        out_specs=[pl.BlockSpec((HEAD_DIM, tm), lambda i: (0, i)),
                   pl.BlockSpec((HEAD_DIM, tm), lambda i: (0, i)),
                   pl.BlockSpec((tm, LANES), lambda i: (i, 0)),
                   pl.BlockSpec((tm, LANES), lambda i: (i, 0))],
        out_shape=[jax.ShapeDtypeStruct((HEAD_DIM, S), F32),
                   jax.ShapeDtypeStruct((HEAD_DIM, S), F32),
                   jax.ShapeDtypeStruct((S, LANES), F32),
                   jax.ShapeDtypeStruct((S, LANES), F32)],
        compiler_params=_params("parallel"),
        name="rope_tables",
    )(positions.reshape(1, S), inv)


def _memkv_kernel(mem_ref, g_ref, w_ref, gk_ref, gmat_ref, mk_ref, mv_ref):
    h = _rms_rows(mem_ref[...], g_ref[...]).astype(BF16)
    kv = _dot(h, w_ref[...])
    k = kv[:, :MEM_W]
    ms = _dot((k * k).astype(BF16), gmat_ref[...])
    mk_ref[...] = (k * lax.rsqrt(ms + EPS) * gk_ref[...]).astype(BF16)
    mv_ref[...] = kv[:, MEM_W:].astype(BF16)


def _mem_kv(mem, g, w_kv, gk_tiled, gmat):
    full = lambda shape: pl.BlockSpec(shape, lambda i: (0,) * len(shape))
    return pl.pallas_call(
        _memkv_kernel,
        grid=(1,),
        in_specs=[full((N_MEM, D_MODEL)), full((1, D_MODEL)), full((D_MODEL, 2 * MEM_W)),
                  full((1, MEM_W)), full((MXU_TILE, MXU_TILE))],
        out_specs=[full((N_MEM, MEM_W)), full((N_MEM, MEM_W))],
        out_shape=[jax.ShapeDtypeStruct((N_MEM, MEM_W), BF16)] * 2,
        compiler_params=_params("arbitrary"),
        name="mem_kv",
    )(mem, g, w_kv, gk_tiled, gmat)


_NAT_NORM = 3 * 384 + MEM_W
_NAT_NORM_PAD = 1536
_NAT_ROPE = 3 * 384
_NAT_COLS = _NAT_NORM + DIL_W


def _proj_kernel(x_ref, g_ref, wt_ref, wn_ref, gq_ref, gnat_ref, gmat_ref,
                 cos_t_ref, sin_t_ref, cos_n_ref, sin_n_ref,
                 qat_ref, vat_ref, ka_ref, km_ref, qm_ref, *rest):
    dil_refs, stage_ref = rest[:-1], rest[-1]
    tm = x_ref.shape[0]
    h = _rms_rows(x_ref[...], g_ref[...]).astype(BF16)

    yt = _dot_nt(wt_ref[...], h)
    cos_t = cos_t_ref[...]
    sin_t = sin_t_ref[...]
    gq = gq_ref[...]
    for hd in range(N_MOBA_HEADS):
        q = yt[hd * HEAD_DIM:(hd + 1) * HEAD_DIM]
        ms = jnp.mean(q * q, axis=0, keepdims=True)
        q = q * lax.rsqrt(ms + EPS) * gq
        rot = jnp.concatenate([q[HALF:], q[:HALF]], axis=0)
        q = (q * cos_t + rot * sin_t) * SCALE
        for b in range(tm // MOBA_BLOCK):
            qat_ref[b, hd * HEAD_DIM:(hd + 1) * HEAD_DIM, :] = (
                q[:, b * MOBA_BLOCK:(b + 1) * MOBA_BLOCK].astype(BF16))
    for b in range(tm // MOBA_BLOCK):
        vat_ref[b] = yt[MOBA_W:, b * MOBA_BLOCK:(b + 1) * MOBA_BLOCK].astype(BF16)

    gmat = gmat_ref[...]
    lane = lax.broadcasted_iota(jnp.int32, (tm, LANES), 1)
    first_half = (lane % HEAD_DIM) < HALF
    cos_n = cos_n_ref[...]
    sin_n = sin_n_ref[...]
    pairs = DIL_W // LANES

    def emit_dilated(ti, tiles):
        for lt in range(pairs):
            stage_ref[ti * pairs + lt] = tiles[lt]
        for pi, (_, d) in enumerate(DIL_PATTERNS):
            ref = dil_refs[3 * pi + ti]
            for r in range(d):
                for lt in range(pairs):
                    col = r * DIL_W + lt * LANES
                    ref[:, col:col + LANES] = stage_ref[
                        ti * pairs + lt, pl.ds(r, tm // d, stride=d), :].astype(BF16)

    chunks = []
    for c in range(_NAT_NORM_PAD // MXU_TILE):
        if c % 2 == 0:
            wide = _dot(h, wn_ref[:, c * MXU_TILE:(c + 2) * MXU_TILE])
        raw = wide[:, (c % 2) * MXU_TILE:(c % 2 + 1) * MXU_TILE]
        ms = _dot((raw * raw).astype(BF16), gmat)
        blk = raw * lax.rsqrt(ms + EPS) * gnat_ref[:, c * MXU_TILE:(c + 1) * MXU_TILE]
        for hc in range(MXU_TILE // LANES):
            col = c * MXU_TILE + hc * LANES
            if col >= _NAT_NORM:
                chunks.append(raw[:, hc * LANES:(hc + 1) * LANES])
                continue
            v = blk[:, hc * LANES:(hc + 1) * LANES]
            if col < _NAT_ROPE:
                rot = jnp.where(first_half, pltpu.roll(v, LANES - HALF, axis=1),
                                pltpu.roll(v, HALF, axis=1))
                v = v * cos_n + rot * sin_n
            chunks.append(v)
        if len(chunks) >= pairs and c == 1:
            ka = jnp.concatenate(chunks[0:pairs], axis=1)
            for b in range(tm // MOBA_BLOCK):
                km_ref[b] = jnp.mean(ka[b * MOBA_BLOCK:(b + 1) * MOBA_BLOCK], axis=0, keepdims=True)
            ka_ref[...] = ka.astype(BF16)
        if c == 2:
            emit_dilated(0, [t * SCALE for t in chunks[pairs:2 * pairs]])
        if c == 4:
            emit_dilated(1, chunks[2 * pairs:3 * pairs])
    qm_ref[...] = (jnp.concatenate(chunks[3 * pairs:3 * pairs + MEM_W // LANES], axis=1) * SCALE).astype(BF16)
    vd_rest = _dot(h, wn_ref[:, _NAT_NORM_PAD:])
    emit_dilated(2, [chunks[-1]] + [vd_rest[:, t * LANES:(t + 1) * LANES] for t in range(pairs - 1)])


def _proj(x, g_mix, wt, wn, gq_col, gnat, gmat, cos_t, sin_t, cos_n, sin_n, S):
    tm = TM_PROJ
    nb = S // MOBA_BLOCK
    bpt = tm // MOBA_BLOCK
    const = lambda shape: pl.BlockSpec(shape, lambda i: (0,) * len(shape))
    rows = lambda w: pl.BlockSpec((tm, w), lambda i: (i, 0))
    dils = [d for _, d in DIL_PATTERNS for _ in range(3)]
    outs = pl.pallas_call(
        _proj_kernel,
        grid=(S // tm,),
        in_specs=[rows(D_MODEL), const((1, D_MODEL)), const((2 * MOBA_W, D_MODEL)),
                  const((D_MODEL, _NAT_COLS)), const((HEAD_DIM, 1)), const((1, _NAT_NORM_PAD)),
                  const((MXU_TILE, MXU_TILE)),
                  pl.BlockSpec((HEAD_DIM, tm), lambda i: (0, i)),
                  pl.BlockSpec((HEAD_DIM, tm), lambda i: (0, i)),
                  rows(LANES), rows(LANES)],
        out_specs=[pl.BlockSpec((bpt, MOBA_W, MOBA_BLOCK), lambda i: (i, 0, 0)),
                   pl.BlockSpec((bpt, MOBA_W, MOBA_BLOCK), lambda i: (i, 0, 0)),
                   rows(MOBA_W),
                   pl.BlockSpec((bpt, 1, MOBA_W), lambda i: (i, 0, 0)),
                   rows(MEM_W)]
                  + [pl.BlockSpec((tm // d, d * DIL_W), lambda i: (i, 0)) for d in dils],
        out_shape=[jax.ShapeDtypeStruct((nb, MOBA_W, MOBA_BLOCK), BF16),
                   jax.ShapeDtypeStruct((nb, MOBA_W, MOBA_BLOCK), BF16),
                   jax.ShapeDtypeStruct((S, MOBA_W), BF16),
                   jax.ShapeDtypeStruct((nb, 1, MOBA_W), F32),
                   jax.ShapeDtypeStruct((S, MEM_W), BF16)]
                  + [jax.ShapeDtypeStruct((S // d, d * DIL_W), BF16) for d in dils],
        scratch_shapes=[pltpu.VMEM((3 * DIL_W // LANES, tm, LANES), F32)],
        compiler_params=_params("parallel"),
        name="in_proj",
    )(x, g_mix, wt, wn, gq_col, gnat, gmat, cos_t, sin_t, cos_n, sin_n)
    return outs[:5], [outs[5 + 3 * k:8 + 3 * k] for k in range(len(DIL_PATTERNS))]


_SHIFT_HEADROOM = 30.0
_FAST_BOUND = 50.0
_MOBA_UNROLL = 4


def _moba_kernel(bound_ref, qt_ref, k_ref, vt_ref, km_ref, o_ref, bias_ref, ot_ref, acc_ref):
    i = pl.program_id(0)
    tq = MOBA_BLOCK
    unroll = _MOBA_UNROLL
    heads = N_MOBA_HEADS
    nb = km_ref.shape[0]
    row = lax.broadcasted_iota(jnp.int32, (LANES, tq), 0)
    blk = lax.broadcasted_iota(jnp.int32, (nb, tq), 0)

    bound = bound_ref[0]
    fast = bound <= _FAST_BOUND
    sel_bias = jnp.where(fast, _SHIFT_HEADROOM - bound, 0.0)

    def pair_cols(hd):
        return slice((hd // 2) * LANES, (hd // 2 + 1) * LANES)

    qs = []
    for hd in range(heads):
        qt = qt_ref[0, pair_cols(hd), :]
        q_h = jnp.where((row // HEAD_DIM) == hd % 2, qt, jnp.zeros_like(qt))
        qs.append(q_h)
        km = km_ref[:, pair_cols(hd)]
        km1 = km.astype(BF16)
        r1 = km - km1.astype(F32)
        km2 = r1.astype(BF16)
        km3 = (r1 - km2.astype(F32)).astype(BF16)
        g3 = _dot(jnp.concatenate([km1, km2, km3], axis=0), q_h)
        gate = g3[:nb] + g3[nb:2 * nb] + g3[2 * nb:]
        gate = jnp.where(blk < i, gate, NEG_INF)
        bias = jnp.full((nb, tq), NEG_INF, F32)
        for _ in range(MOBA_TOPK):
            m = jnp.max(gate, axis=0, keepdims=True)
            idx = jnp.min(jnp.where(gate == m, blk, nb), axis=0, keepdims=True)
            hit = blk == idx
            bias = jnp.where(hit & (m > NEG_INF), sel_bias, bias)
            gate = jnp.where(hit, NEG_INF, gate)
        bias_ref[hd] = bias

    kpos = lax.broadcasted_iota(jnp.int32, (tq, tq), 0)
    qpos = lax.broadcasted_iota(jnp.int32, (tq, tq), 1)
    causal = kpos <= qpos

    def k_rows(j, hd):
        start = j * tq if isinstance(j, int) else pl.multiple_of(j * tq, tq)
        return k_ref[pl.ds(start, tq), pair_cols(hd)]

    def v_rows(j, hd):
        return vt_ref[j, hd * HEAD_DIM:(hd + 1) * HEAD_DIM, :]

    ones = jnp.ones((16, tq), BF16)

    def values(j, hd, p):
        return _dot(jnp.concatenate([v_rows(j, hd), ones], axis=0), p)

    own_s = [_dot(k_rows(i, hd), qs[hd]) for hd in range(heads)]
    own_p = [jnp.exp(jnp.where(causal, s + sel_bias, NEG_INF)).astype(BF16) for s in own_s]

    @pl.when(fast)
    def _():
        def attend(accs, blocks):
            out = list(accs)
            chains = [(j, hd) for j in blocks for hd in range(heads)]
            ss = [_dot(k_rows(j, hd), qs[hd]) for j, hd in chains]
            ps = [jnp.exp(s + bias_ref[hd, pl.ds(j, 1), :]).astype(BF16) for s, (j, hd) in zip(ss, chains)]
            for p, (j, hd) in zip(ps, chains):
                out[hd] = out[hd] + values(j, hd, p)
            return out

        rem = i % unroll
        for r in range(unroll):
            @pl.when(rem == r)
            def _():
                own = [values(i, hd, p) for hd, p in enumerate(own_p)]
                for hd, a in enumerate(attend(own, range(r))):
                    acc_ref[hd] = a

        groups = i // unroll
        single = groups % 2

        def body1(t, accs):
            return tuple(attend(accs, [rem + u for u in range(unroll)]))

        def body2(t, accs):
            start = rem + unroll * single + 2 * unroll * t
            return tuple(attend(accs, [start + u for u in range(2 * unroll)]))

        accs = lax.fori_loop(0, single, body1, tuple(acc_ref[hd] for hd in range(heads)))
        accs = lax.fori_loop(0, groups // 2, body2, accs)
        for hd, a in enumerate(accs):
            ot_ref[hd * HEAD_DIM:(hd + 1) * HEAD_DIM, :] = a[:HEAD_DIM] / a[HEAD_DIM:HEAD_DIM + 1]

    @pl.when(jnp.logical_not(fast))
    def _():
        for pr in range(heads // 2):
            pair = (2 * pr, 2 * pr + 1)
            carry = []
            for hd in pair:
                s = jnp.where(causal, _dot(k_rows(i, hd), qs[hd]), NEG_INF)
                m = jnp.max(s, axis=0, keepdims=True)
                p = jnp.exp(s - m)
                l = jnp.sum(p, axis=0, keepdims=True)
                carry += [m, l, _dot(v_rows(i, hd), p.astype(BF16))]

            def body(j, carry):
                out = []
                for c, hd in enumerate(pair):
                    m, l, acc = carry[3 * c:3 * c + 3]
                    s = _dot(k_rows(j, hd), qs[hd]) + bias_ref[hd, pl.ds(j, 1), :]
                    m_new = jnp.maximum(m, jnp.max(s, axis=0, keepdims=True))
                    alpha = jnp.exp(m - m_new)
                    p = jnp.exp(s - m_new)
                    l = alpha * l + jnp.sum(p, axis=0, keepdims=True)
                    acc = alpha * acc + _dot(v_rows(j, hd), p.astype(BF16))
                    out += [m_new, l, acc]
                return tuple(out)

            carry = lax.fori_loop(0, i, body, tuple(carry))
            for c, hd in enumerate(pair):
                ot_ref[hd * HEAD_DIM:(hd + 1) * HEAD_DIM, :] = carry[3 * c + 2] / carry[3 * c + 1]

    o_ref[...] = ot_ref[...].T.astype(BF16)


def _moba(bound, qat, ka, vat, kmean, S):
    nb = S // MOBA_BLOCK
    once = lambda shape: pl.BlockSpec(shape, lambda i: (0,) * len(shape), pipeline_mode=pl.Buffered(1))
    return pl.pallas_call(
        _moba_kernel,
        grid=(nb,),
        in_specs=[pl.BlockSpec(memory_space=pltpu.SMEM),
                  pl.BlockSpec((1, MOBA_W, MOBA_BLOCK), lambda i: (i, 0, 0)),
                  once((S, MOBA_W)), once((nb, MOBA_W, MOBA_BLOCK)), once((nb, MOBA_W))],
        out_specs=pl.BlockSpec((MOBA_BLOCK, MOBA_W), lambda i: (i, 0)),
        out_shape=jax.ShapeDtypeStruct((S, MOBA_W), BF16),
        scratch_shapes=[pltpu.VMEM((N_MOBA_HEADS, nb, MOBA_BLOCK), F32),
                        pltpu.VMEM((MOBA_W, MOBA_BLOCK), F32),
                        pltpu.VMEM((N_MOBA_HEADS, HEAD_DIM + 16, MOBA_BLOCK), F32)],
        compiler_params=_params("arbitrary"),
        name="moba_attn",
    )(bound, qat, ka, vat, kmean)


def _dil_kernel(span, bound_ref, q_ref, kp_ref, kc_ref, vp_ref, vc_ref, o_ref, lse_ref):
    n = pl.program_id(1)
    blk = DIL_BLOCK
    nsub = q_ref.shape[0] // blk
    lane = lax.broadcasted_iota(jnp.int32, (blk, LANES), 1)
    lo_half = lane < HEAD_DIM
    qi = lax.broadcasted_iota(jnp.int32, (2 * blk, 2 * blk), 0) % blk + blk
    kj = lax.broadcasted_iota(jnp.int32, (2 * blk, 2 * blk), 1)
    dist = qi - kj
    band = jnp.where((dist >= 0) & (dist <= span), 0.0, NEG_INF)
    band_first = jnp.where(kj >= blk, band, NEG_INF)
    bound = bound_ref[0]
    fast = bound <= _FAST_BOUND
    shift = bound - _SHIFT_HEADROOM

    def blocks(fixed_shift):
        sub = shift if fixed_shift else 0.0
        mask_first = jnp.where(n > 0, band, band_first) - sub
        mask_rest = band - sub
        for pr in range(DIL_W // LANES):
            for b in range(nsub):
                block(fixed_shift, b, slice(pr * LANES, (pr + 1) * LANES), mask_first if b == 0 else mask_rest)

    def block(fixed_shift, b, cols, mask):
        q = q_ref[b * blk:(b + 1) * blk, cols]
        if b == 0:
            kk = jnp.concatenate([kp_ref[:, cols], kc_ref[0:blk, cols]], axis=0)
            vv = jnp.concatenate([vp_ref[:, cols], vc_ref[0:blk, cols]], axis=0)
        else:
            kk = kc_ref[(b - 1) * blk:(b + 1) * blk, cols]
            vv = vc_ref[(b - 1) * blk:(b + 1) * blk, cols]
        zero = jnp.zeros_like(q)
        q2 = jnp.concatenate([jnp.where(lo_half, q, zero), jnp.where(lo_half, zero, q)], axis=0)
        s = _dot_nt(q2, kk) + mask
        if fixed_shift:
            p = jnp.exp(s).astype(BF16)
            ov = _dot(p, jnp.concatenate([vv, jnp.ones_like(vv)], axis=1))
            den = ov[:, LANES:]
            o2 = ov[:, :LANES] / den
            lse2 = shift + jnp.log(den)
        else:
            m = jnp.max(s, axis=-1, keepdims=True)
            p = jnp.exp(s - m)
            den = jnp.sum(p, axis=-1, keepdims=True)
            o2 = _dot(p.astype(BF16), vv) / den
            lse2 = jnp.broadcast_to(m + jnp.log(den), (2 * blk, LANES))
        o_ref[b * blk:(b + 1) * blk, cols] = jnp.where(lo_half, o2[:blk], o2[blk:]).astype(BF16)
        lse_ref[b * blk:(b + 1) * blk, cols] = jnp.where(lo_half, lse2[:blk], lse2[blk:])

    pl.when(fast)(lambda: blocks(True))
    pl.when(jnp.logical_not(fast))(lambda: blocks(False))


def _dilated(bound, qd, kd, vd, window, dil, S):
    span = window // dil
    L = S // dil
    rows = min(DIL_ROWS, L)
    sub = rows // DIL_BLOCK
    cur = pl.BlockSpec((rows, DIL_W), lambda r, n: (n, r))
    prev = pl.BlockSpec((DIL_BLOCK, DIL_W), lambda r, n: (jnp.maximum(n * sub - 1, 0), r))
    return pl.pallas_call(
        functools.partial(_dil_kernel, span),
        grid=(dil, L // rows),
        in_specs=[pl.BlockSpec(memory_space=pltpu.SMEM), cur, prev, cur, prev, cur],
        out_specs=[cur, cur],
        out_shape=[jax.ShapeDtypeStruct((L, dil * DIL_W), BF16),
                   jax.ShapeDtypeStruct((L, dil * DIL_W), F32)],
        compiler_params=_params("parallel", "arbitrary"),
        name=f"dilated_attn_d{dil}",
    )(bound, qd, kd, kd, vd, vd)


def _merge_kernel(x_ref, g_ref, wg_ref, oa_ref, od1_ref, od2_ref, od3_ref, l1_ref, l2_ref, l3_ref,
                  qm_ref, mk_ref, mv_ref, wba_ref, wbd_ref, wbm_ref, wo_ref, out_ref, nat_ref):
    x = x_ref[...]
    tm = x.shape[0]
    h = _rms_rows(x, g_ref[...]).astype(BF16)

    def natural(ref, dil, slot):
        if dil == 1:
            return ref[...].astype(F32)
        tiles = DIL_W // LANES
        for r in range(dil):
            for lt in range(tiles):
                col = r * DIL_W + lt * LANES
                nat_ref[slot * tiles + lt, pl.ds(r, tm // dil, stride=dil), :] = (
                    ref[:, col:col + LANES].astype(F32))
        return jnp.concatenate([nat_ref[slot * tiles + lt] for lt in range(tiles)], axis=1)

    lane = lax.broadcasted_iota(jnp.int32, (tm, LANES), 1)
    lo_half = lane < HEAD_DIM
    mem_pairs = MEM_W // LANES
    scores = []
    for pr in range(mem_pairs):
        q = qm_ref[:, pr * LANES:(pr + 1) * LANES]
        zero = jnp.zeros_like(q)
        q2 = jnp.concatenate([jnp.where(lo_half, q, zero), jnp.where(lo_half, zero, q)], axis=0)
        scores.append(_dot_nt(q2, mk_ref[:, pr * LANES:(pr + 1) * LANES]))
    graw = [_dot(h, wg_ref[:, bi * D_MODEL:(bi + 1) * D_MODEL]) for bi in range(3)]

    dils = [d for _, d in DIL_PATTERNS]
    l1, l2, l3 = [natural(ref, d, k) for k, (ref, d) in enumerate(zip((l1_ref, l2_ref, l3_ref), dils))]
    o1, o2, o3 = [natural(ref, d, 3 + k) for k, (ref, d) in enumerate(zip((od1_ref, od2_ref, od3_ref), dils))]
    lmax = jnp.maximum(jnp.maximum(l1, l2), l3)
    e1, e2, e3 = jnp.exp(l1 - lmax), jnp.exp(l2 - lmax), jnp.exp(l3 - lmax)
    o_d = (e1 * o1 + e2 * o2 + e3 * o3) / (e1 + e2 + e3)

    probs = []
    for s in scores:
        p = jnp.exp(s - jnp.max(s, axis=-1, keepdims=True))
        probs.append((p / jnp.sum(p, axis=-1, keepdims=True)).astype(BF16))
    o_m = []
    for pr in range(mem_pairs):
        o2 = _dot(probs[pr], mv_ref[:, pr * LANES:(pr + 1) * LANES])
        o_m.append(jnp.where(lo_half, o2[:tm], o2[tm:]))
    o_m = jnp.concatenate(o_m, axis=1)

    merged = None
    branches = ((oa_ref[...], wba_ref), (o_d.astype(BF16), wbd_ref), (o_m.astype(BF16), wbm_ref))
    for bi, (o_b, w_ref) in enumerate(branches):
        term = jax.nn.sigmoid(graw[bi]) * _dot(o_b, w_ref[...])
        merged = term if merged is None else merged + term
    out_ref[...] = x + _dot(merged.astype(BF16), wo_ref[...])


def _merge(x, g_mix, w_gate, o_a, o_ds, lses, qm, mk, mv, w_ba, w_bd, w_bm, w_out, S):
    tm = TM_MERGE
    const = lambda shape: pl.BlockSpec(shape, lambda i: (0,) * len(shape), pipeline_mode=pl.Buffered(1))
    rows = lambda w: pl.BlockSpec((tm, w), lambda i: (i, 0))
    dil_rows = [pl.BlockSpec((tm // d, d * DIL_W), lambda i: (i, 0)) for _, d in DIL_PATTERNS]
    return pl.pallas_call(
        _merge_kernel,
        grid=(S // tm,),
        in_specs=[rows(D_MODEL), const((1, D_MODEL)), const((D_MODEL, 3 * D_MODEL)),
                  rows(MOBA_W), *dil_rows, *dil_rows,
                  rows(MEM_W), const((N_MEM, MEM_W)), const((N_MEM, MEM_W)),
                  const((MOBA_W, D_MODEL)), const((DIL_W, D_MODEL)), const((MEM_W, D_MODEL)),
                  const((D_MODEL, D_MODEL))],
        out_specs=rows(D_MODEL),
        out_shape=jax.ShapeDtypeStruct((S, D_MODEL), F32),
        scratch_shapes=[pltpu.VMEM((2 * len(DIL_PATTERNS) * DIL_W // LANES, tm, LANES), F32)],
        compiler_params=_params("parallel"),
        name="gated_merge",
    )(x, g_mix, w_gate, o_a, *o_ds, *lses, qm, mk, mv, w_ba, w_bd, w_bm, w_out)


def _ffn_kernel(x_ref, g_ref, wup_ref, cw_ref, cb_ref, wdn_ref, out_ref, halo_ref, act_ref):
    i = pl.program_id(0)
    tm = x_ref.shape[0]

    @pl.when(i == 0)
    def _():
        halo_ref[...] = jnp.zeros_like(halo_ref)

    x = x_ref[...]
    h = _rms_rows(x, g_ref[...]).astype(BF16)

    row8 = lax.broadcasted_iota(jnp.int32, (8, FFN_CHUNK), 0)

    def conv(u, col):
        prev = halo_ref[:, col:col + FFN_CHUNK]
        halo_ref[:, col:col + FFN_CHUNK] = u[tm - 8:, :]
        p6 = jnp.broadcast_to(prev[6:7], (8, FFN_CHUNK))
        p7 = jnp.broadcast_to(prev[7:8], (8, FFN_CHUNK))
        r1 = pltpu.roll(u, 1, axis=0)
        r2 = pltpu.roll(u, 2, axis=0)
        top1 = jnp.where(row8 == 0, p7, r1[:8])
        top2 = jnp.where(row8 == 0, p6, jnp.where(row8 == 1, p7, r2[:8]))
        u1 = jnp.concatenate([top1, r1[8:]], axis=0)
        u2 = jnp.concatenate([top2, r2[8:]], axis=0)
        w = cw_ref[:, col:col + FFN_CHUNK]
        return cb_ref[:, col:col + FFN_CHUNK] + w[0:1] * u2 + w[1:2] * u1 + w[2:3] * u

    for c in range(D_FF // FFN_CHUNK):
        cg = c * FFN_CHUNK
        cv = D_FF + c * FFN_CHUNK
        u_g = conv(_dot(h, wup_ref[:, cg:cg + FFN_CHUNK]), cg)
        u_v = conv(_dot(h, wup_ref[:, cv:cv + FFN_CHUNK]), cv)
        act_ref[:, cg:cg + FFN_CHUNK] = (jax.nn.silu(u_g) * u_v).astype(BF16)
    out_ref[...] = x + _dot(act_ref[...], wdn_ref[...])


def _ffn(x, g_ffn, w_up, conv_w, conv_b, w_down, S):
    tm = TM_FFN
    const = lambda shape: pl.BlockSpec(shape, lambda i: (0,) * len(shape), pipeline_mode=pl.Buffered(1))
    rows = pl.BlockSpec((tm, D_MODEL), lambda i: (i, 0))
    return pl.pallas_call(
        _ffn_kernel,
        grid=(S // tm,),
        in_specs=[rows, const((1, D_MODEL)), const((D_MODEL, 2 * D_FF)),
                  const((CONV_WIDTH, 2 * D_FF)), const((1, 2 * D_FF)), const((D_FF, D_MODEL))],
        out_specs=rows,
        out_shape=jax.ShapeDtypeStruct((S, D_MODEL), F32),
        scratch_shapes=[pltpu.VMEM((8, 2 * D_FF), F32), pltpu.VMEM((tm, D_FF), BF16)],
        compiler_params=_params("arbitrary"),
        name="conv_ffn",
    )(x, g_ffn, w_up, conv_w, conv_b, w_down)


def _group_mean_matrix():
    g = np.arange(MXU_TILE) // HEAD_DIM
    return jnp.asarray((g[:, None] == g[None, :]).astype(np.float32) / HEAD_DIM, dtype=BF16)


def _layer(x, mem, positions, p):
    S = x.shape[0]
    assert S % (max(d for _, d in DIL_PATTERNS) * DIL_BLOCK) == 0 and S % TM_PROJ == 0
    row = lambda v: v.reshape(1, -1).astype(F32)
    c = np.cumsum([0, MOBA_W, MOBA_W, MOBA_W, DIL_W, DIL_W, DIL_W, MEM_W])
    w_in = p["w_in"]
    seg = lambda k: w_in[:, c[k]:c[k + 1]]
    wt = jnp.concatenate([seg(0), seg(2)], axis=1).T.astype(BF16)
    wn = jnp.concatenate([seg(1), seg(3), seg(4), seg(6), seg(5)], axis=1).astype(BF16)
    w_gate = w_in[:, QKV_COLS:].astype(BF16)
    gnat = jnp.concatenate([jnp.tile(p["moba_k_norm_g"], N_MOBA_HEADS), jnp.tile(p["dil_q_norm_g"], N_DIL_HEADS),
                            jnp.tile(p["dil_k_norm_g"], N_DIL_HEADS), jnp.tile(p["mem_q_norm_g"], N_MEM_HEADS),
                            jnp.ones((_NAT_NORM_PAD - _NAT_NORM,), F32)]).reshape(1, _NAT_NORM_PAD)
    gmat = _group_mean_matrix()

    cos_t, sin_t, cos_n, sin_n = _rope_tables(positions, S)
    mk, mv = _mem_kv(mem, row(p["mem_norm_g"]), p["w_mem_kv"].astype(BF16),
                     row(jnp.tile(p["mem_k_norm_g"], N_MEM_HEADS)), gmat)
    (qat, vat, ka, kmean, qm), dil_qkv = _proj(
        x, row(p["mix_norm_g"]), wt, wn, p["moba_q_norm_g"].reshape(HEAD_DIM, 1).astype(F32), gnat, gmat,
        cos_t, sin_t, cos_n, sin_n, S)
    def score_bound(gq, gk):
        return (1.02 * HEAD_DIM * SCALE * jnp.max(jnp.abs(gq)) * jnp.max(jnp.abs(gk))).reshape(1).astype(F32)

    o_a = _moba(score_bound(p["moba_q_norm_g"], p["moba_k_norm_g"]), qat, ka, vat,
                kmean.reshape(S // MOBA_BLOCK, MOBA_W), S)
    bound_d = score_bound(p["dil_q_norm_g"], p["dil_k_norm_g"])
    o_ds, lses = zip(*[_dilated(bound_d, *qkv, w, d, S) for qkv, (w, d) in zip(dil_qkv, DIL_PATTERNS)])
    x1 = _merge(x, row(p["mix_norm_g"]), w_gate, o_a, o_ds, lses, qm, mk, mv,
                p["w_branch_moba"].astype(BF16), p["w_branch_dil"].astype(BF16),
                p["w_branch_mem"].astype(BF16), p["w_out"].astype(BF16), S)
    return _ffn(x1, row(p["ffn_norm_g"]), p["w_ffn_up"].astype(BF16), p["ffn_conv_w"].astype(F32),
                row(p["ffn_conv_b"]), p["w_ffn_down"].astype(BF16), S)


def kernel(x, mem, positions, mix_norm_g, mem_norm_g, w_in, moba_q_norm_g, moba_k_norm_g, dil_q_norm_g, dil_k_norm_g, mem_q_norm_g, mem_k_norm_g, w_mem_kv, w_branch_moba, w_branch_dil, w_branch_mem, w_out, ffn_norm_g, w_ffn_up, ffn_conv_w, ffn_conv_b, w_ffn_down):
    params = dict(mix_norm_g=mix_norm_g, mem_norm_g=mem_norm_g, w_in=w_in, moba_q_norm_g=moba_q_norm_g,
                  moba_k_norm_g=moba_k_norm_g, dil_q_norm_g=dil_q_norm_g, dil_k_norm_g=dil_k_norm_g,
                  mem_q_norm_g=mem_q_norm_g, mem_k_norm_g=mem_k_norm_g, w_mem_kv=w_mem_kv,
                  w_branch_moba=w_branch_moba, w_branch_dil=w_branch_dil, w_branch_mem=w_branch_mem,
                  w_out=w_out, ffn_norm_g=ffn_norm_g, w_ffn_up=w_ffn_up, ffn_conv_w=ffn_conv_w,
                  ffn_conv_b=ffn_conv_b, w_ffn_down=w_ffn_down)
    B = x.shape[0]
    depth = w_in.shape[0]
    outs = []
    for b in range(B):
        xb = x.reshape(x.shape[1:]) if B == 1 else x[b]
        for l in range(depth):
            xb = _layer(xb, mem[b], positions[b], {k: v[l] for k, v in params.items()})
        outs.append(xb)
    return outs[0].reshape(x.shape) if B == 1 else jnp.stack(outs, axis=0)
```

```python
import functools

import numpy as np
import jax
import jax.numpy as jnp
from jax import lax
from jax.experimental import pallas as pl
from jax.experimental.pallas import tpu as pltpu

D_MODEL = 1024
HEAD_DIM = 64
HALF = HEAD_DIM // 2
N_MOBA_HEADS = 6
N_DIL_HEADS = 6
N_MEM_HEADS = 4
N_MEM = 256
MOBA_BLOCK = 256
MOBA_TOPK = 3
DIL_PATTERNS = ((128, 1), (512, 4), (2048, 16))
DIL_BLOCK = 128
D_FF = 2816
CONV_WIDTH = 3
ROPE_THETA = 10000.0
EPS = 1e-6
MOBA_W = N_MOBA_HEADS * HEAD_DIM
DIL_W = N_DIL_HEADS * HEAD_DIM
MEM_W = N_MEM_HEADS * HEAD_DIM
QKV_COLS = 3 * MOBA_W + 3 * DIL_W + MEM_W
SCALE = HEAD_DIM ** -0.5

LANES = 128
MXU_TILE = 256
VMEM_LIMIT = 56 * 1024 * 1024

TM_ROPE = 2048
TM_PROJ = 512
TM_MERGE = 512
TM_FFN = 512
FFN_CHUNK = 256
DIL_ROWS = 1024

F32 = jnp.float32
BF16 = jnp.bfloat16
NEG_INF = float("-inf")


def _dot(a, b):
    return jnp.dot(a, b, preferred_element_type=F32)


def _dot_nt(a, b):
    return lax.dot_general(a, b, (((1,), (1,)), ((), ())), preferred_element_type=F32)


def _rms_rows(x, g):
    ms = jnp.mean(x * x, axis=-1, keepdims=True)
    return x * lax.rsqrt(ms + EPS) * g


def _params(*sem):
    return pltpu.CompilerParams(dimension_semantics=sem, vmem_limit_bytes=VMEM_LIMIT)


def _rope_kernel(pos_ref, inv_ref, cos_t_ref, sin_t_ref, cos_n_ref, sin_n_ref):
    pos = pos_ref[...].astype(F32)
    ang = inv_ref[...] * pos
    c = jnp.cos(ang)
    s = jnp.sin(ang)
    cos_t_ref[...] = jnp.concatenate([c, c], axis=0)
    sin_t_ref[...] = jnp.concatenate([-s, s], axis=0)
    cos_n_ref[...] = jnp.concatenate([c, c, c, c], axis=0).T
    sin_n_ref[...] = jnp.concatenate([-s, s, -s, s], axis=0).T


def _rope_tables(positions, S):
    tm = min(TM_ROPE, S)
    inv = (ROPE_THETA ** (-jnp.arange(HALF, dtype=F32) / HALF)).reshape(HALF, 1)
    return pl.pallas_call(
        _rope_kernel,
        grid=(S // tm,),
        in_specs=[pl.BlockSpec((1, tm), lambda i: (0, i)),
                  pl.BlockSpec((HALF, 1), lambda i: (0, 0))],
        out_specs=[pl.BlockSpec((HEAD_DIM, tm), lambda i: (0, i)),
                   pl.BlockSpec((HEAD_DIM, tm), lambda i: (0, i)),
                   pl.BlockSpec((tm, LANES), lambda i: (i, 0)),
                   pl.BlockSpec((tm, LANES), lambda i: (i, 0))],
        out_shape=[jax.ShapeDtypeStruct((HEAD_DIM, S), F32),
                   jax.ShapeDtypeStruct((HEAD_DIM, S), F32),
                   jax.ShapeDtypeStruct((S, LANES), F32),
                   jax.ShapeDtypeStruct((S, LANES), F32)],
        compiler_params=_params("parallel"),
        name="rope_tables",
    )(positions.reshape(1, S), inv)


def _memkv_kernel(mem_ref, g_ref, w_ref, gk_ref, gmat_ref, mk_ref, mv_ref):
    h = _rms_rows(mem_ref[...], g_ref[...]).astype(BF16)
    kv = _dot(h, w_ref[...])
    k = kv[:, :MEM_W]
    ms = _dot((k * k).astype(BF16), gmat_ref[...])
    mk_ref[...] = (k * lax.rsqrt(ms + EPS) * gk_ref[...]).astype(BF16)
    mv_ref[...] = kv[:, MEM_W:].astype(BF16)


def _mem_kv(mem, g, w_kv, gk_tiled, gmat):
    full = lambda shape: pl.BlockSpec(shape, lambda i: (0,) * len(shape))
    return pl.pallas_call(
        _memkv_kernel,
        grid=(1,),
        in_specs=[full((N_MEM, D_MODEL)), full((1, D_MODEL)), full((D_MODEL, 2 * MEM_W)),
                  full((1, MEM_W)), full((MXU_TILE, MXU_TILE))],
        out_specs=[full((N_MEM, MEM_W)), full((N_MEM, MEM_W))],
        out_shape=[jax.ShapeDtypeStruct((N_MEM, MEM_W), BF16)] * 2,
        compiler_params=_params("arbitrary"),
        name="mem_kv",
    )(mem, g, w_kv, gk_tiled, gmat)


_NAT_NORM = 3 * 384 + MEM_W
_NAT_NORM_PAD = 1536
_NAT_ROPE = 3 * 384
_NAT_COLS = _NAT_NORM + DIL_W


def _proj_kernel(x_ref, g_ref, wt_ref, wn_ref, gq_ref, gnat_ref, gmat_ref,
                 cos_t_ref, sin_t_ref, cos_n_ref, sin_n_ref,
                 qat_ref, vat_ref, ka_ref, km_ref, qm_ref, *rest):
    dil_refs, stage_ref = rest[:-1], rest[-1]
    tm = x_ref.shape[0]
    h = _rms_rows(x_ref[...], g_ref[...]).astype(BF16)

    yt = _dot_nt(wt_ref[...], h)
    cos_t = cos_t_ref[...]
    sin_t = sin_t_ref[...]
    gq = gq_ref[...]
    for hd in range(N_MOBA_HEADS):
        q = yt[hd * HEAD_DIM:(hd + 1) * HEAD_DIM]
        ms = jnp.mean(q * q, axis=0, keepdims=True)
        q = q * lax.rsqrt(ms + EPS) * gq
        rot = jnp.concatenate([q[HALF:], q[:HALF]], axis=0)
        q = (q * cos_t + rot * sin_t) * SCALE
        for b in range(tm // MOBA_BLOCK):
            qat_ref[b, hd * HEAD_DIM:(hd + 1) * HEAD_DIM, :] = (
                q[:, b * MOBA_BLOCK:(b + 1) * MOBA_BLOCK].astype(BF16))
    for b in range(tm // MOBA_BLOCK):
        vat_ref[b] = yt[MOBA_W:, b * MOBA_BLOCK:(b + 1) * MOBA_BLOCK].astype(BF16)

    gmat = gmat_ref[...]
    lane = lax.broadcasted_iota(jnp.int32, (tm, LANES), 1)
    first_half = (lane % HEAD_DIM) < HALF
    cos_n = cos_n_ref[...]
    sin_n = sin_n_ref[...]
    pairs = DIL_W // LANES

    def emit_dilated(ti, tiles):
        for lt in range(pairs):
            stage_ref[ti * pairs + lt] = tiles[lt]
        for pi, (_, d) in enumerate(DIL_PATTERNS):
            ref = dil_refs[3 * pi + ti]
            for r in range(d):
                for lt in range(pairs):
                    col = r * DIL_W + lt * LANES
                    ref[:, col:col + LANES] = stage_ref[
                        ti * pairs + lt, pl.ds(r, tm // d, stride=d), :].astype(BF16)

    chunks = []
    for c in range(_NAT_NORM_PAD // MXU_TILE):
        if c % 2 == 0:
            wide = _dot(h, wn_ref[:, c * MXU_TILE:(c + 2) * MXU_TILE])
        raw = wide[:, (c % 2) * MXU_TILE:(c % 2 + 1) * MXU_TILE]
        ms = _dot((raw * raw).astype(BF16), gmat)
        blk = raw * lax.rsqrt(ms + EPS) * gnat_ref[:, c * MXU_TILE:(c + 1) * MXU_TILE]
        for hc in range(MXU_TILE // LANES):
            col = c * MXU_TILE + hc * LANES
            if col >= _NAT_NORM:
                chunks.append(raw[:, hc * LANES:(hc + 1) * LANES])
                continue
            v = blk[:, hc * LANES:(hc + 1) * LANES]
            if col < _NAT_ROPE:
                rot = jnp.where(first_half, pltpu.roll(v, LANES - HALF, axis=1),
                                pltpu.roll(v, HALF, axis=1))
                v = v * cos_n + rot * sin_n
            chunks.append(v)
        if len(chunks) >= pairs and c == 1:
            ka = jnp.concatenate(chunks[0:pairs], axis=1)
            for b in range(tm // MOBA_BLOCK):
                km_ref[b] = jnp.mean(ka[b * MOBA_BLOCK:(b + 1) * MOBA_BLOCK], axis=0, keepdims=True)
            ka_ref[...] = ka.astype(BF16)
        if c == 2:
            emit_dilated(0, [t * SCALE for t in chunks[pairs:2 * pairs]])
        if c == 4:
            emit_dilated(1, chunks[2 * pairs:3 * pairs])
    qm_ref[...] = (jnp.concatenate(chunks[3 * pairs:3 * pairs + MEM_W // LANES], axis=1) * SCALE).astype(BF16)
    vd_rest = _dot(h, wn_ref[:, _NAT_NORM_PAD:])
    emit_dilated(2, [chunks[-1]] + [vd_rest[:, t * LANES:(t + 1) * LANES] for t in range(pairs - 1)])


def _proj(x, g_mix, wt, wn, gq_col, gnat, gmat, cos_t, sin_t, cos_n, sin_n, S):
    tm = TM_PROJ
    nb = S // MOBA_BLOCK
    bpt = tm // MOBA_BLOCK
    const = lambda shape: pl.BlockSpec(shape, lambda i: (0,) * len(shape))
    rows = lambda w: pl.BlockSpec((tm, w), lambda i: (i, 0))
    dils = [d for _, d in DIL_PATTERNS for _ in range(3)]
    outs = pl.pallas_call(
        _proj_kernel,
        grid=(S // tm,),
        in_specs=[rows(D_MODEL), const((1, D_MODEL)), const((2 * MOBA_W, D_MODEL)),
                  const((D_MODEL, _NAT_COLS)), const((HEAD_DIM, 1)), const((1, _NAT_NORM_PAD)),
                  const((MXU_TILE, MXU_TILE)),
                  pl.BlockSpec((HEAD_DIM, tm), lambda i: (0, i)),
                  pl.BlockSpec((HEAD_DIM, tm), lambda i: (0, i)),
                  rows(LANES), rows(LANES)],
        out_specs=[pl.BlockSpec((bpt, MOBA_W, MOBA_BLOCK), lambda i: (i, 0, 0)),
                   pl.BlockSpec((bpt, MOBA_W, MOBA_BLOCK), lambda i: (i, 0, 0)),
                   rows(MOBA_W),
                   pl.BlockSpec((bpt, 1, MOBA_W), lambda i: (i, 0, 0)),
                   rows(MEM_W)]
                  + [pl.BlockSpec((tm // d, d * DIL_W), lambda i: (i, 0)) for d in dils],
        out_shape=[jax.ShapeDtypeStruct((nb, MOBA_W, MOBA_BLOCK), BF16),
                   jax.ShapeDtypeStruct((nb, MOBA_W, MOBA_BLOCK), BF16),
                   jax.ShapeDtypeStruct((S, MOBA_W), BF16),
                   jax.ShapeDtypeStruct((nb, 1, MOBA_W), F32),
                   jax.ShapeDtypeStruct((S, MEM_W), BF16)]
                  + [jax.ShapeDtypeStruct((S // d, d * DIL_W), BF16) for d in dils],
        scratch_shapes=[pltpu.VMEM((3 * DIL_W // LANES, tm, LANES), F32)],
        compiler_params=_params("parallel"),
        name="in_proj",
    )(x, g_mix, wt, wn, gq_col, gnat, gmat, cos_t, sin_t, cos_n, sin_n)
    return outs[:5], [outs[5 + 3 * k:8 + 3 * k] for k in range(len(DIL_PATTERNS))]


_SHIFT_HEADROOM = 30.0
_FAST_BOUND = 50.0
_MOBA_UNROLL = 4
_MOBA_TRIP_GROUPS = (1, 2, 4)


def _moba_kernel(bound_ref, qt_ref, k_ref, vt_ref, km_ref, o_ref, bias_ref, ot_ref, acc_ref):
    i = pl.program_id(0)
    tq = MOBA_BLOCK
    unroll = _MOBA_UNROLL
    heads = N_MOBA_HEADS
    nb = km_ref.shape[0]
    row = lax.broadcasted_iota(jnp.int32, (LANES, tq), 0)
    blk = lax.broadcasted_iota(jnp.int32, (nb, tq), 0)

    bound = bound_ref[0]
    fast = bound <= _FAST_BOUND
    sel_bias = jnp.where(fast, _SHIFT_HEADROOM - bound, 0.0)

    def pair_cols(hd):
        return slice((hd // 2) * LANES, (hd // 2 + 1) * LANES)

    qs = []
    for hd in range(heads):
        qt = qt_ref[0, pair_cols(hd), :]
        q_h = jnp.where((row // HEAD_DIM) == hd % 2, qt, jnp.zeros_like(qt))
        qs.append(q_h)
        km = km_ref[:, pair_cols(hd)]
        km1 = km.astype(BF16)
        r1 = km - km1.astype(F32)
        km2 = r1.astype(BF16)
        km3 = (r1 - km2.astype(F32)).astype(BF16)
        g3 = _dot(jnp.concatenate([km1, km2, km3], axis=0), q_h)
        gate = g3[:nb] + g3[nb:2 * nb] + g3[2 * nb:]
        gate = jnp.where(blk < i, gate, NEG_INF)
        bias = jnp.full((nb, tq), NEG_INF, F32)
        for _ in range(MOBA_TOPK):
            m = jnp.max(gate, axis=0, keepdims=True)
            idx = jnp.min(jnp.where(gate == m, blk, nb), axis=0, keepdims=True)
            hit = blk == idx
            bias = jnp.where(hit & (m > NEG_INF), sel_bias, bias)
            gate = jnp.where(hit, NEG_INF, gate)
        bias_ref[hd] = bias

    kpos = lax.broadcasted_iota(jnp.int32, (tq, tq), 0)
    qpos = lax.broadcasted_iota(jnp.int32, (tq, tq), 1)
    causal = kpos <= qpos

    def k_rows(j, hd):
        start = j * tq if isinstance(j, int) else pl.multiple_of(j * tq, tq)
        return k_ref[pl.ds(start, tq), pair_cols(hd)]

    def v_rows(j, hd):
        return vt_ref[j, hd * HEAD_DIM:(hd + 1) * HEAD_DIM, :]

    ones = jnp.ones((16, tq), BF16)

    def values(j, hd, p):
        return _dot(jnp.concatenate([v_rows(j, hd), ones], axis=0), p)

    own_s = [_dot(k_rows(i, hd), qs[hd]) for hd in range(heads)]
    own_p = [jnp.exp(jnp.where(causal, s + sel_bias, NEG_INF)).astype(BF16) for s in own_s]

    @pl.when(fast)
    def _():
        def attend(accs, blocks):
            out = list(accs)
            chains = [(j, hd) for j in blocks for hd in range(heads)]
            ss = [_dot(k_rows(j, hd), qs[hd]) for j, hd in chains]
            ps = [jnp.exp(s + bias_ref[hd, pl.ds(j, 1), :]).astype(BF16) for s, (j, hd) in zip(ss, chains)]
            for p, (j, hd) in zip(ps, chains):
                out[hd] = out[hd] + values(j, hd, p)
            return out

        rem = i % unroll
        for r in range(unroll):
            @pl.when(rem == r)
            def _():
                own = [values(i, hd, p) for hd, p in enumerate(own_p)]
                for hd, a in enumerate(attend(own, range(r))):
                    acc_ref[hd] = a

        groups = i // unroll
        accs = tuple(acc_ref[hd] for hd in range(heads))
        start = rem
        for span in _MOBA_TRIP_GROUPS:
            last = span == _MOBA_TRIP_GROUPS[-1]
            trips = groups // span if last else (groups // span) % 2

            def body(t, accs, start=start, span=span):
                first = start + span * unroll * t
                return tuple(attend(accs, [first + u for u in range(span * unroll)]))

            accs = lax.fori_loop(0, trips, body, accs)
            start = start + span * unroll * trips
        for hd, a in enumerate(accs):
            ot_ref[hd * HEAD_DIM:(hd + 1) * HEAD_DIM, :] = a[:HEAD_DIM] / a[HEAD_DIM:HEAD_DIM + 1]

    @pl.when(jnp.logical_not(fast))
    def _():
        for pr in range(heads // 2):
            pair = (2 * pr, 2 * pr + 1)
            carry = []
            for hd in pair:
                s = jnp.where(causal, _dot(k_rows(i, hd), qs[hd]), NEG_INF)
                m = jnp.max(s, axis=0, keepdims=True)
                p = jnp.exp(s - m)
                l = jnp.sum(p, axis=0, keepdims=True)
                carry += [m, l, _dot(v_rows(i, hd), p.astype(BF16))]

            def body(j, carry):
                out = []
                for c, hd in enumerate(pair):
                    m, l, acc = carry[3 * c:3 * c + 3]
                    s = _dot(k_rows(j, hd), qs[hd]) + bias_ref[hd, pl.ds(j, 1), :]
                    m_new = jnp.maximum(m, jnp.max(s, axis=0, keepdims=True))
                    alpha = jnp.exp(m - m_new)
                    p = jnp.exp(s - m_new)
                    l = alpha * l + jnp.sum(p, axis=0, keepdims=True)
                    acc = alpha * acc + _dot(v_rows(j, hd), p.astype(BF16))
                    out += [m_new, l, acc]
                return tuple(out)

            carry = lax.fori_loop(0, i, body, tuple(carry))
            for c, hd in enumerate(pair):
                ot_ref[hd * HEAD_DIM:(hd + 1) * HEAD_DIM, :] = carry[3 * c + 2] / carry[3 * c + 1]

    o_ref[...] = ot_ref[...].T.astype(BF16)


def _moba(bound, qat, ka, vat, kmean, S):
    nb = S // MOBA_BLOCK
    once = lambda shape: pl.BlockSpec(shape, lambda i: (0,) * len(shape), pipeline_mode=pl.Buffered(1))
    return pl.pallas_call(
        _moba_kernel,
        grid=(nb,),
        in_specs=[pl.BlockSpec(memory_space=pltpu.SMEM),
                  pl.BlockSpec((1, MOBA_W, MOBA_BLOCK), lambda i: (i, 0, 0)),
                  once((S, MOBA_W)), once((nb, MOBA_W, MOBA_BLOCK)), once((nb, MOBA_W))],
        out_specs=pl.BlockSpec((MOBA_BLOCK, MOBA_W), lambda i: (i, 0)),
        out_shape=jax.ShapeDtypeStruct((S, MOBA_W), BF16),
        scratch_shapes=[pltpu.VMEM((N_MOBA_HEADS, nb, MOBA_BLOCK), F32),
                        pltpu.VMEM((MOBA_W, MOBA_BLOCK), F32),
                        pltpu.VMEM((N_MOBA_HEADS, HEAD_DIM + 16, MOBA_BLOCK), F32)],
        compiler_params=_params("arbitrary"),
        name="moba_attn",
    )(bound, qat, ka, vat, kmean)


def _dil_kernel(span, bound_ref, q_ref, kp_ref, kc_ref, vp_ref, vc_ref, o_ref, lse_ref):
    n = pl.program_id(1)
    blk = DIL_BLOCK
    nsub = q_ref.shape[0] // blk
    lane = lax.broadcasted_iota(jnp.int32, (blk, LANES), 1)
    lo_half = lane < HEAD_DIM
    qi = lax.broadcasted_iota(jnp.int32, (2 * blk, 2 * blk), 0) % blk + blk
    kj = lax.broadcasted_iota(jnp.int32, (2 * blk, 2 * blk), 1)
    dist = qi - kj
    band = jnp.where((dist >= 0) & (dist <= span), 0.0, NEG_INF)
    band_first = jnp.where(kj >= blk, band, NEG_INF)
    bound = bound_ref[0]
    fast = bound <= _FAST_BOUND
    shift = bound - _SHIFT_HEADROOM

    def blocks(fixed_shift):
        sub = shift if fixed_shift else 0.0
        mask_first = jnp.where(n > 0, band, band_first) - sub
        mask_rest = band - sub
        for pr in range(DIL_W // LANES):
            for b in range(nsub):
                block(fixed_shift, b, slice(pr * LANES, (pr + 1) * LANES), mask_first if b == 0 else mask_rest)

    def block(fixed_shift, b, cols, mask):
        q = q_ref[b * blk:(b + 1) * blk, cols]
        if b == 0:
            kk = jnp.concatenate([kp_ref[:, cols], kc_ref[0:blk, cols]], axis=0)
            vv = jnp.concatenate([vp_ref[:, cols], vc_ref[0:blk, cols]], axis=0)
        else:
            kk = kc_ref[(b - 1) * blk:(b + 1) * blk, cols]
            vv = vc_ref[(b - 1) * blk:(b + 1) * blk, cols]
        zero = jnp.zeros_like(q)
        q2 = jnp.concatenate([jnp.where(lo_half, q, zero), jnp.where(lo_half, zero, q)], axis=0)
        s = _dot_nt(q2, kk) + mask
        if fixed_shift:
            p = jnp.exp(s).astype(BF16)
            ov = _dot(p, jnp.concatenate([vv, jnp.ones_like(vv)], axis=1))
            den = ov[:, LANES:]
            o2 = ov[:, :LANES] / den
            lse2 = shift + jnp.log(den)
        else:
            m = jnp.max(s, axis=-1, keepdims=True)
            p = jnp.exp(s - m)
            den = jnp.sum(p, axis=-1, keepdims=True)
            o2 = _dot(p.astype(BF16), vv) / den
            lse2 = jnp.broadcast_to(m + jnp.log(den), (2 * blk, LANES))
        o_ref[b * blk:(b + 1) * blk, cols] = jnp.where(lo_half, o2[:blk], o2[blk:]).astype(BF16)
        lse_ref[b * blk:(b + 1) * blk, cols] = jnp.where(lo_half, lse2[:blk], lse2[blk:])

    pl.when(fast)(lambda: blocks(True))
    pl.when(jnp.logical_not(fast))(lambda: blocks(False))


def _dilated(bound, qd, kd, vd, window, dil, S):
    span = window // dil
    L = S // dil
    rows = min(DIL_ROWS, L)
    sub = rows // DIL_BLOCK
    cur = pl.BlockSpec((rows, DIL_W), lambda r, n: (n, r))
    prev = pl.BlockSpec((DIL_BLOCK, DIL_W), lambda r, n: (jnp.maximum(n * sub - 1, 0), r))
    return pl.pallas_call(
        functools.partial(_dil_kernel, span),
        grid=(dil, L // rows),
        in_specs=[pl.BlockSpec(memory_space=pltpu.SMEM), cur, prev, cur, prev, cur],
        out_specs=[cur, cur],
        out_shape=[jax.ShapeDtypeStruct((L, dil * DIL_W), BF16),
                   jax.ShapeDtypeStruct((L, dil * DIL_W), F32)],
        compiler_params=_params("parallel", "arbitrary"),
        name=f"dilated_attn_d{dil}",
    )(bound, qd, kd, kd, vd, vd)


def _merge_kernel(x_ref, g_ref, wg_ref, oa_ref, od1_ref, od2_ref, od3_ref, l1_ref, l2_ref, l3_ref,
                  qm_ref, mk_ref, mv_ref, wba_ref, wbd_ref, wbm_ref, wo_ref, out_ref, nat_ref):
    x = x_ref[...]
    tm = x.shape[0]
    h = _rms_rows(x, g_ref[...]).astype(BF16)

    def natural(ref, dil, slot):
        if dil == 1:
            return ref[...].astype(F32)
        tiles = DIL_W // LANES
        for r in range(dil):
            for lt in range(tiles):
                col = r * DIL_W + lt * LANES
                nat_ref[slot * tiles + lt, pl.ds(r, tm // dil, stride=dil), :] = (
                    ref[:, col:col + LANES].astype(F32))
        return jnp.concatenate([nat_ref[slot * tiles + lt] for lt in range(tiles)], axis=1)

    lane = lax.broadcasted_iota(jnp.int32, (tm, LANES), 1)
    lo_half = lane < HEAD_DIM
    mem_pairs = MEM_W // LANES
    scores = []
    for pr in range(mem_pairs):
        q = qm_ref[:, pr * LANES:(pr + 1) * LANES]
        zero = jnp.zeros_like(q)
        q2 = jnp.concatenate([jnp.where(lo_half, q, zero), jnp.where(lo_half, zero, q)], axis=0)
        scores.append(_dot_nt(q2, mk_ref[:, pr * LANES:(pr + 1) * LANES]))
    graw = [_dot(h, wg_ref[:, bi * D_MODEL:(bi + 1) * D_MODEL]) for bi in range(3)]

    dils = [d for _, d in DIL_PATTERNS]
    l1, l2, l3 = [natural(ref, d, k) for k, (ref, d) in enumerate(zip((l1_ref, l2_ref, l3_ref), dils))]
    o1, o2, o3 = [natural(ref, d, 3 + k) for k, (ref, d) in enumerate(zip((od1_ref, od2_ref, od3_ref), dils))]
    lmax = jnp.maximum(jnp.maximum(l1, l2), l3)
    e1, e2, e3 = jnp.exp(l1 - lmax), jnp.exp(l2 - lmax), jnp.exp(l3 - lmax)
    o_d = (e1 * o1 + e2 * o2 + e3 * o3) / (e1 + e2 + e3)

    probs = []
    for s in scores:
        p = jnp.exp(s - jnp.max(s, axis=-1, keepdims=True))
        probs.append((p / jnp.sum(p, axis=-1, keepdims=True)).astype(BF16))
    o_m = []
    for pr in range(mem_pairs):
        o2 = _dot(probs[pr], mv_ref[:, pr * LANES:(pr + 1) * LANES])
        o_m.append(jnp.where(lo_half, o2[:tm], o2[tm:]))
    o_m = jnp.concatenate(o_m, axis=1)

    merged = None
    branches = ((oa_ref[...], wba_ref), (o_d.astype(BF16), wbd_ref), (o_m.astype(BF16), wbm_ref))
    for bi, (o_b, w_ref) in enumerate(branches):
        term = jax.nn.sigmoid(graw[bi]) * _dot(o_b, w_ref[...])
        merged = term if merged is None else merged + term
    out_ref[...] = x + _dot(merged.astype(BF16), wo_ref[...])


def _merge(x, g_mix, w_gate, o_a, o_ds, lses, qm, mk, mv, w_ba, w_bd, w_bm, w_out, S):
    tm = TM_MERGE
    const = lambda shape: pl.BlockSpec(shape, lambda i: (0,) * len(shape), pipeline_mode=pl.Buffered(1))
    rows = lambda w: pl.BlockSpec((tm, w), lambda i: (i, 0))
    dil_rows = [pl.BlockSpec((tm // d, d * DIL_W), lambda i: (i, 0)) for _, d in DIL_PATTERNS]
    return pl.pallas_call(
        _merge_kernel,
        grid=(S // tm,),
        in_specs=[rows(D_MODEL), const((1, D_MODEL)), const((D_MODEL, 3 * D_MODEL)),
                  rows(MOBA_W), *dil_rows, *dil_rows,
                  rows(MEM_W), const((N_MEM, MEM_W)), const((N_MEM, MEM_W)),
                  const((MOBA_W, D_MODEL)), const((DIL_W, D_MODEL)), const((MEM_W, D_MODEL)),
                  const((D_MODEL, D_MODEL))],
        out_specs=rows(D_MODEL),
        out_shape=jax.ShapeDtypeStruct((S, D_MODEL), F32),
        scratch_shapes=[pltpu.VMEM((2 * len(DIL_PATTERNS) * DIL_W // LANES, tm, LANES), F32)],
        compiler_params=_params("parallel"),
        name="gated_merge",
    )(x, g_mix, w_gate, o_a, *o_ds, *lses, qm, mk, mv, w_ba, w_bd, w_bm, w_out)


def _ffn_kernel(x_ref, g_ref, wup_ref, cw_ref, cb_ref, wdn_ref, out_ref, halo_ref, act_ref):
    i = pl.program_id(0)
    tm = x_ref.shape[0]

    @pl.when(i == 0)
    def _():
        halo_ref[...] = jnp.zeros_like(halo_ref)

    x = x_ref[...]
    h = _rms_rows(x, g_ref[...]).astype(BF16)

    row8 = lax.broadcasted_iota(jnp.int32, (8, FFN_CHUNK), 0)

    def conv(u, col):
        prev = halo_ref[:, col:col + FFN_CHUNK]
        halo_ref[:, col:col + FFN_CHUNK] = u[tm - 8:, :]
        p6 = jnp.broadcast_to(prev[6:7], (8, FFN_CHUNK))
        p7 = jnp.broadcast_to(prev[7:8], (8, FFN_CHUNK))
        r1 = pltpu.roll(u, 1, axis=0)
        r2 = pltpu.roll(u, 2, axis=0)
        top1 = jnp.where(row8 == 0, p7, r1[:8])
        top2 = jnp.where(row8 == 0, p6, jnp.where(row8 == 1, p7, r2[:8]))
        u1 = jnp.concatenate([top1, r1[8:]], axis=0)
        u2 = jnp.concatenate([top2, r2[8:]], axis=0)
        w = cw_ref[:, col:col + FFN_CHUNK]
        return cb_ref[:, col:col + FFN_CHUNK] + w[0:1] * u2 + w[1:2] * u1 + w[2:3] * u

    for c in range(D_FF // FFN_CHUNK):
        cg = c * FFN_CHUNK
        cv = D_FF + c * FFN_CHUNK
        u_g = conv(_dot(h, wup_ref[:, cg:cg + FFN_CHUNK]), cg)
        u_v = conv(_dot(h, wup_ref[:, cv:cv + FFN_CHUNK]), cv)
        act_ref[:, cg:cg + FFN_CHUNK] = (jax.nn.silu(u_g) * u_v).astype(BF16)
    out_ref[...] = x + _dot(act_ref[...], wdn_ref[...])


def _ffn(x, g_ffn, w_up, conv_w, conv_b, w_down, S):
    tm = TM_FFN
    const = lambda shape: pl.BlockSpec(shape, lambda i: (0,) * len(shape), pipeline_mode=pl.Buffered(1))
    rows = pl.BlockSpec((tm, D_MODEL), lambda i: (i, 0))
    return pl.pallas_call(
        _ffn_kernel,
        grid=(S // tm,),
        in_specs=[rows, const((1, D_MODEL)), const((D_MODEL, 2 * D_FF)),
                  const((CONV_WIDTH, 2 * D_FF)), const((1, 2 * D_FF)), const((D_FF, D_MODEL))],
        out_specs=rows,
        out_shape=jax.ShapeDtypeStruct((S, D_MODEL), F32),
        scratch_shapes=[pltpu.VMEM((8, 2 * D_FF), F32), pltpu.VMEM((tm, D_FF), BF16)],
        compiler_params=_params("arbitrary"),
        name="conv_ffn",
    )(x, g_ffn, w_up, conv_w, conv_b, w_down)


def _group_mean_matrix():
    g = np.arange(MXU_TILE) // HEAD_DIM
    return jnp.asarray((g[:, None] == g[None, :]).astype(np.float32) / HEAD_DIM, dtype=BF16)


def _layer(x, mem, positions, p):
    S = x.shape[0]
    assert S % (max(d for _, d in DIL_PATTERNS) * DIL_BLOCK) == 0 and S % TM_PROJ == 0
    row = lambda v: v.reshape(1, -1).astype(F32)
    c = np.cumsum([0, MOBA_W, MOBA_W, MOBA_W, DIL_W, DIL_W, DIL_W, MEM_W])
    w_in = p["w_in"]
    seg = lambda k: w_in[:, c[k]:c[k + 1]]
    wt = jnp.concatenate([seg(0), seg(2)], axis=1).T.astype(BF16)
    wn = jnp.concatenate([seg(1), seg(3), seg(4), seg(6), seg(5)], axis=1).astype(BF16)
    w_gate = w_in[:, QKV_COLS:].astype(BF16)
    gnat = jnp.concatenate([jnp.tile(p["moba_k_norm_g"], N_MOBA_HEADS), jnp.tile(p["dil_q_norm_g"], N_DIL_HEADS),
                            jnp.tile(p["dil_k_norm_g"], N_DIL_HEADS), jnp.tile(p["mem_q_norm_g"], N_MEM_HEADS),
                            jnp.ones((_NAT_NORM_PAD - _NAT_NORM,), F32)]).reshape(1, _NAT_NORM_PAD)
    gmat = _group_mean_matrix()

    cos_t, sin_t, cos_n, sin_n = _rope_tables(positions, S)
    mk, mv = _mem_kv(mem, row(p["mem_norm_g"]), p["w_mem_kv"].astype(BF16),
                     row(jnp.tile(p["mem_k_norm_g"], N_MEM_HEADS)), gmat)
    (qat, vat, ka, kmean, qm), dil_qkv = _proj(
        x, row(p["mix_norm_g"]), wt, wn, p["moba_q_norm_g"].reshape(HEAD_DIM, 1).astype(F32), gnat, gmat,
        cos_t, sin_t, cos_n, sin_n, S)
    def score_bound(gq, gk):
        return (1.02 * HEAD_DIM * SCALE * jnp.max(jnp.abs(gq)) * jnp.max(jnp.abs(gk))).reshape(1).astype(F32)

    o_a = _moba(score_bound(p["moba_q_norm_g"], p["moba_k_norm_g"]), qat, ka, vat,
                kmean.reshape(S // MOBA_BLOCK, MOBA_W), S)
    bound_d = score_bound(p["dil_q_norm_g"], p["dil_k_norm_g"])
    o_ds, lses = zip(*[_dilated(bound_d, *qkv, w, d, S) for qkv, (w, d) in zip(dil_qkv, DIL_PATTERNS)])
    x1 = _merge(x, row(p["mix_norm_g"]), w_gate, o_a, o_ds, lses, qm, mk, mv,
                p["w_branch_moba"].astype(BF16), p["w_branch_dil"].astype(BF16),
                p["w_branch_mem"].astype(BF16), p["w_out"].astype(BF16), S)
    return _ffn(x1, row(p["ffn_norm_g"]), p["w_ffn_up"].astype(BF16), p["ffn_conv_w"].astype(F32),
                row(p["ffn_conv_b"]), p["w_ffn_down"].astype(BF16), S)


def kernel(x, mem, positions, mix_norm_g, mem_norm_g, w_in, moba_q_norm_g, moba_k_norm_g, dil_q_norm_g, dil_k_norm_g, mem_q_norm_g, mem_k_norm_g, w_mem_kv, w_branch_moba, w_branch_dil, w_branch_mem, w_out, ffn_norm_g, w_ffn_up, ffn_conv_w, ffn_conv_b, w_ffn_down):
    params = dict(mix_norm_g=mix_norm_g, mem_norm_g=mem_norm_g, w_in=w_in, moba_q_norm_g=moba_q_norm_g,
                  moba_k_norm_g=moba_k_norm_g, dil_q_norm_g=dil_q_norm_g, dil_k_norm_g=dil_k_norm_g,
                  mem_q_norm_g=mem_q_norm_g, mem_k_norm_g=mem_k_norm_g, w_mem_kv=w_mem_kv,
                  w_branch_moba=w_branch_moba, w_branch_dil=w_branch_dil, w_branch_mem=w_branch_mem,
                  w_out=w_out, ffn_norm_g=ffn_norm_g, w_ffn_up=w_ffn_up, ffn_conv_w=ffn_conv_w,
                  ffn_conv_b=ffn_conv_b, w_ffn_down=w_ffn_down)
    B = x.shape[0]
    depth = w_in.shape[0]
    outs = []
    for b in range(B):
        xb = x.reshape(x.shape[1:]) if B == 1 else x[b]
        for l in range(depth):
            xb = _layer(xb, mem[b], positions[b], {k: v[l] for k, v in params.items()})
        outs.append(xb)
    return outs[0].reshape(x.shape) if B == 1 else jnp.stack(outs, axis=0)
```

```python
import numpy as np
import jax
import jax.numpy as jnp
from jax import lax
from jax.experimental import pallas as pl
from jax.experimental.pallas import tpu as pltpu

D_MODEL = 1024
HEAD_DIM = 64
HALF = HEAD_DIM // 2
N_MOBA_HEADS = 6
N_DIL_HEADS = 6
N_MEM_HEADS = 4
N_MEM = 256
MOBA_BLOCK = 256
MOBA_TOPK = 3
DIL_PATTERNS = ((128, 1), (512, 4), (2048, 16))
DIL_BLOCK = 128
D_FF = 2816
CONV_WIDTH = 3
ROPE_THETA = 10000.0
EPS = 1e-6
MOBA_W = N_MOBA_HEADS * HEAD_DIM
DIL_W = N_DIL_HEADS * HEAD_DIM
MEM_W = N_MEM_HEADS * HEAD_DIM
QKV_COLS = 3 * MOBA_W + 3 * DIL_W + MEM_W
SCALE = HEAD_DIM ** -0.5

LANES = 128
MXU_TILE = 256
VMEM_LIMIT = 56 * 1024 * 1024

TM_ROPE = 2048
TM_PROJ = 512
TM_MERGE = 512
TM_FFN = 512
FFN_CHUNK = 256
DIL_TILE = 2048

F32 = jnp.float32
BF16 = jnp.bfloat16
NEG_INF = float("-inf")


def _dot(a, b):
    return jnp.dot(a, b, preferred_element_type=F32)


def _dot_nt(a, b):
    return lax.dot_general(a, b, (((1,), (1,)), ((), ())), preferred_element_type=F32)


def _rms_rows(x, g):
    ms = jnp.mean(x * x, axis=-1, keepdims=True)
    return x * lax.rsqrt(ms + EPS) * g


def _params(*sem):
    return pltpu.CompilerParams(dimension_semantics=sem, vmem_limit_bytes=VMEM_LIMIT)


def _rope_kernel(pos_ref, inv_ref, cos_t_ref, sin_t_ref, cos_n_ref, sin_n_ref):
    pos = pos_ref[...].astype(F32)
    ang = inv_ref[...] * pos
    c = jnp.cos(ang)
    s = jnp.sin(ang)
    cos_t_ref[...] = jnp.concatenate([c, c], axis=0)
    sin_t_ref[...] = jnp.concatenate([-s, s], axis=0)
    cos_n_ref[...] = jnp.concatenate([c, c, c, c], axis=0).T
    sin_n_ref[...] = jnp.concatenate([-s, s, -s, s], axis=0).T


def _rope_tables(positions, S):
    tm = min(TM_ROPE, S)
    inv = (ROPE_THETA ** (-jnp.arange(HALF, dtype=F32) / HALF)).reshape(HALF, 1)
    return pl.pallas_call(
        _rope_kernel,
        grid=(S // tm,),
        in_specs=[pl.BlockSpec((1, tm), lambda i: (0, i)),
                  pl.BlockSpec((HALF, 1), lambda i: (0, 0))],
        out_specs=[pl.BlockSpec((HEAD_DIM, tm), lambda i: (0, i)),
                   pl.BlockSpec((HEAD_DIM, tm), lambda i: (0, i)),
                   pl.BlockSpec((tm, LANES), lambda i: (i, 0)),
                   pl.BlockSpec((tm, LANES), lambda i: (i, 0))],
        out_shape=[jax.ShapeDtypeStruct((HEAD_DIM, S), F32),
                   jax.ShapeDtypeStruct((HEAD_DIM, S), F32),
                   jax.ShapeDtypeStruct((S, LANES), F32),
                   jax.ShapeDtypeStruct((S, LANES), F32)],
        compiler_params=_params("parallel"),
        name="rope_tables",
    )(positions.reshape(1, S), inv)


def _memkv_kernel(mem_ref, g_ref, w_ref, gk_ref, gmat_ref, mk_ref, mv_ref):
    h = _rms_rows(mem_ref[...], g_ref[...]).astype(BF16)
    kv = _dot(h, w_ref[...])
    k = kv[:, :MEM_W]
    ms = _dot((k * k).astype(BF16), gmat_ref[...])
    mk_ref[...] = (k * lax.rsqrt(ms + EPS) * gk_ref[...]).astype(BF16)
    mv_ref[...] = kv[:, MEM_W:].astype(BF16)


def _mem_kv(mem, g, w_kv, gk_tiled, gmat):
    full = lambda shape: pl.BlockSpec(shape, lambda i: (0,) * len(shape))
    return pl.pallas_call(
        _memkv_kernel,
        grid=(1,),
        in_specs=[full((N_MEM, D_MODEL)), full((1, D_MODEL)), full((D_MODEL, 2 * MEM_W)),
                  full((1, MEM_W)), full((MXU_TILE, MXU_TILE))],
        out_specs=[full((N_MEM, MEM_W)), full((N_MEM, MEM_W))],
        out_shape=[jax.ShapeDtypeStruct((N_MEM, MEM_W), BF16)] * 2,
        compiler_params=_params("arbitrary"),
        name="mem_kv",
    )(mem, g, w_kv, gk_tiled, gmat)


_NAT_NORM = 3 * 384 + MEM_W
_NAT_NORM_PAD = 1536
_NAT_ROPE = 3 * 384
_NAT_COLS = _NAT_NORM + DIL_W


def _proj_kernel(x_ref, g_ref, wt_ref, wn_ref, gq_ref, gnat_ref, gmat_ref,
                 cos_t_ref, sin_t_ref, cos_n_ref, sin_n_ref,
                 qat_ref, vat_ref, ka_ref, km_ref, qm_ref, qd_ref, kd_ref, vd_ref):
    tm = x_ref.shape[0]
    h = _rms_rows(x_ref[...], g_ref[...]).astype(BF16)

    yt = _dot_nt(wt_ref[...], h)
    cos_t = cos_t_ref[...]
    sin_t = sin_t_ref[...]
    gq = gq_ref[...]
    for hd in range(N_MOBA_HEADS):
        q = yt[hd * HEAD_DIM:(hd + 1) * HEAD_DIM]
        ms = jnp.mean(q * q, axis=0, keepdims=True)
        q = q * lax.rsqrt(ms + EPS) * gq
        rot = jnp.concatenate([q[HALF:], q[:HALF]], axis=0)
        q = (q * cos_t + rot * sin_t) * SCALE
        for b in range(tm // MOBA_BLOCK):
            qat_ref[b, hd * HEAD_DIM:(hd + 1) * HEAD_DIM, :] = (
                q[:, b * MOBA_BLOCK:(b + 1) * MOBA_BLOCK].astype(BF16))
    for b in range(tm // MOBA_BLOCK):
        vat_ref[b] = yt[MOBA_W:, b * MOBA_BLOCK:(b + 1) * MOBA_BLOCK].astype(BF16)

    gmat = gmat_ref[...]
    lane = lax.broadcasted_iota(jnp.int32, (tm, LANES), 1)
    first_half = (lane % HEAD_DIM) < HALF
    cos_n = cos_n_ref[...]
    sin_n = sin_n_ref[...]
    pairs = DIL_W // LANES

    def emit_dilated(ref, tiles):
        for lt in range(pairs):
            ref[:, lt * LANES:(lt + 1) * LANES] = tiles[lt]

    chunks = []
    for c in range(_NAT_NORM_PAD // MXU_TILE):
        if c % 2 == 0:
            wide = _dot(h, wn_ref[:, c * MXU_TILE:(c + 2) * MXU_TILE])
        raw = wide[:, (c % 2) * MXU_TILE:(c % 2 + 1) * MXU_TILE]
        ms = _dot((raw * raw).astype(BF16), gmat)
        blk = raw * lax.rsqrt(ms + EPS) * gnat_ref[:, c * MXU_TILE:(c + 1) * MXU_TILE]
        for hc in range(MXU_TILE // LANES):
            col = c * MXU_TILE + hc * LANES
            if col >= _NAT_NORM:
                chunks.append(raw[:, hc * LANES:(hc + 1) * LANES])
                continue
            v = blk[:, hc * LANES:(hc + 1) * LANES]
            if col < _NAT_ROPE:
                rot = jnp.where(first_half, pltpu.roll(v, LANES - HALF, axis=1),
                                pltpu.roll(v, HALF, axis=1))
                v = v * cos_n + rot * sin_n
            chunks.append(v)
        if len(chunks) >= pairs and c == 1:
            ka = jnp.concatenate(chunks[0:pairs], axis=1)
            for b in range(tm // MOBA_BLOCK):
                km_ref[b] = jnp.mean(ka[b * MOBA_BLOCK:(b + 1) * MOBA_BLOCK], axis=0, keepdims=True)
            ka_ref[...] = ka.astype(BF16)
        if c == 2:
            emit_dilated(qd_ref, [t * SCALE for t in chunks[pairs:2 * pairs]])
        if c == 4:
            emit_dilated(kd_ref, chunks[2 * pairs:3 * pairs])
    qm_ref[...] = (jnp.concatenate(chunks[3 * pairs:3 * pairs + MEM_W // LANES], axis=1) * SCALE).astype(BF16)
    vd_rest = _dot(h, wn_ref[:, _NAT_NORM_PAD:])
    emit_dilated(vd_ref, [chunks[-1]] + [vd_rest[:, t * LANES:(t + 1) * LANES] for t in range(pairs - 1)])


def _proj(x, g_mix, wt, wn, gq_col, gnat, gmat, cos_t, sin_t, cos_n, sin_n, S):
    tm = TM_PROJ
    nb = S // MOBA_BLOCK
    bpt = tm // MOBA_BLOCK
    const = lambda shape: pl.BlockSpec(shape, lambda i: (0,) * len(shape))
    rows = lambda w: pl.BlockSpec((tm, w), lambda i: (i, 0))
    outs = pl.pallas_call(
        _proj_kernel,
        grid=(S // tm,),
        in_specs=[rows(D_MODEL), const((1, D_MODEL)), const((2 * MOBA_W, D_MODEL)),
                  const((D_MODEL, _NAT_COLS)), const((HEAD_DIM, 1)), const((1, _NAT_NORM_PAD)),
                  const((MXU_TILE, MXU_TILE)),
                  pl.BlockSpec((HEAD_DIM, tm), lambda i: (0, i)),
                  pl.BlockSpec((HEAD_DIM, tm), lambda i: (0, i)),
                  rows(LANES), rows(LANES)],
        out_specs=[pl.BlockSpec((bpt, MOBA_W, MOBA_BLOCK), lambda i: (i, 0, 0)),
                   pl.BlockSpec((bpt, MOBA_W, MOBA_BLOCK), lambda i: (i, 0, 0)),
                   rows(MOBA_W),
                   pl.BlockSpec((bpt, 1, MOBA_W), lambda i: (i, 0, 0)),
                   rows(MEM_W), rows(DIL_W), rows(DIL_W), rows(DIL_W)],
        out_shape=[jax.ShapeDtypeStruct((nb, MOBA_W, MOBA_BLOCK), BF16),
                   jax.ShapeDtypeStruct((nb, MOBA_W, MOBA_BLOCK), BF16),
                   jax.ShapeDtypeStruct((S, MOBA_W), BF16),
                   jax.ShapeDtypeStruct((nb, 1, MOBA_W), F32),
                   jax.ShapeDtypeStruct((S, MEM_W), BF16)]
                  + [jax.ShapeDtypeStruct((S, DIL_W), F32)] * 3,
        compiler_params=_params("parallel"),
        name="in_proj",
    )(x, g_mix, wt, wn, gq_col, gnat, gmat, cos_t, sin_t, cos_n, sin_n)
    return outs[:5], outs[5:]


_SHIFT_HEADROOM = 30.0
_FAST_BOUND = 50.0
_MOBA_UNROLL = 4
_MOBA_TRIP_GROUPS = (1, 2, 4)


def _moba_kernel(bound_ref, qt_ref, k_ref, vt_ref, km_ref, o_ref, bias_ref, ot_ref, acc_ref):
    i = pl.program_id(0)
    tq = MOBA_BLOCK
    unroll = _MOBA_UNROLL
    heads = N_MOBA_HEADS
    nb = km_ref.shape[0]
    row = lax.broadcasted_iota(jnp.int32, (LANES, tq), 0)
    blk = lax.broadcasted_iota(jnp.int32, (nb, tq), 0)

    bound = bound_ref[0]
    fast = bound <= _FAST_BOUND
    sel_bias = jnp.where(fast, _SHIFT_HEADROOM - bound, 0.0)

    def pair_cols(hd):
        return slice((hd // 2) * LANES, (hd // 2 + 1) * LANES)

    qs = []
    for hd in range(heads):
        qt = qt_ref[0, pair_cols(hd), :]
        q_h = jnp.where((row // HEAD_DIM) == hd % 2, qt, jnp.zeros_like(qt))
        qs.append(q_h)
        km = km_ref[:, pair_cols(hd)]
        km1 = km.astype(BF16)
        r1 = km - km1.astype(F32)
        km2 = r1.astype(BF16)
        km3 = (r1 - km2.astype(F32)).astype(BF16)
        g3 = _dot(jnp.concatenate([km1, km2, km3], axis=0), q_h)
        gate = g3[:nb] + g3[nb:2 * nb] + g3[2 * nb:]
        gate = jnp.where(blk < i, gate, NEG_INF)
        bias = jnp.full((nb, tq), NEG_INF, F32)
        for _ in range(MOBA_TOPK):
            m = jnp.max(gate, axis=0, keepdims=True)
            idx = jnp.min(jnp.where(gate == m, blk, nb), axis=0, keepdims=True)
            hit = blk == idx
            bias = jnp.where(hit & (m > NEG_INF), sel_bias, bias)
            gate = jnp.where(hit, NEG_INF, gate)
        bias_ref[hd] = bias

    kpos = lax.broadcasted_iota(jnp.int32, (tq, tq), 0)
    qpos = lax.broadcasted_iota(jnp.int32, (tq, tq), 1)
    causal = kpos <= qpos

    def k_rows(j, hd):
        start = j * tq if isinstance(j, int) else pl.multiple_of(j * tq, tq)
        return k_ref[pl.ds(start, tq), pair_cols(hd)]

    def v_rows(j, hd):
        return vt_ref[j, hd * HEAD_DIM:(hd + 1) * HEAD_DIM, :]

    ones = jnp.ones((16, tq), BF16)

    def values(j, hd, p):
        return _dot(jnp.concatenate([v_rows(j, hd), ones], axis=0), p)

    own_s = [_dot(k_rows(i, hd), qs[hd]) for hd in range(heads)]
    own_p = [jnp.exp(jnp.where(causal, s + sel_bias, NEG_INF)).astype(BF16) for s in own_s]

    @pl.when(fast)
    def _():
        def attend(accs, blocks):
            out = list(accs)
            chains = [(j, hd) for j in blocks for hd in range(heads)]
            ss = [_dot(k_rows(j, hd), qs[hd]) for j, hd in chains]
            ps = [jnp.exp(s + bias_ref[hd, pl.ds(j, 1), :]).astype(BF16) for s, (j, hd) in zip(ss, chains)]
            for p, (j, hd) in zip(ps, chains):
                out[hd] = out[hd] + values(j, hd, p)
            return out

        rem = i % unroll
        for r in range(unroll):
            @pl.when(rem == r)
            def _():
                own = [values(i, hd, p) for hd, p in enumerate(own_p)]
                for hd, a in enumerate(attend(own, range(r))):
                    acc_ref[hd] = a

        groups = i // unroll
        accs = tuple(acc_ref[hd] for hd in range(heads))
        start = rem
        for span in _MOBA_TRIP_GROUPS:
            last = span == _MOBA_TRIP_GROUPS[-1]
            trips = groups // span if last else (groups // span) % 2

            def body(t, accs, start=start, span=span):
                first = start + span * unroll * t
                return tuple(attend(accs, [first + u for u in range(span * unroll)]))

            accs = lax.fori_loop(0, trips, body, accs)
            start = start + span * unroll * trips
        for hd, a in enumerate(accs):
            ot_ref[hd * HEAD_DIM:(hd + 1) * HEAD_DIM, :] = a[:HEAD_DIM] / a[HEAD_DIM:HEAD_DIM + 1]

    @pl.when(jnp.logical_not(fast))
    def _():
        for pr in range(heads // 2):
            pair = (2 * pr, 2 * pr + 1)
            carry = []
            for hd in pair:
                s = jnp.where(causal, _dot(k_rows(i, hd), qs[hd]), NEG_INF)
                m = jnp.max(s, axis=0, keepdims=True)
                p = jnp.exp(s - m)
                l = jnp.sum(p, axis=0, keepdims=True)
                carry += [m, l, _dot(v_rows(i, hd), p.astype(BF16))]

            def body(j, carry):
                out = []
                for c, hd in enumerate(pair):
                    m, l, acc = carry[3 * c:3 * c + 3]
                    s = _dot(k_rows(j, hd), qs[hd]) + bias_ref[hd, pl.ds(j, 1), :]
                    m_new = jnp.maximum(m, jnp.max(s, axis=0, keepdims=True))
                    alpha = jnp.exp(m - m_new)
                    p = jnp.exp(s - m_new)
                    l = alpha * l + jnp.sum(p, axis=0, keepdims=True)
                    acc = alpha * acc + _dot(v_rows(j, hd), p.astype(BF16))
                    out += [m_new, l, acc]
                return tuple(out)

            carry = lax.fori_loop(0, i, body, tuple(carry))
            for c, hd in enumerate(pair):
                ot_ref[hd * HEAD_DIM:(hd + 1) * HEAD_DIM, :] = carry[3 * c + 2] / carry[3 * c + 1]

    o_ref[...] = ot_ref[...].T.astype(BF16)


def _moba(bound, qat, ka, vat, kmean, S):
    nb = S // MOBA_BLOCK
    once = lambda shape: pl.BlockSpec(shape, lambda i: (0,) * len(shape), pipeline_mode=pl.Buffered(1))
    return pl.pallas_call(
        _moba_kernel,
        grid=(nb,),
        in_specs=[pl.BlockSpec(memory_space=pltpu.SMEM),
                  pl.BlockSpec((1, MOBA_W, MOBA_BLOCK), lambda i: (i, 0, 0)),
                  once((S, MOBA_W)), once((nb, MOBA_W, MOBA_BLOCK)), once((nb, MOBA_W))],
        out_specs=pl.BlockSpec((MOBA_BLOCK, MOBA_W), lambda i: (i, 0)),
        out_shape=jax.ShapeDtypeStruct((S, MOBA_W), BF16),
        scratch_shapes=[pltpu.VMEM((N_MOBA_HEADS, nb, MOBA_BLOCK), F32),
                        pltpu.VMEM((MOBA_W, MOBA_BLOCK), F32),
                        pltpu.VMEM((N_MOBA_HEADS, HEAD_DIM + 16, MOBA_BLOCK), F32)],
        compiler_params=_params("arbitrary"),
        name="moba_attn",
    )(bound, qat, ka, vat, kmean)


def _dil_kernel(bound_ref, q_ref, kp_ref, kc_ref, vp_ref, vc_ref, o_ref, num_ref, den_ref, m_ref):
    g = pl.program_id(1)
    blk = DIL_BLOCK
    tile = q_ref.shape[0]
    lane = lax.broadcasted_iota(jnp.int32, (blk, LANES), 1)
    lo_half = lane < HEAD_DIM
    qi = lax.broadcasted_iota(jnp.int32, (2 * blk, 2 * blk), 0) % blk + blk
    kj = lax.broadcasted_iota(jnp.int32, (2 * blk, 2 * blk), 1)
    dist = qi - kj
    bound = bound_ref[0]
    fast = bound <= _FAST_BOUND
    shift = bound - _SHIFT_HEADROOM

    def rows(ref, start, n, d):
        return ref[pl.ds(start, n, stride=d), :].astype(BF16)

    def patterns(fixed_shift):
        for pi, (window, d) in enumerate(DIL_PATTERNS):
            span = window // d
            band = jnp.where((dist >= 0) & (dist <= span), 0.0, NEG_INF)
            band_first = jnp.where(kj >= blk, band, NEG_INF)
            sub = shift if fixed_shift else 0.0
            mask_first = jnp.where(g > 0, band, band_first) - sub
            mask_rest = band - sub
            nsub = tile // (blk * d)
            for r in range(d):
                for b in range(nsub):
                    start = r + b * blk * d
                    q = rows(q_ref, start, blk, d)
                    if b == 0:
                        last = r + (nsub - 1) * blk * d
                        kk = jnp.concatenate([rows(kp_ref, last, blk, d), rows(kc_ref, r, blk, d)], axis=0)
                        vv = jnp.concatenate([rows(vp_ref, last, blk, d), rows(vc_ref, r, blk, d)], axis=0)
                        mask = mask_first
                    else:
                        kk = rows(kc_ref, start - blk * d, 2 * blk, d)
                        vv = rows(vc_ref, start - blk * d, 2 * blk, d)
                        mask = mask_rest
                    zero = jnp.zeros_like(q)
                    q2 = jnp.concatenate([jnp.where(lo_half, q, zero), jnp.where(lo_half, zero, q)], axis=0)
                    s = _dot_nt(q2, kk) + mask
                    here = pl.ds(start, blk, stride=d)
                    if fixed_shift:
                        ov = _dot(jnp.exp(s).astype(BF16), jnp.concatenate([vv, jnp.ones_like(vv)], axis=1))
                        num = jnp.where(lo_half, ov[:blk, :LANES], ov[blk:, :LANES])
                        den = jnp.where(lo_half, ov[:blk, LANES:], ov[blk:, LANES:])
                        if pi == 0:
                            num_ref[here, :] = num
                            den_ref[here, :] = den
                        else:
                            num_ref[here, :] += num
                            den_ref[here, :] += den
                    else:
                        m2 = jnp.max(s, axis=-1, keepdims=True)
                        p = jnp.exp(s - m2)
                        den2 = jnp.broadcast_to(jnp.sum(p, axis=-1, keepdims=True), (2 * blk, LANES))
                        num2 = _dot(p.astype(BF16), vv)
                        m2 = jnp.broadcast_to(m2, (2 * blk, LANES))
                        num = jnp.where(lo_half, num2[:blk], num2[blk:])
                        den = jnp.where(lo_half, den2[:blk], den2[blk:])
                        m = jnp.where(lo_half, m2[:blk], m2[blk:])
                        if pi == 0:
                            num_ref[here, :] = num
                            den_ref[here, :] = den
                            m_ref[here, :] = m
                        else:
                            m_old = m_ref[here, :]
                            m_new = jnp.maximum(m_old, m)
                            a_old = jnp.exp(m_old - m_new)
                            a_new = jnp.exp(m - m_new)
                            num_ref[here, :] = num_ref[here, :] * a_old + num * a_new
                            den_ref[here, :] = den_ref[here, :] * a_old + den * a_new
                            m_ref[here, :] = m_new
        o_ref[...] = (num_ref[...] / den_ref[...]).astype(BF16)

    pl.when(fast)(lambda: patterns(True))
    pl.when(jnp.logical_not(fast))(lambda: patterns(False))


def _dilated(bound, qd, kd, vd, S):
    tile = DIL_TILE
    pairs = DIL_W // LANES
    cur = pl.BlockSpec((tile, LANES), lambda p, g: (g, p))
    prev = pl.BlockSpec((tile, LANES), lambda p, g: (jnp.maximum(g - 1, 0), p))
    return pl.pallas_call(
        _dil_kernel,
        grid=(pairs, S // tile),
        in_specs=[pl.BlockSpec(memory_space=pltpu.SMEM), cur, prev, cur, prev, cur],
        out_specs=cur,
        out_shape=jax.ShapeDtypeStruct((S, DIL_W), BF16),
        scratch_shapes=[pltpu.VMEM((tile, LANES), F32)] * 3,
        compiler_params=_params("parallel", "arbitrary"),
        name="dilated_attn",
    )(bound, qd, kd, kd, vd, vd)


def _merge_kernel(x_ref, g_ref, wg_ref, oa_ref, od_ref, qm_ref, mk_ref, mv_ref,
                  wba_ref, wbd_ref, wbm_ref, wo_ref, out_ref):
    x = x_ref[...]
    tm = x.shape[0]
    h = _rms_rows(x, g_ref[...]).astype(BF16)

    lane = lax.broadcasted_iota(jnp.int32, (tm, LANES), 1)
    lo_half = lane < HEAD_DIM
    mem_pairs = MEM_W // LANES
    scores = []
    for pr in range(mem_pairs):
        q = qm_ref[:, pr * LANES:(pr + 1) * LANES]
        zero = jnp.zeros_like(q)
        q2 = jnp.concatenate([jnp.where(lo_half, q, zero), jnp.where(lo_half, zero, q)], axis=0)
        scores.append(_dot_nt(q2, mk_ref[:, pr * LANES:(pr + 1) * LANES]))
    graw = [_dot(h, wg_ref[:, bi * D_MODEL:(bi + 1) * D_MODEL]) for bi in range(3)]

    probs = []
    for s in scores:
        p = jnp.exp(s - jnp.max(s, axis=-1, keepdims=True))
        probs.append((p / jnp.sum(p, axis=-1, keepdims=True)).astype(BF16))
    o_m = []
    for pr in range(mem_pairs):
        o2 = _dot(probs[pr], mv_ref[:, pr * LANES:(pr + 1) * LANES])
        o_m.append(jnp.where(lo_half, o2[:tm], o2[tm:]))
    o_m = jnp.concatenate(o_m, axis=1)

    merged = None
    branches = ((oa_ref[...], wba_ref), (od_ref[...], wbd_ref), (o_m.astype(BF16), wbm_ref))
    for bi, (o_b, w_ref) in enumerate(branches):
        term = jax.nn.sigmoid(graw[bi]) * _dot(o_b, w_ref[...])
        merged = term if merged is None else merged + term
    out_ref[...] = x + _dot(merged.astype(BF16), wo_ref[...])


def _merge(x, g_mix, w_gate, o_a, o_d, qm, mk, mv, w_ba, w_bd, w_bm, w_out, S):
    tm = TM_MERGE
    const = lambda shape: pl.BlockSpec(shape, lambda i: (0,) * len(shape), pipeline_mode=pl.Buffered(1))
    rows = lambda w: pl.BlockSpec((tm, w), lambda i: (i, 0))
    return pl.pallas_call(
        _merge_kernel,
        grid=(S // tm,),
        in_specs=[rows(D_MODEL), const((1, D_MODEL)), const((D_MODEL, 3 * D_MODEL)),
                  rows(MOBA_W), rows(DIL_W),
                  rows(MEM_W), const((N_MEM, MEM_W)), const((N_MEM, MEM_W)),
                  const((MOBA_W, D_MODEL)), const((DIL_W, D_MODEL)), const((MEM_W, D_MODEL)),
                  const((D_MODEL, D_MODEL))],
        out_specs=rows(D_MODEL),
        out_shape=jax.ShapeDtypeStruct((S, D_MODEL), F32),
        compiler_params=_params("parallel"),
        name="gated_merge",
    )(x, g_mix, w_gate, o_a, o_d, qm, mk, mv, w_ba, w_bd, w_bm, w_out)


def _ffn_kernel(x_ref, g_ref, wup_ref, cw_ref, cb_ref, wdn_ref, out_ref, halo_ref, act_ref):
    i = pl.program_id(0)
    tm = x_ref.shape[0]

    @pl.when(i == 0)
    def _():
        halo_ref[...] = jnp.zeros_like(halo_ref)

    x = x_ref[...]
    h = _rms_rows(x, g_ref[...]).astype(BF16)

    row8 = lax.broadcasted_iota(jnp.int32, (8, FFN_CHUNK), 0)

    def conv(u, col):
        prev = halo_ref[:, col:col + FFN_CHUNK]
        halo_ref[:, col:col + FFN_CHUNK] = u[tm - 8:, :]
        p6 = jnp.broadcast_to(prev[6:7], (8, FFN_CHUNK))
        p7 = jnp.broadcast_to(prev[7:8], (8, FFN_CHUNK))
        r1 = pltpu.roll(u, 1, axis=0)
        r2 = pltpu.roll(u, 2, axis=0)
        top1 = jnp.where(row8 == 0, p7, r1[:8])
        top2 = jnp.where(row8 == 0, p6, jnp.where(row8 == 1, p7, r2[:8]))
        u1 = jnp.concatenate([top1, r1[8:]], axis=0)
        u2 = jnp.concatenate([top2, r2[8:]], axis=0)
        w = cw_ref[:, col:col + FFN_CHUNK]
        return cb_ref[:, col:col + FFN_CHUNK] + w[0:1] * u2 + w[1:2] * u1 + w[2:3] * u

    for c in range(D_FF // FFN_CHUNK):
        cg = c * FFN_CHUNK
        cv = D_FF + c * FFN_CHUNK
        u_g = conv(_dot(h, wup_ref[:, cg:cg + FFN_CHUNK]), cg)
        u_v = conv(_dot(h, wup_ref[:, cv:cv + FFN_CHUNK]), cv)
        act_ref[:, cg:cg + FFN_CHUNK] = (jax.nn.silu(u_g) * u_v).astype(BF16)
    out_ref[...] = x + _dot(act_ref[...], wdn_ref[...])


def _ffn(x, g_ffn, w_up, conv_w, conv_b, w_down, S):
    tm = TM_FFN
    const = lambda shape: pl.BlockSpec(shape, lambda i: (0,) * len(shape), pipeline_mode=pl.Buffered(1))
    rows = pl.BlockSpec((tm, D_MODEL), lambda i: (i, 0))
    return pl.pallas_call(
        _ffn_kernel,
        grid=(S // tm,),
        in_specs=[rows, const((1, D_MODEL)), const((D_MODEL, 2 * D_FF)),
                  const((CONV_WIDTH, 2 * D_FF)), const((1, 2 * D_FF)), const((D_FF, D_MODEL))],
        out_specs=rows,
        out_shape=jax.ShapeDtypeStruct((S, D_MODEL), F32),
        scratch_shapes=[pltpu.VMEM((8, 2 * D_FF), F32), pltpu.VMEM((tm, D_FF), BF16)],
        compiler_params=_params("arbitrary"),
        name="conv_ffn",
    )(x, g_ffn, w_up, conv_w, conv_b, w_down)


def _group_mean_matrix():
    g = np.arange(MXU_TILE) // HEAD_DIM
    return jnp.asarray((g[:, None] == g[None, :]).astype(np.float32) / HEAD_DIM, dtype=BF16)


def _layer(x, mem, positions, p):
    S = x.shape[0]
    assert S % (max(d for _, d in DIL_PATTERNS) * DIL_BLOCK) == 0 and S % TM_PROJ == 0
    row = lambda v: v.reshape(1, -1).astype(F32)
    c = np.cumsum([0, MOBA_W, MOBA_W, MOBA_W, DIL_W, DIL_W, DIL_W, MEM_W])
    w_in = p["w_in"]
    seg = lambda k: w_in[:, c[k]:c[k + 1]]
    wt = jnp.concatenate([seg(0), seg(2)], axis=1).T.astype(BF16)
    wn = jnp.concatenate([seg(1), seg(3), seg(4), seg(6), seg(5)], axis=1).astype(BF16)
    w_gate = w_in[:, QKV_COLS:].astype(BF16)
    gnat = jnp.concatenate([jnp.tile(p["moba_k_norm_g"], N_MOBA_HEADS), jnp.tile(p["dil_q_norm_g"], N_DIL_HEADS),
                            jnp.tile(p["dil_k_norm_g"], N_DIL_HEADS), jnp.tile(p["mem_q_norm_g"], N_MEM_HEADS),
                            jnp.ones((_NAT_NORM_PAD - _NAT_NORM,), F32)]).reshape(1, _NAT_NORM_PAD)
    gmat = _group_mean_matrix()

    cos_t, sin_t, cos_n, sin_n = _rope_tables(positions, S)
    mk, mv = _mem_kv(mem, row(p["mem_norm_g"]), p["w_mem_kv"].astype(BF16),
                     row(jnp.tile(p["mem_k_norm_g"], N_MEM_HEADS)), gmat)
    (qat, vat, ka, kmean, qm), dil_qkv = _proj(
        x, row(p["mix_norm_g"]), wt, wn, p["moba_q_norm_g"].reshape(HEAD_DIM, 1).astype(F32), gnat, gmat,
        cos_t, sin_t, cos_n, sin_n, S)
    def score_bound(gq, gk):
        return (1.02 * HEAD_DIM * SCALE * jnp.max(jnp.abs(gq)) * jnp.max(jnp.abs(gk))).reshape(1).astype(F32)

    o_a = _moba(score_bound(p["moba_q_norm_g"], p["moba_k_norm_g"]), qat, ka, vat,
                kmean.reshape(S // MOBA_BLOCK, MOBA_W), S)
    bound_d = score_bound(p["dil_q_norm_g"], p["dil_k_norm_g"])
    o_d = _dilated(bound_d, *dil_qkv, S)
    x1 = _merge(x, row(p["mix_norm_g"]), w_gate, o_a, o_d, qm, mk, mv,
                p["w_branch_moba"].astype(BF16), p["w_branch_dil"].astype(BF16),
                p["w_branch_mem"].astype(BF16), p["w_out"].astype(BF16), S)
    return _ffn(x1, row(p["ffn_norm_g"]), p["w_ffn_up"].astype(BF16), p["ffn_conv_w"].astype(F32),
                row(p["ffn_conv_b"]), p["w_ffn_down"].astype(BF16), S)


def kernel(x, mem, positions, mix_norm_g, mem_norm_g, w_in, moba_q_norm_g, moba_k_norm_g, dil_q_norm_g, dil_k_norm_g, mem_q_norm_g, mem_k_norm_g, w_mem_kv, w_branch_moba, w_branch_dil, w_branch_mem, w_out, ffn_norm_g, w_ffn_up, ffn_conv_w, ffn_conv_b, w_ffn_down):
    params = dict(mix_norm_g=mix_norm_g, mem_norm_g=mem_norm_g, w_in=w_in, moba_q_norm_g=moba_q_norm_g,
                  moba_k_norm_g=moba_k_norm_g, dil_q_norm_g=dil_q_norm_g, dil_k_norm_g=dil_k_norm_g,
                  mem_q_norm_g=mem_q_norm_g, mem_k_norm_g=mem_k_norm_g, w_mem_kv=w_mem_kv,
                  w_branch_moba=w_branch_moba, w_branch_dil=w_branch_dil, w_branch_mem=w_branch_mem,
                  w_out=w_out, ffn_norm_g=ffn_norm_g, w_ffn_up=w_ffn_up, ffn_conv_w=ffn_conv_w,
                  ffn_conv_b=ffn_conv_b, w_ffn_down=w_ffn_down)
    B = x.shape[0]
    depth = w_in.shape[0]
    outs = []
    for b in range(B):
        xb = x.reshape(x.shape[1:]) if B == 1 else x[b]
        for l in range(depth):
            xb = _layer(xb, mem[b], positions[b], {k: v[l] for k, v in params.items()})
        outs.append(xb)
    return outs[0].reshape(x.shape) if B == 1 else jnp.stack(outs, axis=0)
```

```python
import numpy as np
import jax
import jax.numpy as jnp
from jax import lax
from jax.experimental import pallas as pl
from jax.experimental.pallas import tpu as pltpu

D_MODEL = 1024
HEAD_DIM = 64
HALF = HEAD_DIM // 2
N_MOBA_HEADS = 6
N_DIL_HEADS = 6
N_MEM_HEADS = 4
N_MEM = 256
MOBA_BLOCK = 256
MOBA_TOPK = 3
DIL_PATTERNS = ((128, 1), (512, 4), (2048, 16))
DIL_BLOCK = 128
D_FF = 2816
CONV_WIDTH = 3
ROPE_THETA = 10000.0
EPS = 1e-6
MOBA_W = N_MOBA_HEADS * HEAD_DIM
DIL_W = N_DIL_HEADS * HEAD_DIM
MEM_W = N_MEM_HEADS * HEAD_DIM
QKV_COLS = 3 * MOBA_W + 3 * DIL_W + MEM_W
SCALE = HEAD_DIM ** -0.5

LANES = 128
SUBLANES = 8
BF16_ROWS = 16
MXU_TILE = 256
VMEM_LIMIT = 56 * 1024 * 1024

TM_ROPE = 2048
TM_PROJ = 512
TM_MERGE = 512
TM_FFN = 512
FFN_CHUNK = 256
DIL_TILE = 2048

F32 = jnp.float32
BF16 = jnp.bfloat16
NEG_INF = float("-inf")


def _dot(a, b):
    return jnp.dot(a, b, preferred_element_type=F32)


def _dot_nt(a, b):
    return lax.dot_general(a, b, (((1,), (1,)), ((), ())), preferred_element_type=F32)


def _rms_rows(x, g):
    ms = jnp.mean(x * x, axis=-1, keepdims=True)
    return x * lax.rsqrt(ms + EPS) * g


def _params(*sem):
    return pltpu.CompilerParams(dimension_semantics=sem, vmem_limit_bytes=VMEM_LIMIT)


def _rope_kernel(pos_ref, inv_ref, cos_t_ref, sin_t_ref, cos_n_ref, sin_n_ref):
    pos = pos_ref[...].astype(F32)
    ang = inv_ref[...] * pos
    c = jnp.cos(ang)
    s = jnp.sin(ang)
    cos_t_ref[...] = jnp.concatenate([c, c], axis=0)
    sin_t_ref[...] = jnp.concatenate([-s, s], axis=0)
    cos_n_ref[...] = jnp.concatenate([c, c, c, c], axis=0).T
    sin_n_ref[...] = jnp.concatenate([-s, s, -s, s], axis=0).T


def _rope_tables(positions, S):
    tm = min(TM_ROPE, S)
    inv = (ROPE_THETA ** (-jnp.arange(HALF, dtype=F32) / HALF)).reshape(HALF, 1)
    return pl.pallas_call(
        _rope_kernel,
        grid=(S // tm,),
        in_specs=[pl.BlockSpec((1, tm), lambda i: (0, i)),
                  pl.BlockSpec((HALF, 1), lambda i: (0, 0))],
        out_specs=[pl.BlockSpec((HEAD_DIM, tm), lambda i: (0, i)),
                   pl.BlockSpec((HEAD_DIM, tm), lambda i: (0, i)),
                   pl.BlockSpec((tm, LANES), lambda i: (i, 0)),
                   pl.BlockSpec((tm, LANES), lambda i: (i, 0))],
        out_shape=[jax.ShapeDtypeStruct((HEAD_DIM, S), F32),
                   jax.ShapeDtypeStruct((HEAD_DIM, S), F32),
                   jax.ShapeDtypeStruct((S, LANES), F32),
                   jax.ShapeDtypeStruct((S, LANES), F32)],
        compiler_params=_params("parallel"),
        name="rope_tables",
    )(positions.reshape(1, S), inv)


def _memkv_kernel(mem_ref, g_ref, w_ref, gk_ref, gmat_ref, mk_ref, mv_ref):
    h = _rms_rows(mem_ref[...], g_ref[...]).astype(BF16)
    kv = _dot(h, w_ref[...])
    k = kv[:, :MEM_W]
    ms = _dot((k * k).astype(BF16), gmat_ref[...])
    mk_ref[...] = (k * lax.rsqrt(ms + EPS) * gk_ref[...]).astype(BF16)
    mv_ref[...] = kv[:, MEM_W:].astype(BF16)


def _mem_kv(mem, g, w_kv, gk_tiled, gmat):
    full = lambda shape: pl.BlockSpec(shape, lambda i: (0,) * len(shape))
    return pl.pallas_call(
        _memkv_kernel,
        grid=(1,),
        in_specs=[full((N_MEM, D_MODEL)), full((1, D_MODEL)), full((D_MODEL, 2 * MEM_W)),
                  full((1, MEM_W)), full((MXU_TILE, MXU_TILE))],
        out_specs=[full((N_MEM, MEM_W)), full((N_MEM, MEM_W))],
        out_shape=[jax.ShapeDtypeStruct((N_MEM, MEM_W), BF16)] * 2,
        compiler_params=_params("arbitrary"),
        name="mem_kv",
    )(mem, g, w_kv, gk_tiled, gmat)


_NAT_NORM = 3 * 384 + MEM_W
_NAT_NORM_PAD = 1536
_NAT_ROPE = 3 * 384
_NAT_COLS = _NAT_NORM + DIL_W


def _proj_kernel(x_ref, g_ref, wt_ref, wn_ref, gq_ref, gnat_ref, gmat_ref,
                 cos_t_ref, sin_t_ref, cos_n_ref, sin_n_ref,
                 qat_ref, vat_ref, ka_ref, km_ref, qm_ref, qd_ref, kd_ref, vd_ref):
    tm = x_ref.shape[0]
    h = _rms_rows(x_ref[...], g_ref[...]).astype(BF16)

    yt = _dot_nt(wt_ref[...], h)
    cos_t = cos_t_ref[...]
    sin_t = sin_t_ref[...]
    gq = gq_ref[...]
    for hd in range(N_MOBA_HEADS):
        q = yt[hd * HEAD_DIM:(hd + 1) * HEAD_DIM]
        ms = jnp.mean(q * q, axis=0, keepdims=True)
        q = q * lax.rsqrt(ms + EPS) * gq
        rot = jnp.concatenate([q[HALF:], q[:HALF]], axis=0)
        q = (q * cos_t + rot * sin_t) * SCALE
        for b in range(tm // MOBA_BLOCK):
            qat_ref[b, hd * HEAD_DIM:(hd + 1) * HEAD_DIM, :] = (
                q[:, b * MOBA_BLOCK:(b + 1) * MOBA_BLOCK].astype(BF16))
    for b in range(tm // MOBA_BLOCK):
        vat_ref[b] = yt[MOBA_W:, b * MOBA_BLOCK:(b + 1) * MOBA_BLOCK].astype(BF16)

    gmat = gmat_ref[...]
    lane = lax.broadcasted_iota(jnp.int32, (tm, LANES), 1)
    first_half = (lane % HEAD_DIM) < HALF
    cos_n = cos_n_ref[...]
    sin_n = sin_n_ref[...]
    pairs = DIL_W // LANES

    def emit_dilated(ref, tiles):
        for lt in range(pairs):
            ref[:, lt * LANES:(lt + 1) * LANES] = tiles[lt]

    chunks = []
    for c in range(_NAT_NORM_PAD // MXU_TILE):
        if c % 2 == 0:
            wide = _dot(h, wn_ref[:, c * MXU_TILE:(c + 2) * MXU_TILE])
        raw = wide[:, (c % 2) * MXU_TILE:(c % 2 + 1) * MXU_TILE]
        ms = _dot((raw * raw).astype(BF16), gmat)
        blk = raw * lax.rsqrt(ms + EPS) * gnat_ref[:, c * MXU_TILE:(c + 1) * MXU_TILE]
        for hc in range(MXU_TILE // LANES):
            col = c * MXU_TILE + hc * LANES
            if col >= _NAT_NORM:
                chunks.append(raw[:, hc * LANES:(hc + 1) * LANES])
                continue
            v = blk[:, hc * LANES:(hc + 1) * LANES]
            if col < _NAT_ROPE:
                rot = jnp.where(first_half, pltpu.roll(v, LANES - HALF, axis=1),
                                pltpu.roll(v, HALF, axis=1))
                v = v * cos_n + rot * sin_n
            chunks.append(v)
        if len(chunks) >= pairs and c == 1:
            ka = jnp.concatenate(chunks[0:pairs], axis=1)
            for b in range(tm // MOBA_BLOCK):
                km_ref[b] = jnp.mean(ka[b * MOBA_BLOCK:(b + 1) * MOBA_BLOCK], axis=0, keepdims=True)
            ka_ref[...] = ka.astype(BF16)
        if c == 2:
            emit_dilated(qd_ref, [t * SCALE for t in chunks[pairs:2 * pairs]])
        if c == 4:
            emit_dilated(kd_ref, chunks[2 * pairs:3 * pairs])
    qm_ref[...] = (jnp.concatenate(chunks[3 * pairs:3 * pairs + MEM_W // LANES], axis=1) * SCALE).astype(BF16)
    vd_rest = _dot(h, wn_ref[:, _NAT_NORM_PAD:])
    emit_dilated(vd_ref, [chunks[-1]] + [vd_rest[:, t * LANES:(t + 1) * LANES] for t in range(pairs - 1)])


def _proj(x, g_mix, wt, wn, gq_col, gnat, gmat, cos_t, sin_t, cos_n, sin_n, S):
    tm = TM_PROJ
    nb = S // MOBA_BLOCK
    bpt = tm // MOBA_BLOCK
    const = lambda shape: pl.BlockSpec(shape, lambda i: (0,) * len(shape))
    rows = lambda w: pl.BlockSpec((tm, w), lambda i: (i, 0))
    outs = pl.pallas_call(
        _proj_kernel,
        grid=(S // tm,),
        in_specs=[rows(D_MODEL), const((1, D_MODEL)), const((2 * MOBA_W, D_MODEL)),
                  const((D_MODEL, _NAT_COLS)), const((HEAD_DIM, 1)), const((1, _NAT_NORM_PAD)),
                  const((MXU_TILE, MXU_TILE)),
                  pl.BlockSpec((HEAD_DIM, tm), lambda i: (0, i)),
                  pl.BlockSpec((HEAD_DIM, tm), lambda i: (0, i)),
                  rows(LANES), rows(LANES)],
        out_specs=[pl.BlockSpec((bpt, MOBA_W, MOBA_BLOCK), lambda i: (i, 0, 0)),
                   pl.BlockSpec((bpt, MOBA_W, MOBA_BLOCK), lambda i: (i, 0, 0)),
                   rows(MOBA_W),
                   pl.BlockSpec((bpt, 1, MOBA_W), lambda i: (i, 0, 0)),
                   rows(MEM_W), rows(DIL_W), rows(DIL_W), rows(DIL_W)],
        out_shape=[jax.ShapeDtypeStruct((nb, MOBA_W, MOBA_BLOCK), BF16),
                   jax.ShapeDtypeStruct((nb, MOBA_W, MOBA_BLOCK), BF16),
                   jax.ShapeDtypeStruct((S, MOBA_W), BF16),
                   jax.ShapeDtypeStruct((nb, 1, MOBA_W), F32),
                   jax.ShapeDtypeStruct((S, MEM_W), BF16)]
                  + [jax.ShapeDtypeStruct((S, DIL_W), F32)] * 3,
        compiler_params=_params("parallel"),
        name="in_proj",
    )(x, g_mix, wt, wn, gq_col, gnat, gmat, cos_t, sin_t, cos_n, sin_n)
    return outs[:5], outs[5:]


_SHIFT_HEADROOM = 30.0
_FAST_BOUND = 50.0
_BOUND_SLACK = 1.02
_MOBA_UNROLL = 4
_MOBA_TRIP_GROUPS = (1, 2, 4)


def _moba_kernel(bound_ref, qt_ref, k_ref, vt_ref, km_ref, o_ref, bias_ref, ot_ref, acc_ref):
    i = pl.program_id(0)
    tq = MOBA_BLOCK
    unroll = _MOBA_UNROLL
    heads = N_MOBA_HEADS
    nb = km_ref.shape[0]
    row = lax.broadcasted_iota(jnp.int32, (LANES, tq), 0)
    blk = lax.broadcasted_iota(jnp.int32, (nb, tq), 0)

    bound = bound_ref[0]
    fast = bound <= _FAST_BOUND
    sel_bias = jnp.where(fast, _SHIFT_HEADROOM - bound, 0.0)

    def pair_cols(hd):
        return slice((hd // 2) * LANES, (hd // 2 + 1) * LANES)

    qs = []
    for hd in range(heads):
        qt = qt_ref[0, pair_cols(hd), :]
        q_h = jnp.where((row // HEAD_DIM) == hd % 2, qt, jnp.zeros_like(qt))
        qs.append(q_h)
        km = km_ref[:, pair_cols(hd)]
        km1 = km.astype(BF16)
        r1 = km - km1.astype(F32)
        km2 = r1.astype(BF16)
        km3 = (r1 - km2.astype(F32)).astype(BF16)
        g3 = _dot(jnp.concatenate([km1, km2, km3], axis=0), q_h)
        gate = g3[:nb] + g3[nb:2 * nb] + g3[2 * nb:]
        gate = jnp.where(blk < i, gate, NEG_INF)
        bias = jnp.full((nb, tq), NEG_INF, F32)
        for _ in range(MOBA_TOPK):
            m = jnp.max(gate, axis=0, keepdims=True)
            idx = jnp.min(jnp.where(gate == m, blk, nb), axis=0, keepdims=True)
            hit = blk == idx
            bias = jnp.where(hit & (m > NEG_INF), sel_bias, bias)
            gate = jnp.where(hit, NEG_INF, gate)
        bias_ref[hd] = bias

    kpos = lax.broadcasted_iota(jnp.int32, (tq, tq), 0)
    qpos = lax.broadcasted_iota(jnp.int32, (tq, tq), 1)
    causal = kpos <= qpos

    def k_rows(j, hd):
        start = j * tq if isinstance(j, int) else pl.multiple_of(j * tq, tq)
        return k_ref[pl.ds(start, tq), pair_cols(hd)]

    def v_rows(j, hd):
        return vt_ref[j, hd * HEAD_DIM:(hd + 1) * HEAD_DIM, :]

    ones = jnp.ones((BF16_ROWS, tq), BF16)

    def values(j, hd, p):
        return _dot(jnp.concatenate([v_rows(j, hd), ones], axis=0), p)

    own_s = [_dot(k_rows(i, hd), qs[hd]) for hd in range(heads)]
    own_bias = jnp.where(causal, sel_bias, NEG_INF)
    own_p = [jnp.exp(s + own_bias).astype(BF16) for s in own_s]

    @pl.when(fast)
    def _():
        def attend(accs, blocks):
            out = list(accs)
            chains = [(j, hd) for j in blocks for hd in range(heads)]
            ss = [_dot(k_rows(j, hd), qs[hd]) for j, hd in chains]
            ps = [jnp.exp(s + bias_ref[hd, pl.ds(j, 1), :]).astype(BF16) for s, (j, hd) in zip(ss, chains)]
            for p, (j, hd) in zip(ps, chains):
                out[hd] = out[hd] + values(j, hd, p)
            return out

        rem = i % unroll
        for r in range(unroll):
            @pl.when(rem == r)
            def _():
                own = [values(i, hd, p) for hd, p in enumerate(own_p)]
                for hd, a in enumerate(attend(own, range(r))):
                    acc_ref[hd] = a

        groups = i // unroll
        accs = tuple(acc_ref[hd] for hd in range(heads))
        start = rem
        for span in _MOBA_TRIP_GROUPS:
            last = span == _MOBA_TRIP_GROUPS[-1]
            trips = groups // span if last else (groups // span) % 2

            def body(t, accs, start=start, span=span):
                first = start + span * unroll * t
                return tuple(attend(accs, [first + u for u in range(span * unroll)]))

            accs = lax.fori_loop(0, trips, body, accs)
            start = start + span * unroll * trips
        for hd, a in enumerate(accs):
            ot_ref[hd * HEAD_DIM:(hd + 1) * HEAD_DIM, :] = a[:HEAD_DIM] / a[HEAD_DIM:HEAD_DIM + 1]

    @pl.when(jnp.logical_not(fast))
    def _():
        for pr in range(heads // 2):
            pair = (2 * pr, 2 * pr + 1)
            carry = []
            for hd in pair:
                s = jnp.where(causal, _dot(k_rows(i, hd), qs[hd]), NEG_INF)
                m = jnp.max(s, axis=0, keepdims=True)
                p = jnp.exp(s - m)
                l = jnp.sum(p, axis=0, keepdims=True)
                carry += [m, l, _dot(v_rows(i, hd), p.astype(BF16))]

            def body(j, carry):
                out = []
                for c, hd in enumerate(pair):
                    m, l, acc = carry[3 * c:3 * c + 3]
                    s = _dot(k_rows(j, hd), qs[hd]) + bias_ref[hd, pl.ds(j, 1), :]
                    m_new = jnp.maximum(m, jnp.max(s, axis=0, keepdims=True))
                    alpha = jnp.exp(m - m_new)
                    p = jnp.exp(s - m_new)
                    l = alpha * l + jnp.sum(p, axis=0, keepdims=True)
                    acc = alpha * acc + _dot(v_rows(j, hd), p.astype(BF16))
                    out += [m_new, l, acc]
                return tuple(out)

            carry = lax.fori_loop(0, i, body, tuple(carry))
            for c, hd in enumerate(pair):
                ot_ref[hd * HEAD_DIM:(hd + 1) * HEAD_DIM, :] = carry[3 * c + 2] / carry[3 * c + 1]

    o_ref[...] = ot_ref[...].T.astype(BF16)


def _moba(bound, qat, ka, vat, kmean, S):
    nb = S // MOBA_BLOCK
    once = lambda shape: pl.BlockSpec(shape, lambda i: (0,) * len(shape), pipeline_mode=pl.Buffered(1))
    return pl.pallas_call(
        _moba_kernel,
        grid=(nb,),
        in_specs=[pl.BlockSpec(memory_space=pltpu.SMEM),
                  pl.BlockSpec((1, MOBA_W, MOBA_BLOCK), lambda i: (i, 0, 0)),
                  once((S, MOBA_W)), once((nb, MOBA_W, MOBA_BLOCK)), once((nb, MOBA_W))],
        out_specs=pl.BlockSpec((MOBA_BLOCK, MOBA_W), lambda i: (i, 0)),
        out_shape=jax.ShapeDtypeStruct((S, MOBA_W), BF16),
        scratch_shapes=[pltpu.VMEM((N_MOBA_HEADS, nb, MOBA_BLOCK), F32),
                        pltpu.VMEM((MOBA_W, MOBA_BLOCK), F32),
                        pltpu.VMEM((N_MOBA_HEADS, HEAD_DIM + BF16_ROWS, MOBA_BLOCK), F32)],
        compiler_params=_params("arbitrary"),
        name="moba_attn",
    )(bound, qat, ka, vat, kmean)


def _dil_kernel(bound_ref, q_ref, kp_ref, kc_ref, vp_ref, vc_ref, o_ref, num_ref, den_ref, m_ref):
    g = pl.program_id(1)
    blk = DIL_BLOCK
    tile = q_ref.shape[0]
    lane = lax.broadcasted_iota(jnp.int32, (blk, LANES), 1)
    lo_half = lane < HEAD_DIM
    qi = lax.broadcasted_iota(jnp.int32, (2 * blk, 2 * blk), 0) % blk + blk
    kj = lax.broadcasted_iota(jnp.int32, (2 * blk, 2 * blk), 1)
    dist = qi - kj
    bound = bound_ref[0]
    fast = bound <= _FAST_BOUND
    shift = bound - _SHIFT_HEADROOM

    def rows(ref, start, n, d):
        return ref[pl.ds(start, n, stride=d), :].astype(BF16)

    def patterns(fixed_shift):
        for pi, (window, d) in enumerate(DIL_PATTERNS):
            span = window // d
            band = jnp.where((dist >= 0) & (dist <= span), 0.0, NEG_INF)
            band_first = jnp.where(kj >= blk, band, NEG_INF)
            sub = shift if fixed_shift else 0.0
            mask_first = jnp.where(g > 0, band, band_first) - sub
            mask_rest = band - sub
            nsub = tile // (blk * d)
            for r in range(d):
                for b in range(nsub):
                    start = r + b * blk * d
                    q = rows(q_ref, start, blk, d)
                    if b == 0:
                        last = r + (nsub - 1) * blk * d
                        kk = jnp.concatenate([rows(kp_ref, last, blk, d), rows(kc_ref, r, blk, d)], axis=0)
                        vv = jnp.concatenate([rows(vp_ref, last, blk, d), rows(vc_ref, r, blk, d)], axis=0)
                        mask = mask_first
                    else:
                        kk = rows(kc_ref, start - blk * d, 2 * blk, d)
                        vv = rows(vc_ref, start - blk * d, 2 * blk, d)
                        mask = mask_rest
                    zero = jnp.zeros_like(q)
                    q2 = jnp.concatenate([jnp.where(lo_half, q, zero), jnp.where(lo_half, zero, q)], axis=0)
                    s = _dot_nt(q2, kk) + mask
                    here = pl.ds(start, blk, stride=d)
                    if fixed_shift:
                        ov = _dot(jnp.exp(s).astype(BF16), jnp.concatenate([vv, jnp.ones_like(vv)], axis=1))
                        num = jnp.where(lo_half, ov[:blk, :LANES], ov[blk:, :LANES])
                        den = jnp.where(lo_half, ov[:blk, LANES:], ov[blk:, LANES:])
                        if pi == 0:
                            num_ref[here, :] = num
                            den_ref[here, :] = den
                        else:
                            num_ref[here, :] += num
                            den_ref[here, :] += den
                    else:
                        m2 = jnp.max(s, axis=-1, keepdims=True)
                        p = jnp.exp(s - m2)
                        den2 = jnp.broadcast_to(jnp.sum(p, axis=-1, keepdims=True), (2 * blk, LANES))
                        num2 = _dot(p.astype(BF16), vv)
                        m2 = jnp.broadcast_to(m2, (2 * blk, LANES))
                        num = jnp.where(lo_half, num2[:blk], num2[blk:])
                        den = jnp.where(lo_half, den2[:blk], den2[blk:])
                        m = jnp.where(lo_half, m2[:blk], m2[blk:])
                        if pi == 0:
                            num_ref[here, :] = num
                            den_ref[here, :] = den
                            m_ref[here, :] = m
                        else:
                            m_old = m_ref[here, :]
                            m_new = jnp.maximum(m_old, m)
                            a_old = jnp.exp(m_old - m_new)
                            a_new = jnp.exp(m - m_new)
                            num_ref[here, :] = num_ref[here, :] * a_old + num * a_new
                            den_ref[here, :] = den_ref[here, :] * a_old + den * a_new
                            m_ref[here, :] = m_new
        o_ref[...] = (num_ref[...] / den_ref[...]).astype(BF16)

    pl.when(fast)(lambda: patterns(True))
    pl.when(jnp.logical_not(fast))(lambda: patterns(False))


def _dilated(bound, qd, kd, vd, S):
    tile = DIL_TILE
    pairs = DIL_W // LANES
    cur = pl.BlockSpec((tile, LANES), lambda p, g: (g, p))
    prev = pl.BlockSpec((tile, LANES), lambda p, g: (jnp.maximum(g - 1, 0), p))
    return pl.pallas_call(
        _dil_kernel,
        grid=(pairs, S // tile),
        in_specs=[pl.BlockSpec(memory_space=pltpu.SMEM), cur, prev, cur, prev, cur],
        out_specs=cur,
        out_shape=jax.ShapeDtypeStruct((S, DIL_W), BF16),
        scratch_shapes=[pltpu.VMEM((tile, LANES), F32)] * 3,
        compiler_params=_params("parallel", "arbitrary"),
        name="dilated_attn",
    )(bound, qd, kd, kd, vd, vd)


def _merge_kernel(x_ref, g_ref, wg_ref, oa_ref, od_ref, qm_ref, mk_ref, mv_ref,
                  wba_ref, wbd_ref, wbm_ref, wo_ref, out_ref):
    x = x_ref[...]
    tm = x.shape[0]
    h = _rms_rows(x, g_ref[...]).astype(BF16)

    lane = lax.broadcasted_iota(jnp.int32, (tm, LANES), 1)
    lo_half = lane < HEAD_DIM
    mem_pairs = MEM_W // LANES
    scores = []
    for pr in range(mem_pairs):
        q = qm_ref[:, pr * LANES:(pr + 1) * LANES]
        zero = jnp.zeros_like(q)
        q2 = jnp.concatenate([jnp.where(lo_half, q, zero), jnp.where(lo_half, zero, q)], axis=0)
        scores.append(_dot_nt(q2, mk_ref[:, pr * LANES:(pr + 1) * LANES]))
    graw = [_dot(h, wg_ref[:, bi * D_MODEL:(bi + 1) * D_MODEL]) for bi in range(3)]

    probs = []
    for s in scores:
        p = jnp.exp(s - jnp.max(s, axis=-1, keepdims=True))
        probs.append((p / jnp.sum(p, axis=-1, keepdims=True)).astype(BF16))
    o_m = []
    for pr in range(mem_pairs):
        o2 = _dot(probs[pr], mv_ref[:, pr * LANES:(pr + 1) * LANES])
        o_m.append(jnp.where(lo_half, o2[:tm], o2[tm:]))
    o_m = jnp.concatenate(o_m, axis=1)

    merged = None
    branches = ((oa_ref[...], wba_ref), (od_ref[...], wbd_ref), (o_m.astype(BF16), wbm_ref))
    for bi, (o_b, w_ref) in enumerate(branches):
        term = jax.nn.sigmoid(graw[bi]) * _dot(o_b, w_ref[...])
        merged = term if merged is None else merged + term
    out_ref[...] = x + _dot(merged.astype(BF16), wo_ref[...])


def _merge(x, g_mix, w_gate, o_a, o_d, qm, mk, mv, w_ba, w_bd, w_bm, w_out, S):
    tm = TM_MERGE
    const = lambda shape: pl.BlockSpec(shape, lambda i: (0,) * len(shape), pipeline_mode=pl.Buffered(1))
    rows = lambda w: pl.BlockSpec((tm, w), lambda i: (i, 0))
    return pl.pallas_call(
        _merge_kernel,
        grid=(S // tm,),
        in_specs=[rows(D_MODEL), const((1, D_MODEL)), const((D_MODEL, 3 * D_MODEL)),
                  rows(MOBA_W), rows(DIL_W),
                  rows(MEM_W), const((N_MEM, MEM_W)), const((N_MEM, MEM_W)),
                  const((MOBA_W, D_MODEL)), const((DIL_W, D_MODEL)), const((MEM_W, D_MODEL)),
                  const((D_MODEL, D_MODEL))],
        out_specs=rows(D_MODEL),
        out_shape=jax.ShapeDtypeStruct((S, D_MODEL), F32),
        compiler_params=_params("parallel"),
        name="gated_merge",
    )(x, g_mix, w_gate, o_a, o_d, qm, mk, mv, w_ba, w_bd, w_bm, w_out)


def _ffn_kernel(x_ref, g_ref, wup_ref, cw_ref, cb_ref, wdn_ref, out_ref, halo_ref, act_ref):
    i = pl.program_id(0)
    tm = x_ref.shape[0]

    @pl.when(i == 0)
    def _():
        halo_ref[...] = jnp.zeros_like(halo_ref)

    x = x_ref[...]
    h = _rms_rows(x, g_ref[...]).astype(BF16)

    sl = SUBLANES
    row_in_tile = lax.broadcasted_iota(jnp.int32, (sl, FFN_CHUNK), 0)

    def conv(u, col):
        prev = halo_ref[:, col:col + FFN_CHUNK]
        halo_ref[:, col:col + FFN_CHUNK] = u[tm - sl:, :]
        last2 = jnp.broadcast_to(prev[sl - 2:sl - 1], (sl, FFN_CHUNK))
        last1 = jnp.broadcast_to(prev[sl - 1:sl], (sl, FFN_CHUNK))
        r1 = pltpu.roll(u, 1, axis=0)
        r2 = pltpu.roll(u, 2, axis=0)
        top1 = jnp.where(row_in_tile == 0, last1, r1[:sl])
        top2 = jnp.where(row_in_tile == 0, last2, jnp.where(row_in_tile == 1, last1, r2[:sl]))
        u1 = jnp.concatenate([top1, r1[sl:]], axis=0)
        u2 = jnp.concatenate([top2, r2[sl:]], axis=0)
        w = cw_ref[:, col:col + FFN_CHUNK]
        return cb_ref[:, col:col + FFN_CHUNK] + w[0:1] * u2 + w[1:2] * u1 + w[2:3] * u

    for c in range(D_FF // FFN_CHUNK):
        cg = c * FFN_CHUNK
        cv = D_FF + c * FFN_CHUNK
        u_g = conv(_dot(h, wup_ref[:, cg:cg + FFN_CHUNK]), cg)
        u_v = conv(_dot(h, wup_ref[:, cv:cv + FFN_CHUNK]), cv)
        act_ref[:, cg:cg + FFN_CHUNK] = (jax.nn.silu(u_g) * u_v).astype(BF16)
    out_ref[...] = x + _dot(act_ref[...], wdn_ref[...])


def _ffn(x, g_ffn, w_up, conv_w, conv_b, w_down, S):
    tm = TM_FFN
    const = lambda shape: pl.BlockSpec(shape, lambda i: (0,) * len(shape), pipeline_mode=pl.Buffered(1))
    rows = pl.BlockSpec((tm, D_MODEL), lambda i: (i, 0))
    return pl.pallas_call(
        _ffn_kernel,
        grid=(S // tm,),
        in_specs=[rows, const((1, D_MODEL)), const((D_MODEL, 2 * D_FF)),
                  const((CONV_WIDTH, 2 * D_FF)), const((1, 2 * D_FF)), const((D_FF, D_MODEL))],
        out_specs=rows,
        out_shape=jax.ShapeDtypeStruct((S, D_MODEL), F32),
        scratch_shapes=[pltpu.VMEM((SUBLANES, 2 * D_FF), F32), pltpu.VMEM((tm, D_FF), BF16)],
        compiler_params=_params("arbitrary"),
        name="conv_ffn",
    )(x, g_ffn, w_up, conv_w, conv_b, w_down)


def _group_mean_matrix():
    g = np.arange(MXU_TILE) // HEAD_DIM
    return jnp.asarray((g[:, None] == g[None, :]).astype(np.float32) / HEAD_DIM, dtype=BF16)


def _layer(x, mem, positions, p):
    S = x.shape[0]
    assert S % (max(d for _, d in DIL_PATTERNS) * DIL_BLOCK) == 0 and S % TM_PROJ == 0
    row = lambda v: v.reshape(1, -1).astype(F32)
    c = np.cumsum([0, MOBA_W, MOBA_W, MOBA_W, DIL_W, DIL_W, DIL_W, MEM_W])
    w_in = p["w_in"]
    seg = lambda k: w_in[:, c[k]:c[k + 1]]
    wt = jnp.concatenate([seg(0), seg(2)], axis=1).T.astype(BF16)
    wn = jnp.concatenate([seg(1), seg(3), seg(4), seg(6), seg(5)], axis=1).astype(BF16)
    w_gate = w_in[:, QKV_COLS:].astype(BF16)
    gnat = jnp.concatenate([jnp.tile(p["moba_k_norm_g"], N_MOBA_HEADS), jnp.tile(p["dil_q_norm_g"], N_DIL_HEADS),
                            jnp.tile(p["dil_k_norm_g"], N_DIL_HEADS), jnp.tile(p["mem_q_norm_g"], N_MEM_HEADS),
                            jnp.ones((_NAT_NORM_PAD - _NAT_NORM,), F32)]).reshape(1, _NAT_NORM_PAD)
    gmat = _group_mean_matrix()

    cos_t, sin_t, cos_n, sin_n = _rope_tables(positions, S)
    mk, mv = _mem_kv(mem, row(p["mem_norm_g"]), p["w_mem_kv"].astype(BF16),
                     row(jnp.tile(p["mem_k_norm_g"], N_MEM_HEADS)), gmat)
    (qat, vat, ka, kmean, qm), dil_qkv = _proj(
        x, row(p["mix_norm_g"]), wt, wn, p["moba_q_norm_g"].reshape(HEAD_DIM, 1).astype(F32), gnat, gmat,
        cos_t, sin_t, cos_n, sin_n, S)
    def score_bound(gq, gk):
        return (_BOUND_SLACK * HEAD_DIM * SCALE * jnp.max(jnp.abs(gq)) * jnp.max(jnp.abs(gk))
                ).reshape(1).astype(F32)

    o_a = _moba(score_bound(p["moba_q_norm_g"], p["moba_k_norm_g"]), qat, ka, vat,
                kmean.reshape(S // MOBA_BLOCK, MOBA_W), S)
    bound_d = score_bound(p["dil_q_norm_g"], p["dil_k_norm_g"])
    o_d = _dilated(bound_d, *dil_qkv, S)
    x1 = _merge(x, row(p["mix_norm_g"]), w_gate, o_a, o_d, qm, mk, mv,
                p["w_branch_moba"].astype(BF16), p["w_branch_dil"].astype(BF16),
                p["w_branch_mem"].astype(BF16), p["w_out"].astype(BF16), S)
    return _ffn(x1, row(p["ffn_norm_g"]), p["w_ffn_up"].astype(BF16), p["ffn_conv_w"].astype(F32),
                row(p["ffn_conv_b"]), p["w_ffn_down"].astype(BF16), S)


def kernel(x, mem, positions, mix_norm_g, mem_norm_g, w_in, moba_q_norm_g, moba_k_norm_g, dil_q_norm_g, dil_k_norm_g, mem_q_norm_g, mem_k_norm_g, w_mem_kv, w_branch_moba, w_branch_dil, w_branch_mem, w_out, ffn_norm_g, w_ffn_up, ffn_conv_w, ffn_conv_b, w_ffn_down):
    params = dict(mix_norm_g=mix_norm_g, mem_norm_g=mem_norm_g, w_in=w_in, moba_q_norm_g=moba_q_norm_g,
                  moba_k_norm_g=moba_k_norm_g, dil_q_norm_g=dil_q_norm_g, dil_k_norm_g=dil_k_norm_g,
                  mem_q_norm_g=mem_q_norm_g, mem_k_norm_g=mem_k_norm_g, w_mem_kv=w_mem_kv,
                  w_branch_moba=w_branch_moba, w_branch_dil=w_branch_dil, w_branch_mem=w_branch_mem,
                  w_out=w_out, ffn_norm_g=ffn_norm_g, w_ffn_up=w_ffn_up, ffn_conv_w=ffn_conv_w,
                  ffn_conv_b=ffn_conv_b, w_ffn_down=w_ffn_down)
    B = x.shape[0]
    depth = w_in.shape[0]
    outs = []
    for b in range(B):
        xb = x.reshape(x.shape[1:]) if B == 1 else x[b]
        for l in range(depth):
            xb = _layer(xb, mem[b], positions[b], {k: v[l] for k, v in params.items()})
        outs.append(xb)
    return outs[0].reshape(x.shape) if B == 1 else jnp.stack(outs, axis=0)
```

```python
import numpy as np
import jax
import jax.numpy as jnp
from jax import lax
from jax.experimental import pallas as pl
from jax.experimental.pallas import tpu as pltpu

D_MODEL = 1024
HEAD_DIM = 64
HALF = HEAD_DIM // 2
N_MOBA_HEADS = 6
N_DIL_HEADS = 6
N_MEM_HEADS = 4
N_MEM = 256
MOBA_BLOCK = 256
MOBA_TOPK = 3
DIL_PATTERNS = ((128, 1), (512, 4), (2048, 16))
DIL_BLOCK = 128
D_FF = 2816
CONV_WIDTH = 3
ROPE_THETA = 10000.0
EPS = 1e-6
MOBA_W = N_MOBA_HEADS * HEAD_DIM
DIL_W = N_DIL_HEADS * HEAD_DIM
MEM_W = N_MEM_HEADS * HEAD_DIM
QKV_COLS = 3 * MOBA_W + 3 * DIL_W + MEM_W
SCALE = HEAD_DIM ** -0.5

LANES = 128
SUBLANES = 8
BF16_ROWS = 16
MXU_TILE = 256
VMEM_LIMIT = 56 * 1024 * 1024

TM_ROPE = 2048
TM_PROJ = 512
TM_MERGE = 512
TM_FFN = 1024
FFN_CHUNK = 256
DIL_TILE = 2048

F32 = jnp.float32
BF16 = jnp.bfloat16
NEG_INF = float("-inf")


def _dot(a, b):
    return jnp.dot(a, b, preferred_element_type=F32)


def _dot_nt(a, b):
    return lax.dot_general(a, b, (((1,), (1,)), ((), ())), preferred_element_type=F32)


def _rms_rows(x, g):
    ms = jnp.mean(x * x, axis=-1, keepdims=True)
    return x * lax.rsqrt(ms + EPS) * g


def _params(*sem):
    return pltpu.CompilerParams(dimension_semantics=sem, vmem_limit_bytes=VMEM_LIMIT)


def _rope_kernel(pos_ref, inv_ref, cos_t_ref, sin_t_ref, cos_n_ref, sin_n_ref):
    pos = pos_ref[...].astype(F32)
    ang = inv_ref[...] * pos
    c = jnp.cos(ang)
    s = jnp.sin(ang)
    cos_t_ref[...] = jnp.concatenate([c, c], axis=0)
    sin_t_ref[...] = jnp.concatenate([-s, s], axis=0)
    cos_n_ref[...] = jnp.concatenate([c, c, c, c], axis=0).T
    sin_n_ref[...] = jnp.concatenate([-s, s, -s, s], axis=0).T


def _rope_tables(positions, S):
    tm = min(TM_ROPE, S)
    inv = (ROPE_THETA ** (-jnp.arange(HALF, dtype=F32) / HALF)).reshape(HALF, 1)
    return pl.pallas_call(
        _rope_kernel,
        grid=(S // tm,),
        in_specs=[pl.BlockSpec((1, tm), lambda i: (0, i)),
                  pl.BlockSpec((HALF, 1), lambda i: (0, 0))],
        out_specs=[pl.BlockSpec((HEAD_DIM, tm), lambda i: (0, i)),
                   pl.BlockSpec((HEAD_DIM, tm), lambda i: (0, i)),
                   pl.BlockSpec((tm, LANES), lambda i: (i, 0)),
                   pl.BlockSpec((tm, LANES), lambda i: (i, 0))],
        out_shape=[jax.ShapeDtypeStruct((HEAD_DIM, S), F32),
                   jax.ShapeDtypeStruct((HEAD_DIM, S), F32),
                   jax.ShapeDtypeStruct((S, LANES), F32),
                   jax.ShapeDtypeStruct((S, LANES), F32)],
        compiler_params=_params("parallel"),
        name="rope_tables",
    )(positions.reshape(1, S), inv)


def _memkv_kernel(mem_ref, g_ref, w_ref, gk_ref, gmat_ref, mk_ref, mv_ref):
    h = _rms_rows(mem_ref[...], g_ref[...]).astype(BF16)
    kv = _dot(h, w_ref[...])
    k = kv[:, :MEM_W]
    ms = _dot((k * k).astype(BF16), gmat_ref[...])
    mk_ref[...] = (k * lax.rsqrt(ms + EPS) * gk_ref[...]).astype(BF16)
    mv_ref[...] = kv[:, MEM_W:].astype(BF16)


def _mem_kv(mem, g, w_kv, gk_tiled, gmat):
    full = lambda shape: pl.BlockSpec(shape, lambda i: (0,) * len(shape))
    return pl.pallas_call(
        _memkv_kernel,
        grid=(1,),
        in_specs=[full((N_MEM, D_MODEL)), full((1, D_MODEL)), full((D_MODEL, 2 * MEM_W)),
                  full((1, MEM_W)), full((MXU_TILE, MXU_TILE))],
        out_specs=[full((N_MEM, MEM_W)), full((N_MEM, MEM_W))],
        out_shape=[jax.ShapeDtypeStruct((N_MEM, MEM_W), BF16)] * 2,
        compiler_params=_params("arbitrary"),
        name="mem_kv",
    )(mem, g, w_kv, gk_tiled, gmat)


_NAT_NORM = 3 * 384 + MEM_W
_NAT_NORM_PAD = 1536
_NAT_ROPE = 3 * 384
_NAT_COLS = _NAT_NORM + DIL_W


def _proj_kernel(x_ref, g_ref, wt_ref, wn_ref, gq_ref, gnat_ref, gmat_ref,
                 cos_t_ref, sin_t_ref, cos_n_ref, sin_n_ref,
                 qat_ref, vat_ref, ka_ref, km_ref, qm_ref, qd_ref, kd_ref, vd_ref):
    tm = x_ref.shape[0]
    h = _rms_rows(x_ref[...], g_ref[...]).astype(BF16)

    yt = _dot_nt(wt_ref[...], h)
    cos_t = cos_t_ref[...]
    sin_t = sin_t_ref[...]
    gq = gq_ref[...]
    for hd in range(N_MOBA_HEADS):
        q = yt[hd * HEAD_DIM:(hd + 1) * HEAD_DIM]
        ms = jnp.mean(q * q, axis=0, keepdims=True)
        q = q * lax.rsqrt(ms + EPS) * gq
        rot = jnp.concatenate([q[HALF:], q[:HALF]], axis=0)
        q = (q * cos_t + rot * sin_t) * SCALE
        for b in range(tm // MOBA_BLOCK):
            qat_ref[b, hd * HEAD_DIM:(hd + 1) * HEAD_DIM, :] = (
                q[:, b * MOBA_BLOCK:(b + 1) * MOBA_BLOCK].astype(BF16))
    for b in range(tm // MOBA_BLOCK):
        vat_ref[b] = yt[MOBA_W:, b * MOBA_BLOCK:(b + 1) * MOBA_BLOCK].astype(BF16)

    gmat = gmat_ref[...]
    lane = lax.broadcasted_iota(jnp.int32, (tm, LANES), 1)
    first_half = (lane % HEAD_DIM) < HALF
    cos_n = cos_n_ref[...]
    sin_n = sin_n_ref[...]
    pairs = DIL_W // LANES

    def emit_dilated(ref, tiles):
        for lt in range(pairs):
            ref[:, lt * LANES:(lt + 1) * LANES] = tiles[lt]

    chunks = []
    for c in range(_NAT_NORM_PAD // MXU_TILE):
        if c % 2 == 0:
            wide = _dot(h, wn_ref[:, c * MXU_TILE:(c + 2) * MXU_TILE])
        raw = wide[:, (c % 2) * MXU_TILE:(c % 2 + 1) * MXU_TILE]
        ms = _dot((raw * raw).astype(BF16), gmat)
        blk = raw * lax.rsqrt(ms + EPS) * gnat_ref[:, c * MXU_TILE:(c + 1) * MXU_TILE]
        for hc in range(MXU_TILE // LANES):
            col = c * MXU_TILE + hc * LANES
            if col >= _NAT_NORM:
                chunks.append(raw[:, hc * LANES:(hc + 1) * LANES])
                continue
            v = blk[:, hc * LANES:(hc + 1) * LANES]
            if col < _NAT_ROPE:
                rot = jnp.where(first_half, pltpu.roll(v, LANES - HALF, axis=1),
                                pltpu.roll(v, HALF, axis=1))
                v = v * cos_n + rot * sin_n
            chunks.append(v)
        if len(chunks) >= pairs and c == 1:
            ka = jnp.concatenate(chunks[0:pairs], axis=1)
            for b in range(tm // MOBA_BLOCK):
                km_ref[b] = jnp.mean(ka[b * MOBA_BLOCK:(b + 1) * MOBA_BLOCK], axis=0, keepdims=True)
            ka_ref[...] = ka.astype(BF16)
        if c == 2:
            emit_dilated(qd_ref, [t * SCALE for t in chunks[pairs:2 * pairs]])
        if c == 4:
            emit_dilated(kd_ref, chunks[2 * pairs:3 * pairs])
    qm_ref[...] = (jnp.concatenate(chunks[3 * pairs:3 * pairs + MEM_W // LANES], axis=1) * SCALE).astype(BF16)
    vd_rest = _dot(h, wn_ref[:, _NAT_NORM_PAD:])
    emit_dilated(vd_ref, [chunks[-1]] + [vd_rest[:, t * LANES:(t + 1) * LANES] for t in range(pairs - 1)])


def _proj(x, g_mix, wt, wn, gq_col, gnat, gmat, cos_t, sin_t, cos_n, sin_n, S):
    tm = TM_PROJ
    nb = S // MOBA_BLOCK
    bpt = tm // MOBA_BLOCK
    const = lambda shape: pl.BlockSpec(shape, lambda i: (0,) * len(shape))
    rows = lambda w: pl.BlockSpec((tm, w), lambda i: (i, 0))
    outs = pl.pallas_call(
        _proj_kernel,
        grid=(S // tm,),
        in_specs=[rows(D_MODEL), const((1, D_MODEL)), const((2 * MOBA_W, D_MODEL)),
                  const((D_MODEL, _NAT_COLS)), const((HEAD_DIM, 1)), const((1, _NAT_NORM_PAD)),
                  const((MXU_TILE, MXU_TILE)),
                  pl.BlockSpec((HEAD_DIM, tm), lambda i: (0, i)),
                  pl.BlockSpec((HEAD_DIM, tm), lambda i: (0, i)),
                  rows(LANES), rows(LANES)],
        out_specs=[pl.BlockSpec((bpt, MOBA_W, MOBA_BLOCK), lambda i: (i, 0, 0)),
                   pl.BlockSpec((bpt, MOBA_W, MOBA_BLOCK), lambda i: (i, 0, 0)),
                   rows(MOBA_W),
                   pl.BlockSpec((bpt, 1, MOBA_W), lambda i: (i, 0, 0)),
                   rows(MEM_W), rows(DIL_W), rows(DIL_W), rows(DIL_W)],
        out_shape=[jax.ShapeDtypeStruct((nb, MOBA_W, MOBA_BLOCK), BF16),
                   jax.ShapeDtypeStruct((nb, MOBA_W, MOBA_BLOCK), BF16),
                   jax.ShapeDtypeStruct((S, MOBA_W), BF16),
                   jax.ShapeDtypeStruct((nb, 1, MOBA_W), F32),
                   jax.ShapeDtypeStruct((S, MEM_W), BF16)]
                  + [jax.ShapeDtypeStruct((S, DIL_W), F32)] * 3,
        compiler_params=_params("parallel"),
        name="in_proj",
    )(x, g_mix, wt, wn, gq_col, gnat, gmat, cos_t, sin_t, cos_n, sin_n)
    return outs[:5], outs[5:]


_SHIFT_HEADROOM = 30.0
_FAST_BOUND = 50.0
_BOUND_SLACK = 1.02
_MOBA_UNROLL = 4
_MOBA_TRIP_GROUPS = (1, 2, 4)


def _moba_kernel(bound_ref, qt_ref, k_ref, vt_ref, km_ref, o_ref, bias_ref, ot_ref, acc_ref):
    i = pl.program_id(0)
    tq = MOBA_BLOCK
    unroll = _MOBA_UNROLL
    heads = N_MOBA_HEADS
    nb = km_ref.shape[0]
    row = lax.broadcasted_iota(jnp.int32, (LANES, tq), 0)
    blk = lax.broadcasted_iota(jnp.int32, (nb, tq), 0)

    bound = bound_ref[0]
    fast = bound <= _FAST_BOUND
    sel_bias = jnp.where(fast, _SHIFT_HEADROOM - bound, 0.0)

    def pair_cols(hd):
        return slice((hd // 2) * LANES, (hd // 2 + 1) * LANES)

    qs = []
    for hd in range(heads):
        qt = qt_ref[0, pair_cols(hd), :]
        q_h = jnp.where((row // HEAD_DIM) == hd % 2, qt, jnp.zeros_like(qt))
        qs.append(q_h)
        km = km_ref[:, pair_cols(hd)]
        km1 = km.astype(BF16)
        r1 = km - km1.astype(F32)
        km2 = r1.astype(BF16)
        km3 = (r1 - km2.astype(F32)).astype(BF16)
        g3 = _dot(jnp.concatenate([km1, km2, km3], axis=0), q_h)
        gate = g3[:nb] + g3[nb:2 * nb] + g3[2 * nb:]
        gate = jnp.where(blk < i, gate, NEG_INF)
        bias = jnp.full((nb, tq), NEG_INF, F32)
        for _ in range(MOBA_TOPK):
            m = jnp.max(gate, axis=0, keepdims=True)
            idx = jnp.min(jnp.where(gate == m, blk, nb), axis=0, keepdims=True)
            hit = blk == idx
            bias = jnp.where(hit & (m > NEG_INF), sel_bias, bias)
            gate = jnp.where(hit, NEG_INF, gate)
        bias_ref[hd] = bias

    kpos = lax.broadcasted_iota(jnp.int32, (tq, tq), 0)
    qpos = lax.broadcasted_iota(jnp.int32, (tq, tq), 1)
    causal = kpos <= qpos

    def k_rows(j, hd):
        start = j * tq if isinstance(j, int) else pl.multiple_of(j * tq, tq)
        return k_ref[pl.ds(start, tq), pair_cols(hd)]

    def v_rows(j, hd):
        return vt_ref[j, hd * HEAD_DIM:(hd + 1) * HEAD_DIM, :]

    ones = jnp.ones((BF16_ROWS, tq), BF16)

    def values(j, hd, p):
        return _dot(jnp.concatenate([v_rows(j, hd), ones], axis=0), p)

    own_s = [_dot(k_rows(i, hd), qs[hd]) for hd in range(heads)]
    own_bias = jnp.where(causal, sel_bias, NEG_INF)
    own_p = [jnp.exp(s + own_bias).astype(BF16) for s in own_s]

    @pl.when(fast)
    def _():
        def attend(accs, blocks):
            out = list(accs)
            chains = [(j, hd) for j in blocks for hd in range(heads)]
            ss = [_dot(k_rows(j, hd), qs[hd]) for j, hd in chains]
            ps = [jnp.exp(s + bias_ref[hd, pl.ds(j, 1), :]).astype(BF16) for s, (j, hd) in zip(ss, chains)]
            for p, (j, hd) in zip(ps, chains):
                out[hd] = out[hd] + values(j, hd, p)
            return out

        rem = i % unroll
        for r in range(unroll):
            @pl.when(rem == r)
            def _():
                own = [values(i, hd, p) for hd, p in enumerate(own_p)]
                for hd, a in enumerate(attend(own, range(r))):
                    acc_ref[hd] = a

        groups = i // unroll
        accs = tuple(acc_ref[hd] for hd in range(heads))
        start = rem
        for span in _MOBA_TRIP_GROUPS:
            last = span == _MOBA_TRIP_GROUPS[-1]
            trips = groups // span if last else (groups // span) % 2

            def body(t, accs, start=start, span=span):
                first = start + span * unroll * t
                return tuple(attend(accs, [first + u for u in range(span * unroll)]))

            accs = lax.fori_loop(0, trips, body, accs)
            start = start + span * unroll * trips
        for hd, a in enumerate(accs):
            ot_ref[hd * HEAD_DIM:(hd + 1) * HEAD_DIM, :] = a[:HEAD_DIM] / a[HEAD_DIM:HEAD_DIM + 1]

    @pl.when(jnp.logical_not(fast))
    def _():
        for pr in range(heads // 2):
            pair = (2 * pr, 2 * pr + 1)
            carry = []
            for hd in pair:
                s = jnp.where(causal, _dot(k_rows(i, hd), qs[hd]), NEG_INF)
                m = jnp.max(s, axis=0, keepdims=True)
                p = jnp.exp(s - m)
                l = jnp.sum(p, axis=0, keepdims=True)
                carry += [m, l, _dot(v_rows(i, hd), p.astype(BF16))]

            def body(j, carry):
                out = []
                for c, hd in enumerate(pair):
                    m, l, acc = carry[3 * c:3 * c + 3]
                    s = _dot(k_rows(j, hd), qs[hd]) + bias_ref[hd, pl.ds(j, 1), :]
                    m_new = jnp.maximum(m, jnp.max(s, axis=0, keepdims=True))
                    alpha = jnp.exp(m - m_new)
                    p = jnp.exp(s - m_new)
                    l = alpha * l + jnp.sum(p, axis=0, keepdims=True)
                    acc = alpha * acc + _dot(v_rows(j, hd), p.astype(BF16))
                    out += [m_new, l, acc]
                return tuple(out)

            carry = lax.fori_loop(0, i, body, tuple(carry))
            for c, hd in enumerate(pair):
                ot_ref[hd * HEAD_DIM:(hd + 1) * HEAD_DIM, :] = carry[3 * c + 2] / carry[3 * c + 1]

    o_ref[...] = ot_ref[...].T.astype(BF16)


def _moba(bound, qat, ka, vat, kmean, S):
    nb = S // MOBA_BLOCK
    once = lambda shape: pl.BlockSpec(shape, lambda i: (0,) * len(shape), pipeline_mode=pl.Buffered(1))
    return pl.pallas_call(
        _moba_kernel,
        grid=(nb,),
        in_specs=[pl.BlockSpec(memory_space=pltpu.SMEM),
                  pl.BlockSpec((1, MOBA_W, MOBA_BLOCK), lambda i: (i, 0, 0)),
                  once((S, MOBA_W)), once((nb, MOBA_W, MOBA_BLOCK)), once((nb, MOBA_W))],
        out_specs=pl.BlockSpec((MOBA_BLOCK, MOBA_W), lambda i: (i, 0)),
        out_shape=jax.ShapeDtypeStruct((S, MOBA_W), BF16),
        scratch_shapes=[pltpu.VMEM((N_MOBA_HEADS, nb, MOBA_BLOCK), F32),
                        pltpu.VMEM((MOBA_W, MOBA_BLOCK), F32),
                        pltpu.VMEM((N_MOBA_HEADS, HEAD_DIM + BF16_ROWS, MOBA_BLOCK), F32)],
        compiler_params=_params("arbitrary"),
        name="moba_attn",
    )(bound, qat, ka, vat, kmean)


def _dil_kernel(bound_ref, q_ref, kp_ref, kc_ref, vp_ref, vc_ref, o_ref, num_ref, den_ref, m_ref):
    g = pl.program_id(1)
    blk = DIL_BLOCK
    tile = q_ref.shape[0]
    lane = lax.broadcasted_iota(jnp.int32, (blk, LANES), 1)
    lo_half = lane < HEAD_DIM
    qi = lax.broadcasted_iota(jnp.int32, (2 * blk, 2 * blk), 0) % blk + blk
    kj = lax.broadcasted_iota(jnp.int32, (2 * blk, 2 * blk), 1)
    dist = qi - kj
    bound = bound_ref[0]
    fast = bound <= _FAST_BOUND
    shift = bound - _SHIFT_HEADROOM

    def rows(ref, start, n, d):
        return ref[pl.ds(start, n, stride=d), :].astype(BF16)

    def patterns(fixed_shift):
        for pi, (window, d) in enumerate(DIL_PATTERNS):
            span = window // d
            band = jnp.where((dist >= 0) & (dist <= span), 0.0, NEG_INF)
            band_first = jnp.where(kj >= blk, band, NEG_INF)
            sub = shift if fixed_shift else 0.0
            mask_first = jnp.where(g > 0, band, band_first) - sub
            mask_rest = band - sub
            nsub = tile // (blk * d)
            for r in range(d):
                for b in range(nsub):
                    start = r + b * blk * d
                    q = rows(q_ref, start, blk, d)
                    if b == 0:
                        last = r + (nsub - 1) * blk * d
                        kk = jnp.concatenate([rows(kp_ref, last, blk, d), rows(kc_ref, r, blk, d)], axis=0)
                        vv = jnp.concatenate([rows(vp_ref, last, blk, d), rows(vc_ref, r, blk, d)], axis=0)
                        mask = mask_first
                    else:
                        kk = rows(kc_ref, start - blk * d, 2 * blk, d)
                        vv = rows(vc_ref, start - blk * d, 2 * blk, d)
                        mask = mask_rest
                    zero = jnp.zeros_like(q)
                    q2 = jnp.concatenate([jnp.where(lo_half, q, zero), jnp.where(lo_half, zero, q)], axis=0)
                    s = _dot_nt(q2, kk) + mask
                    here = pl.ds(start, blk, stride=d)
                    if fixed_shift:
                        ov = _dot(jnp.exp(s).astype(BF16), jnp.concatenate([vv, jnp.ones_like(vv)], axis=1))
                        num = jnp.where(lo_half, ov[:blk, :LANES], ov[blk:, :LANES])
                        den = jnp.where(lo_half, ov[:blk, LANES:], ov[blk:, LANES:])
                        if pi == 0:
                            num_ref[here, :] = num
                            den_ref[here, :] = den
                        else:
                            num_ref[here, :] += num
                            den_ref[here, :] += den
                    else:
                        m2 = jnp.max(s, axis=-1, keepdims=True)
                        p = jnp.exp(s - m2)
                        den2 = jnp.broadcast_to(jnp.sum(p, axis=-1, keepdims=True), (2 * blk, LANES))
                        num2 = _dot(p.astype(BF16), vv)
                        m2 = jnp.broadcast_to(m2, (2 * blk, LANES))
                        num = jnp.where(lo_half, num2[:blk], num2[blk:])
                        den = jnp.where(lo_half, den2[:blk], den2[blk:])
                        m = jnp.where(lo_half, m2[:blk], m2[blk:])
                        if pi == 0:
                            num_ref[here, :] = num
                            den_ref[here, :] = den
                            m_ref[here, :] = m
                        else:
                            m_old = m_ref[here, :]
                            m_new = jnp.maximum(m_old, m)
                            a_old = jnp.exp(m_old - m_new)
                            a_new = jnp.exp(m - m_new)
                            num_ref[here, :] = num_ref[here, :] * a_old + num * a_new
                            den_ref[here, :] = den_ref[here, :] * a_old + den * a_new
                            m_ref[here, :] = m_new
        o_ref[...] = (num_ref[...] / den_ref[...]).astype(BF16)

    pl.when(fast)(lambda: patterns(True))
    pl.when(jnp.logical_not(fast))(lambda: patterns(False))


def _dilated(bound, qd, kd, vd, S):
    tile = DIL_TILE
    pairs = DIL_W // LANES
    cur = pl.BlockSpec((tile, LANES), lambda p, g: (g, p))
    prev = pl.BlockSpec((tile, LANES), lambda p, g: (jnp.maximum(g - 1, 0), p))
    return pl.pallas_call(
        _dil_kernel,
        grid=(pairs, S // tile),
        in_specs=[pl.BlockSpec(memory_space=pltpu.SMEM), cur, prev, cur, prev, cur],
        out_specs=cur,
        out_shape=jax.ShapeDtypeStruct((S, DIL_W), BF16),
        scratch_shapes=[pltpu.VMEM((tile, LANES), F32)] * 3,
        compiler_params=_params("parallel", "arbitrary"),
        name="dilated_attn",
    )(bound, qd, kd, kd, vd, vd)


def _merge_kernel(x_ref, g_ref, wg_ref, oa_ref, od_ref, qm_ref, mk_ref, mv_ref,
                  wba_ref, wbd_ref, wbm_ref, wo_ref, out_ref):
    x = x_ref[...]
    tm = x.shape[0]
    h = _rms_rows(x, g_ref[...]).astype(BF16)

    lane = lax.broadcasted_iota(jnp.int32, (tm, LANES), 1)
    lo_half = lane < HEAD_DIM
    mem_pairs = MEM_W // LANES
    scores = []
    for pr in range(mem_pairs):
        q = qm_ref[:, pr * LANES:(pr + 1) * LANES]
        zero = jnp.zeros_like(q)
        q2 = jnp.concatenate([jnp.where(lo_half, q, zero), jnp.where(lo_half, zero, q)], axis=0)
        scores.append(_dot_nt(q2, mk_ref[:, pr * LANES:(pr + 1) * LANES]))
    graw = [_dot(h, wg_ref[:, bi * D_MODEL:(bi + 1) * D_MODEL]) for bi in range(3)]

    probs = []
    for s in scores:
        p = jnp.exp(s - jnp.max(s, axis=-1, keepdims=True))
        probs.append((p / jnp.sum(p, axis=-1, keepdims=True)).astype(BF16))
    o_m = []
    for pr in range(mem_pairs):
        o2 = _dot(probs[pr], mv_ref[:, pr * LANES:(pr + 1) * LANES])
        o_m.append(jnp.where(lo_half, o2[:tm], o2[tm:]))
    o_m = jnp.concatenate(o_m, axis=1)

    merged = None
    branches = ((oa_ref[...], wba_ref), (od_ref[...], wbd_ref), (o_m.astype(BF16), wbm_ref))
    for bi, (o_b, w_ref) in enumerate(branches):
        term = jax.nn.sigmoid(graw[bi]) * _dot(o_b, w_ref[...])
        merged = term if merged is None else merged + term
    out_ref[...] = x + _dot(merged.astype(BF16), wo_ref[...])


def _merge(x, g_mix, w_gate, o_a, o_d, qm, mk, mv, w_ba, w_bd, w_bm, w_out, S):
    tm = TM_MERGE
    const = lambda shape: pl.BlockSpec(shape, lambda i: (0,) * len(shape), pipeline_mode=pl.Buffered(1))
    rows = lambda w: pl.BlockSpec((tm, w), lambda i: (i, 0))
    return pl.pallas_call(
        _merge_kernel,
        grid=(S // tm,),
        in_specs=[rows(D_MODEL), const((1, D_MODEL)), const((D_MODEL, 3 * D_MODEL)),
                  rows(MOBA_W), rows(DIL_W),
                  rows(MEM_W), const((N_MEM, MEM_W)), const((N_MEM, MEM_W)),
                  const((MOBA_W, D_MODEL)), const((DIL_W, D_MODEL)), const((MEM_W, D_MODEL)),
                  const((D_MODEL, D_MODEL))],
        out_specs=rows(D_MODEL),
        out_shape=jax.ShapeDtypeStruct((S, D_MODEL), F32),
        compiler_params=_params("parallel"),
        name="gated_merge",
    )(x, g_mix, w_gate, o_a, o_d, qm, mk, mv, w_ba, w_bd, w_bm, w_out)


def _ffn_kernel(x_ref, g_ref, wup_ref, cw_ref, cb_ref, wdn_ref, out_ref, halo_ref, act_ref):
    i = pl.program_id(0)
    tm = x_ref.shape[0]

    @pl.when(i == 0)
    def _():
        halo_ref[...] = jnp.zeros_like(halo_ref)

    x = x_ref[...]
    h = _rms_rows(x, g_ref[...]).astype(BF16)

    sl = SUBLANES
    row_in_tile = lax.broadcasted_iota(jnp.int32, (sl, FFN_CHUNK), 0)

    def conv(u, col):
        prev = halo_ref[:, col:col + FFN_CHUNK]
        halo_ref[:, col:col + FFN_CHUNK] = u[tm - sl:, :]
        last2 = jnp.broadcast_to(prev[sl - 2:sl - 1], (sl, FFN_CHUNK))
        last1 = jnp.broadcast_to(prev[sl - 1:sl], (sl, FFN_CHUNK))
        r1 = pltpu.roll(u, 1, axis=0)
        r2 = pltpu.roll(u, 2, axis=0)
        top1 = jnp.where(row_in_tile == 0, last1, r1[:sl])
        top2 = jnp.where(row_in_tile == 0, last2, jnp.where(row_in_tile == 1, last1, r2[:sl]))
        u1 = jnp.concatenate([top1, r1[sl:]], axis=0)
        u2 = jnp.concatenate([top2, r2[sl:]], axis=0)
        w = cw_ref[:, col:col + FFN_CHUNK]
        return cb_ref[:, col:col + FFN_CHUNK] + w[0:1] * u2 + w[1:2] * u1 + w[2:3] * u

    for c in range(D_FF // FFN_CHUNK):
        cg = c * FFN_CHUNK
        cv = D_FF + c * FFN_CHUNK
        u_g = conv(_dot(h, wup_ref[:, cg:cg + FFN_CHUNK]), cg)
        u_v = conv(_dot(h, wup_ref[:, cv:cv + FFN_CHUNK]), cv)
        act_ref[:, cg:cg + FFN_CHUNK] = (jax.nn.silu(u_g) * u_v).astype(BF16)
    out_ref[...] = x + _dot(act_ref[...], wdn_ref[...])


def _ffn(x, g_ffn, w_up, conv_w, conv_b, w_down, S):
    tm = TM_FFN
    const = lambda shape: pl.BlockSpec(shape, lambda i: (0,) * len(shape), pipeline_mode=pl.Buffered(1))
    rows = pl.BlockSpec((tm, D_MODEL), lambda i: (i, 0))
    return pl.pallas_call(
        _ffn_kernel,
        grid=(S // tm,),
        in_specs=[rows, const((1, D_MODEL)), const((D_MODEL, 2 * D_FF)),
                  const((CONV_WIDTH, 2 * D_FF)), const((1, 2 * D_FF)), const((D_FF, D_MODEL))],
        out_specs=rows,
        out_shape=jax.ShapeDtypeStruct((S, D_MODEL), F32),
        scratch_shapes=[pltpu.VMEM((SUBLANES, 2 * D_FF), F32), pltpu.VMEM((tm, D_FF), BF16)],
        compiler_params=_params("arbitrary"),
        name="conv_ffn",
    )(x, g_ffn, w_up, conv_w, conv_b, w_down)


def _group_mean_matrix():
    g = np.arange(MXU_TILE) // HEAD_DIM
    return jnp.asarray((g[:, None] == g[None, :]).astype(np.float32) / HEAD_DIM, dtype=BF16)


def _layer(x, mem, positions, p):
    S = x.shape[0]
    assert S % (max(d for _, d in DIL_PATTERNS) * DIL_BLOCK) == 0 and S % TM_PROJ == 0
    row = lambda v: v.reshape(1, -1).astype(F32)
    c = np.cumsum([0, MOBA_W, MOBA_W, MOBA_W, DIL_W, DIL_W, DIL_W, MEM_W])
    w_in = p["w_in"]
    seg = lambda k: w_in[:, c[k]:c[k + 1]]
    wt = jnp.concatenate([seg(0), seg(2)], axis=1).T.astype(BF16)
    wn = jnp.concatenate([seg(1), seg(3), seg(4), seg(6), seg(5)], axis=1).astype(BF16)
    w_gate = w_in[:, QKV_COLS:].astype(BF16)
    gnat = jnp.concatenate([jnp.tile(p["moba_k_norm_g"], N_MOBA_HEADS), jnp.tile(p["dil_q_norm_g"], N_DIL_HEADS),
                            jnp.tile(p["dil_k_norm_g"], N_DIL_HEADS), jnp.tile(p["mem_q_norm_g"], N_MEM_HEADS),
                            jnp.ones((_NAT_NORM_PAD - _NAT_NORM,), F32)]).reshape(1, _NAT_NORM_PAD)
    gmat = _group_mean_matrix()

    cos_t, sin_t, cos_n, sin_n = _rope_tables(positions, S)
    mk, mv = _mem_kv(mem, row(p["mem_norm_g"]), p["w_mem_kv"].astype(BF16),
                     row(jnp.tile(p["mem_k_norm_g"], N_MEM_HEADS)), gmat)
    (qat, vat, ka, kmean, qm), dil_qkv = _proj(
        x, row(p["mix_norm_g"]), wt, wn, p["moba_q_norm_g"].reshape(HEAD_DIM, 1).astype(F32), gnat, gmat,
        cos_t, sin_t, cos_n, sin_n, S)
    def score_bound(gq, gk):
        return (_BOUND_SLACK * HEAD_DIM * SCALE * jnp.max(jnp.abs(gq)) * jnp.max(jnp.abs(gk))
                ).reshape(1).astype(F32)

    o_a = _moba(score_bound(p["moba_q_norm_g"], p["moba_k_norm_g"]), qat, ka, vat,
                kmean.reshape(S // MOBA_BLOCK, MOBA_W), S)
    bound_d = score_bound(p["dil_q_norm_g"], p["dil_k_norm_g"])
    o_d = _dilated(bound_d, *dil_qkv, S)
    x1 = _merge(x, row(p["mix_norm_g"]), w_gate, o_a, o_d, qm, mk, mv,
                p["w_branch_moba"].astype(BF16), p["w_branch_dil"].astype(BF16),
                p["w_branch_mem"].astype(BF16), p["w_out"].astype(BF16), S)
    return _ffn(x1, row(p["ffn_norm_g"]), p["w_ffn_up"].astype(BF16), p["ffn_conv_w"].astype(F32),
                row(p["ffn_conv_b"]), p["w_ffn_down"].astype(BF16), S)


def kernel(x, mem, positions, mix_norm_g, mem_norm_g, w_in, moba_q_norm_g, moba_k_norm_g, dil_q_norm_g, dil_k_norm_g, mem_q_norm_g, mem_k_norm_g, w_mem_kv, w_branch_moba, w_branch_dil, w_branch_mem, w_out, ffn_norm_g, w_ffn_up, ffn_conv_w, ffn_conv_b, w_ffn_down):
    params = dict(mix_norm_g=mix_norm_g, mem_norm_g=mem_norm_g, w_in=w_in, moba_q_norm_g=moba_q_norm_g,
                  moba_k_norm_g=moba_k_norm_g, dil_q_norm_g=dil_q_norm_g, dil_k_norm_g=dil_k_norm_g,
                  mem_q_norm_g=mem_q_norm_g, mem_k_norm_g=mem_k_norm_g, w_mem_kv=w_mem_kv,
                  w_branch_moba=w_branch_moba, w_branch_dil=w_branch_dil, w_branch_mem=w_branch_mem,
                  w_out=w_out, ffn_norm_g=ffn_norm_g, w_ffn_up=w_ffn_up, ffn_conv_w=ffn_conv_w,
                  ffn_conv_b=ffn_conv_b, w_ffn_down=w_ffn_down)
    B = x.shape[0]
    depth = w_in.shape[0]
    outs = []
    for b in range(B):
        xb = x.reshape(x.shape[1:]) if B == 1 else x[b]
        for l in range(depth):
            xb = _layer(xb, mem[b], positions[b], {k: v[l] for k, v in params.items()})
        outs.append(xb)
    return outs[0].reshape(x.shape) if B == 1 else jnp.stack(outs, axis=0)
```

```python
import numpy as np
import jax
import jax.numpy as jnp
from jax import lax
from jax.experimental import pallas as pl
from jax.experimental.pallas import tpu as pltpu

D_MODEL = 1024
HEAD_DIM = 64
HALF = HEAD_DIM // 2
N_MOBA_HEADS = 6
N_DIL_HEADS = 6
N_MEM_HEADS = 4
N_MEM = 256
MOBA_BLOCK = 256
MOBA_TOPK = 3
DIL_PATTERNS = ((128, 1), (512, 4), (2048, 16))
DIL_BLOCK = 128
D_FF = 2816
CONV_WIDTH = 3
ROPE_THETA = 10000.0
EPS = 1e-6
MOBA_W = N_MOBA_HEADS * HEAD_DIM
DIL_W = N_DIL_HEADS * HEAD_DIM
MEM_W = N_MEM_HEADS * HEAD_DIM
QKV_COLS = 3 * MOBA_W + 3 * DIL_W + MEM_W
SCALE = HEAD_DIM ** -0.5

LANES = 128
SUBLANES = 8
BF16_ROWS = 16
MXU_TILE = 256
VMEM_LIMIT = 56 * 1024 * 1024

TM_PROJ = 512
TM_MERGE = 512
TM_FFN = 1024
FFN_CHUNK = 256
DIL_TILE = 2048

F32 = jnp.float32
BF16 = jnp.bfloat16
NEG_INF = float("-inf")


def _dot(a, b):
    return jnp.dot(a, b, preferred_element_type=F32)


def _dot_nt(a, b):
    return lax.dot_general(a, b, (((1,), (1,)), ((), ())), preferred_element_type=F32)


def _rms_rows(x, g):
    ms = jnp.mean(x * x, axis=-1, keepdims=True)
    return x * lax.rsqrt(ms + EPS) * g


def _params(*sem):
    return pltpu.CompilerParams(dimension_semantics=sem, vmem_limit_bytes=VMEM_LIMIT)


def _rope_tables(pos_ref, inv_ref):
    pos = pos_ref[...].astype(F32)
    ang = inv_ref[...] * pos
    c = jnp.cos(ang)
    s = jnp.sin(ang)
    return (jnp.concatenate([c, c], axis=0), jnp.concatenate([-s, s], axis=0),
            jnp.concatenate([c, c, c, c], axis=0).T, jnp.concatenate([-s, s, -s, s], axis=0).T)


def _memkv_kernel(mem_ref, g_ref, w_ref, gk_ref, gmat_ref, mk_ref, mv_ref):
    h = _rms_rows(mem_ref[...], g_ref[...]).astype(BF16)
    kv = _dot(h, w_ref[...])
    k = kv[:, :MEM_W]
    ms = _dot((k * k).astype(BF16), gmat_ref[...])
    mk_ref[...] = (k * lax.rsqrt(ms + EPS) * gk_ref[...]).astype(BF16)
    mv_ref[...] = kv[:, MEM_W:].astype(BF16)


def _mem_kv(mem, g, w_kv, gk_tiled, gmat):
    full = lambda shape: pl.BlockSpec(shape, lambda i: (0,) * len(shape))
    return pl.pallas_call(
        _memkv_kernel,
        grid=(1,),
        in_specs=[full((N_MEM, D_MODEL)), full((1, D_MODEL)), full((D_MODEL, 2 * MEM_W)),
                  full((1, MEM_W)), full((MXU_TILE, MXU_TILE))],
        out_specs=[full((N_MEM, MEM_W)), full((N_MEM, MEM_W))],
        out_shape=[jax.ShapeDtypeStruct((N_MEM, MEM_W), BF16)] * 2,
        compiler_params=_params("arbitrary"),
        name="mem_kv",
    )(mem, g, w_kv, gk_tiled, gmat)


_NAT_NORM = 3 * 384 + MEM_W
_NAT_NORM_PAD = 1536
_NAT_ROPE = 3 * 384
_NAT_COLS = _NAT_NORM + DIL_W


def _proj_kernel(x_ref, g_ref, wt_ref, wn_ref, gq_ref, gnat_ref, gmat_ref,
                 pos_ref, inv_ref,
                 qat_ref, vat_ref, ka_ref, km_ref, qm_ref, qd_ref, kd_ref, vd_ref):
    tm = x_ref.shape[0]
    h = _rms_rows(x_ref[...], g_ref[...]).astype(BF16)

    yt = _dot_nt(wt_ref[...], h)
    cos_t, sin_t, cos_n, sin_n = _rope_tables(pos_ref, inv_ref)
    gq = gq_ref[...]
    for hd in range(N_MOBA_HEADS):
        q = yt[hd * HEAD_DIM:(hd + 1) * HEAD_DIM]
        ms = jnp.mean(q * q, axis=0, keepdims=True)
        q = q * lax.rsqrt(ms + EPS) * gq
        rot = jnp.concatenate([q[HALF:], q[:HALF]], axis=0)
        q = (q * cos_t + rot * sin_t) * SCALE
        for b in range(tm // MOBA_BLOCK):
            qat_ref[b, hd * HEAD_DIM:(hd + 1) * HEAD_DIM, :] = (
                q[:, b * MOBA_BLOCK:(b + 1) * MOBA_BLOCK].astype(BF16))
    for b in range(tm // MOBA_BLOCK):
        vat_ref[b] = yt[MOBA_W:, b * MOBA_BLOCK:(b + 1) * MOBA_BLOCK].astype(BF16)

    gmat = gmat_ref[...]
    lane = lax.broadcasted_iota(jnp.int32, (tm, LANES), 1)
    first_half = (lane % HEAD_DIM) < HALF
    pairs = DIL_W // LANES

    def emit_dilated(ref, tiles):
        for lt in range(pairs):
            ref[:, lt * LANES:(lt + 1) * LANES] = tiles[lt]

    chunks = []
    for c in range(_NAT_NORM_PAD // MXU_TILE):
        if c % 2 == 0:
            wide = _dot(h, wn_ref[:, c * MXU_TILE:(c + 2) * MXU_TILE])
        raw = wide[:, (c % 2) * MXU_TILE:(c % 2 + 1) * MXU_TILE]
        ms = _dot((raw * raw).astype(BF16), gmat)
        blk = raw * lax.rsqrt(ms + EPS) * gnat_ref[:, c * MXU_TILE:(c + 1) * MXU_TILE]
        for hc in range(MXU_TILE // LANES):
            col = c * MXU_TILE + hc * LANES
            if col >= _NAT_NORM:
                chunks.append(raw[:, hc * LANES:(hc + 1) * LANES])
                continue
            v = blk[:, hc * LANES:(hc + 1) * LANES]
            if col < _NAT_ROPE:
                rot = jnp.where(first_half, pltpu.roll(v, LANES - HALF, axis=1),
                                pltpu.roll(v, HALF, axis=1))
                v = v * cos_n + rot * sin_n
            chunks.append(v)
        if len(chunks) >= pairs and c == 1:
            ka = jnp.concatenate(chunks[0:pairs], axis=1)
            for b in range(tm // MOBA_BLOCK):
                km_ref[b] = jnp.mean(ka[b * MOBA_BLOCK:(b + 1) * MOBA_BLOCK], axis=0, keepdims=True)
            ka_ref[...] = ka.astype(BF16)
        if c == 2:
            emit_dilated(qd_ref, [t * SCALE for t in chunks[pairs:2 * pairs]])
        if c == 4:
            emit_dilated(kd_ref, chunks[2 * pairs:3 * pairs])
    qm_ref[...] = (jnp.concatenate(chunks[3 * pairs:3 * pairs + MEM_W // LANES], axis=1) * SCALE).astype(BF16)
    vd_rest = _dot(h, wn_ref[:, _NAT_NORM_PAD:])
    emit_dilated(vd_ref, [chunks[-1]] + [vd_rest[:, t * LANES:(t + 1) * LANES] for t in range(pairs - 1)])


def _proj(x, g_mix, wt, wn, gq_col, gnat, gmat, positions, S):
    tm = TM_PROJ
    nb = S // MOBA_BLOCK
    bpt = tm // MOBA_BLOCK
    inv = (ROPE_THETA ** (-jnp.arange(HALF, dtype=F32) / HALF)).reshape(HALF, 1)
    const = lambda shape: pl.BlockSpec(shape, lambda i: (0,) * len(shape))
    rows = lambda w: pl.BlockSpec((tm, w), lambda i: (i, 0))
    outs = pl.pallas_call(
        _proj_kernel,
        grid=(S // tm,),
        in_specs=[rows(D_MODEL), const((1, D_MODEL)), const((2 * MOBA_W, D_MODEL)),
                  const((D_MODEL, _NAT_COLS)), const((HEAD_DIM, 1)), const((1, _NAT_NORM_PAD)),
                  const((MXU_TILE, MXU_TILE)),
                  pl.BlockSpec((1, tm), lambda i: (0, i)), const((HALF, 1))],
        out_specs=[pl.BlockSpec((bpt, MOBA_W, MOBA_BLOCK), lambda i: (i, 0, 0)),
                   pl.BlockSpec((bpt, MOBA_W, MOBA_BLOCK), lambda i: (i, 0, 0)),
                   rows(MOBA_W),
                   pl.BlockSpec((bpt, 1, MOBA_W), lambda i: (i, 0, 0)),
                   rows(MEM_W), rows(DIL_W), rows(DIL_W), rows(DIL_W)],
        out_shape=[jax.ShapeDtypeStruct((nb, MOBA_W, MOBA_BLOCK), BF16),
                   jax.ShapeDtypeStruct((nb, MOBA_W, MOBA_BLOCK), BF16),
                   jax.ShapeDtypeStruct((S, MOBA_W), BF16),
                   jax.ShapeDtypeStruct((nb, 1, MOBA_W), F32),
                   jax.ShapeDtypeStruct((S, MEM_W), BF16)]
                  + [jax.ShapeDtypeStruct((S, DIL_W), F32)] * 3,
        compiler_params=_params("parallel"),
        name="in_proj",
    )(x, g_mix, wt, wn, gq_col, gnat, gmat, positions.reshape(1, S), inv)
    return outs[:5], outs[5:]


_SHIFT_HEADROOM = 30.0
_FAST_BOUND = 50.0
_BOUND_SLACK = 1.02
_MOBA_UNROLL = 4
_MOBA_TRIP_GROUPS = (1, 2, 4)


def _moba_kernel(bound_ref, qt_ref, k_ref, vt_ref, km_ref, o_ref, bias_ref, ot_ref, acc_ref):
    i = pl.program_id(0)
    tq = MOBA_BLOCK
    unroll = _MOBA_UNROLL
    heads = N_MOBA_HEADS
    nb = km_ref.shape[0]
    row = lax.broadcasted_iota(jnp.int32, (LANES, tq), 0)
    blk = lax.broadcasted_iota(jnp.int32, (nb, tq), 0)

    bound = bound_ref[0]
    fast = bound <= _FAST_BOUND
    sel_bias = jnp.where(fast, _SHIFT_HEADROOM - bound, 0.0)

    def pair_cols(hd):
        return slice((hd // 2) * LANES, (hd // 2 + 1) * LANES)

    qs = []
    for hd in range(heads):
        qt = qt_ref[0, pair_cols(hd), :]
        q_h = jnp.where((row // HEAD_DIM) == hd % 2, qt, jnp.zeros_like(qt))
        qs.append(q_h)
        km = km_ref[:, pair_cols(hd)]
        km1 = km.astype(BF16)
        r1 = km - km1.astype(F32)
        km2 = r1.astype(BF16)
        km3 = (r1 - km2.astype(F32)).astype(BF16)
        g3 = _dot(jnp.concatenate([km1, km2, km3], axis=0), q_h)
        gate = g3[:nb] + g3[nb:2 * nb] + g3[2 * nb:]
        gate = jnp.where(blk < i, gate, NEG_INF)
        bias = jnp.full((nb, tq), NEG_INF, F32)
        for _ in range(MOBA_TOPK):
            m = jnp.max(gate, axis=0, keepdims=True)
            idx = jnp.min(jnp.where(gate == m, blk, nb), axis=0, keepdims=True)
            hit = blk == idx
            bias = jnp.where(hit & (m > NEG_INF), sel_bias, bias)
            gate = jnp.where(hit, NEG_INF, gate)
        bias_ref[hd] = bias

    kpos = lax.broadcasted_iota(jnp.int32, (tq, tq), 0)
    qpos = lax.broadcasted_iota(jnp.int32, (tq, tq), 1)
    causal = kpos <= qpos

    def k_rows(j, hd):
        start = j * tq if isinstance(j, int) else pl.multiple_of(j * tq, tq)
        return k_ref[pl.ds(start, tq), pair_cols(hd)]

    def v_rows(j, hd):
        return vt_ref[j, hd * HEAD_DIM:(hd + 1) * HEAD_DIM, :]

    ones = jnp.ones((BF16_ROWS, tq), BF16)

    def values(j, hd, p):
        return _dot(jnp.concatenate([v_rows(j, hd), ones], axis=0), p)

    own_s = [_dot(k_rows(i, hd), qs[hd]) for hd in range(heads)]
    own_bias = jnp.where(causal, sel_bias, NEG_INF)
    own_p = [jnp.exp(s + own_bias).astype(BF16) for s in own_s]

    @pl.when(fast)
    def _():
        def attend(accs, blocks):
            out = list(accs)
            chains = [(j, hd) for j in blocks for hd in range(heads)]
            ss = [_dot(k_rows(j, hd), qs[hd]) for j, hd in chains]
            ps = [jnp.exp(s + bias_ref[hd, pl.ds(j, 1), :]).astype(BF16) for s, (j, hd) in zip(ss, chains)]
            for p, (j, hd) in zip(ps, chains):
                out[hd] = out[hd] + values(j, hd, p)
            return out

        rem = i % unroll
        for r in range(unroll):
            @pl.when(rem == r)
            def _():
                own = [values(i, hd, p) for hd, p in enumerate(own_p)]
                for hd, a in enumerate(attend(own, range(r))):
                    acc_ref[hd] = a

        groups = i // unroll
        accs = tuple(acc_ref[hd] for hd in range(heads))
        start = rem
        for span in _MOBA_TRIP_GROUPS:
            last = span == _MOBA_TRIP_GROUPS[-1]
            trips = groups // span if last else (groups // span) % 2

            def body(t, accs, start=start, span=span):
                first = start + span * unroll * t
                return tuple(attend(accs, [first + u for u in range(span * unroll)]))

            accs = lax.fori_loop(0, trips, body, accs)
            start = start + span * unroll * trips
        for hd, a in enumerate(accs):
            ot_ref[hd * HEAD_DIM:(hd + 1) * HEAD_DIM, :] = a[:HEAD_DIM] / a[HEAD_DIM:HEAD_DIM + 1]

    @pl.when(jnp.logical_not(fast))
    def _():
        for pr in range(heads // 2):
            pair = (2 * pr, 2 * pr + 1)
            carry = []
            for hd in pair:
                s = jnp.where(causal, _dot(k_rows(i, hd), qs[hd]), NEG_INF)
                m = jnp.max(s, axis=0, keepdims=True)
                p = jnp.exp(s - m)
                l = jnp.sum(p, axis=0, keepdims=True)
                carry += [m, l, _dot(v_rows(i, hd), p.astype(BF16))]

            def body(j, carry):
                out = []
                for c, hd in enumerate(pair):
                    m, l, acc = carry[3 * c:3 * c + 3]
                    s = _dot(k_rows(j, hd), qs[hd]) + bias_ref[hd, pl.ds(j, 1), :]
                    m_new = jnp.maximum(m, jnp.max(s, axis=0, keepdims=True))
                    alpha = jnp.exp(m - m_new)
                    p = jnp.exp(s - m_new)
                    l = alpha * l + jnp.sum(p, axis=0, keepdims=True)
                    acc = alpha * acc + _dot(v_rows(j, hd), p.astype(BF16))
                    out += [m_new, l, acc]
                return tuple(out)

            carry = lax.fori_loop(0, i, body, tuple(carry))
            for c, hd in enumerate(pair):
                ot_ref[hd * HEAD_DIM:(hd + 1) * HEAD_DIM, :] = carry[3 * c + 2] / carry[3 * c + 1]

    o_ref[...] = ot_ref[...].T.astype(BF16)


def _moba(bound, qat, ka, vat, kmean, S):
    nb = S // MOBA_BLOCK
    once = lambda shape: pl.BlockSpec(shape, lambda i: (0,) * len(shape), pipeline_mode=pl.Buffered(1))
    return pl.pallas_call(
        _moba_kernel,
        grid=(nb,),
        in_specs=[pl.BlockSpec(memory_space=pltpu.SMEM),
                  pl.BlockSpec((1, MOBA_W, MOBA_BLOCK), lambda i: (i, 0, 0)),
                  once((S, MOBA_W)), once((nb, MOBA_W, MOBA_BLOCK)), once((nb, MOBA_W))],
        out_specs=pl.BlockSpec((MOBA_BLOCK, MOBA_W), lambda i: (i, 0)),
        out_shape=jax.ShapeDtypeStruct((S, MOBA_W), BF16),
        scratch_shapes=[pltpu.VMEM((N_MOBA_HEADS, nb, MOBA_BLOCK), F32),
                        pltpu.VMEM((MOBA_W, MOBA_BLOCK), F32),
                        pltpu.VMEM((N_MOBA_HEADS, HEAD_DIM + BF16_ROWS, MOBA_BLOCK), F32)],
        compiler_params=_params("arbitrary"),
        name="moba_attn",
    )(bound, qat, ka, vat, kmean)


def _dil_kernel(bound_ref, q_ref, kp_ref, kc_ref, vp_ref, vc_ref, o_ref, num_ref, den_ref, m_ref):
    g = pl.program_id(1)
    blk = DIL_BLOCK
    tile = q_ref.shape[0]
    lane = lax.broadcasted_iota(jnp.int32, (blk, LANES), 1)
    lo_half = lane < HEAD_DIM
    qi = lax.broadcasted_iota(jnp.int32, (2 * blk, 2 * blk), 0) % blk + blk
    kj = lax.broadcasted_iota(jnp.int32, (2 * blk, 2 * blk), 1)
    dist = qi - kj
    bound = bound_ref[0]
    fast = bound <= _FAST_BOUND
    shift = bound - _SHIFT_HEADROOM

    def rows(ref, start, n, d):
        return ref[pl.ds(start, n, stride=d), :].astype(BF16)

    def patterns(fixed_shift):
        for pi, (window, d) in enumerate(DIL_PATTERNS):
            span = window // d
            band = jnp.where((dist >= 0) & (dist <= span), 0.0, NEG_INF)
            band_first = jnp.where(kj >= blk, band, NEG_INF)
            sub = shift if fixed_shift else 0.0
            mask_first = jnp.where(g > 0, band, band_first) - sub
            mask_rest = band - sub
            nsub = tile // (blk * d)
            for r in range(d):
                for b in range(nsub):
                    start = r + b * blk * d
                    q = rows(q_ref, start, blk, d)
                    if b == 0:
                        last = r + (nsub - 1) * blk * d
                        kk = jnp.concatenate([rows(kp_ref, last, blk, d), rows(kc_ref, r, blk, d)], axis=0)
                        vv = jnp.concatenate([rows(vp_ref, last, blk, d), rows(vc_ref, r, blk, d)], axis=0)
                        mask = mask_first
                    else:
                        kk = rows(kc_ref, start - blk * d, 2 * blk, d)
                        vv = rows(vc_ref, start - blk * d, 2 * blk, d)
                        mask = mask_rest
                    zero = jnp.zeros_like(q)
                    q2 = jnp.concatenate([jnp.where(lo_half, q, zero), jnp.where(lo_half, zero, q)], axis=0)
                    s = _dot_nt(q2, kk) + mask
                    here = pl.ds(start, blk, stride=d)
                    if fixed_shift:
                        ov = _dot(jnp.exp(s).astype(BF16), jnp.concatenate([vv, jnp.ones_like(vv)], axis=1))
                        num = jnp.where(lo_half, ov[:blk, :LANES], ov[blk:, :LANES])
                        den = jnp.where(lo_half, ov[:blk, LANES:], ov[blk:, LANES:])
                        if pi == 0:
                            num_ref[here, :] = num
                            den_ref[here, :] = den
                        else:
                            num_ref[here, :] += num
                            den_ref[here, :] += den
                    else:
                        m2 = jnp.max(s, axis=-1, keepdims=True)
                        p = jnp.exp(s - m2)
                        den2 = jnp.broadcast_to(jnp.sum(p, axis=-1, keepdims=True), (2 * blk, LANES))
                        num2 = _dot(p.astype(BF16), vv)
                        m2 = jnp.broadcast_to(m2, (2 * blk, LANES))
                        num = jnp.where(lo_half, num2[:blk], num2[blk:])
                        den = jnp.where(lo_half, den2[:blk], den2[blk:])
                        m = jnp.where(lo_half, m2[:blk], m2[blk:])
                        if pi == 0:
                            num_ref[here, :] = num
                            den_ref[here, :] = den
                            m_ref[here, :] = m
                        else:
                            m_old = m_ref[here, :]
                            m_new = jnp.maximum(m_old, m)
                            a_old = jnp.exp(m_old - m_new)
                            a_new = jnp.exp(m - m_new)
                            num_ref[here, :] = num_ref[here, :] * a_old + num * a_new
                            den_ref[here, :] = den_ref[here, :] * a_old + den * a_new
                            m_ref[here, :] = m_new
        o_ref[...] = (num_ref[...] / den_ref[...]).astype(BF16)

    pl.when(fast)(lambda: patterns(True))
    pl.when(jnp.logical_not(fast))(lambda: patterns(False))


def _dilated(bound, qd, kd, vd, S):
    tile = DIL_TILE
    pairs = DIL_W // LANES
    cur = pl.BlockSpec((tile, LANES), lambda p, g: (g, p))
    prev = pl.BlockSpec((tile, LANES), lambda p, g: (jnp.maximum(g - 1, 0), p))
    return pl.pallas_call(
        _dil_kernel,
        grid=(pairs, S // tile),
        in_specs=[pl.BlockSpec(memory_space=pltpu.SMEM), cur, prev, cur, prev, cur],
        out_specs=cur,
        out_shape=jax.ShapeDtypeStruct((S, DIL_W), BF16),
        scratch_shapes=[pltpu.VMEM((tile, LANES), F32)] * 3,
        compiler_params=_params("parallel", "arbitrary"),
        name="dilated_attn",
    )(bound, qd, kd, kd, vd, vd)


def _merge_kernel(x_ref, g_ref, wg_ref, oa_ref, od_ref, qm_ref, mk_ref, mv_ref,
                  wba_ref, wbd_ref, wbm_ref, wo_ref, out_ref):
    x = x_ref[...]
    tm = x.shape[0]
    h = _rms_rows(x, g_ref[...]).astype(BF16)

    lane = lax.broadcasted_iota(jnp.int32, (tm, LANES), 1)
    lo_half = lane < HEAD_DIM
    mem_pairs = MEM_W // LANES
    scores = []
    for pr in range(mem_pairs):
        q = qm_ref[:, pr * LANES:(pr + 1) * LANES]
        zero = jnp.zeros_like(q)
        q2 = jnp.concatenate([jnp.where(lo_half, q, zero), jnp.where(lo_half, zero, q)], axis=0)
        scores.append(_dot_nt(q2, mk_ref[:, pr * LANES:(pr + 1) * LANES]))
    graw = [_dot(h, wg_ref[:, bi * D_MODEL:(bi + 1) * D_MODEL]) for bi in range(3)]

    probs = []
    for s in scores:
        p = jnp.exp(s - jnp.max(s, axis=-1, keepdims=True))
        probs.append((p / jnp.sum(p, axis=-1, keepdims=True)).astype(BF16))
    o_m = []
    for pr in range(mem_pairs):
        o2 = _dot(probs[pr], mv_ref[:, pr * LANES:(pr + 1) * LANES])
        o_m.append(jnp.where(lo_half, o2[:tm], o2[tm:]))
    o_m = jnp.concatenate(o_m, axis=1)

    merged = None
    branches = ((oa_ref[...], wba_ref), (od_ref[...], wbd_ref), (o_m.astype(BF16), wbm_ref))
    for bi, (o_b, w_ref) in enumerate(branches):
        term = jax.nn.sigmoid(graw[bi]) * _dot(o_b, w_ref[...])
        merged = term if merged is None else merged + term
    out_ref[...] = x + _dot(merged.astype(BF16), wo_ref[...])


def _merge(x, g_mix, w_gate, o_a, o_d, qm, mk, mv, w_ba, w_bd, w_bm, w_out, S):
    tm = TM_MERGE
    const = lambda shape: pl.BlockSpec(shape, lambda i: (0,) * len(shape), pipeline_mode=pl.Buffered(1))
    rows = lambda w: pl.BlockSpec((tm, w), lambda i: (i, 0))
    return pl.pallas_call(
        _merge_kernel,
        grid=(S // tm,),
        in_specs=[rows(D_MODEL), const((1, D_MODEL)), const((D_MODEL, 3 * D_MODEL)),
                  rows(MOBA_W), rows(DIL_W),
                  rows(MEM_W), const((N_MEM, MEM_W)), const((N_MEM, MEM_W)),
                  const((MOBA_W, D_MODEL)), const((DIL_W, D_MODEL)), const((MEM_W, D_MODEL)),
                  const((D_MODEL, D_MODEL))],
        out_specs=rows(D_MODEL),
        out_shape=jax.ShapeDtypeStruct((S, D_MODEL), F32),
        compiler_params=_params("parallel"),
        name="gated_merge",
    )(x, g_mix, w_gate, o_a, o_d, qm, mk, mv, w_ba, w_bd, w_bm, w_out)


def _ffn_kernel(x_ref, g_ref, wup_ref, cw_ref, cb_ref, wdn_ref, out_ref, halo_ref, act_ref):
    i = pl.program_id(0)
    tm = x_ref.shape[0]

    @pl.when(i == 0)
    def _():
        halo_ref[...] = jnp.zeros_like(halo_ref)

    x = x_ref[...]
    h = _rms_rows(x, g_ref[...]).astype(BF16)

    sl = SUBLANES
    row_in_tile = lax.broadcasted_iota(jnp.int32, (sl, FFN_CHUNK), 0)

    def conv(u, col):
        prev = halo_ref[:, col:col + FFN_CHUNK]
        halo_ref[:, col:col + FFN_CHUNK] = u[tm - sl:, :]
        last2 = jnp.broadcast_to(prev[sl - 2:sl - 1], (sl, FFN_CHUNK))
        last1 = jnp.broadcast_to(prev[sl - 1:sl], (sl, FFN_CHUNK))
        r1 = pltpu.roll(u, 1, axis=0)
        r2 = pltpu.roll(u, 2, axis=0)
        top1 = jnp.where(row_in_tile == 0, last1, r1[:sl])
        top2 = jnp.where(row_in_tile == 0, last2, jnp.where(row_in_tile == 1, last1, r2[:sl]))
        u1 = jnp.concatenate([top1, r1[sl:]], axis=0)
        u2 = jnp.concatenate([top2, r2[sl:]], axis=0)
        w = cw_ref[:, col:col + FFN_CHUNK]
        return cb_ref[:, col:col + FFN_CHUNK] + w[0:1] * u2 + w[1:2] * u1 + w[2:3] * u

    for c in range(D_FF // FFN_CHUNK):
        cg = c * FFN_CHUNK
        cv = D_FF + c * FFN_CHUNK
        u_g = conv(_dot(h, wup_ref[:, cg:cg + FFN_CHUNK]), cg)
        u_v = conv(_dot(h, wup_ref[:, cv:cv + FFN_CHUNK]), cv)
        act_ref[:, cg:cg + FFN_CHUNK] = (jax.nn.silu(u_g) * u_v).astype(BF16)
    out_ref[...] = x + _dot(act_ref[...], wdn_ref[...])


def _ffn(x, g_ffn, w_up, conv_w, conv_b, w_down, S):
    tm = TM_FFN
    const = lambda shape: pl.BlockSpec(shape, lambda i: (0,) * len(shape), pipeline_mode=pl.Buffered(1))
    rows = pl.BlockSpec((tm, D_MODEL), lambda i: (i, 0))
    return pl.pallas_call(
        _ffn_kernel,
        grid=(S // tm,),
        in_specs=[rows, const((1, D_MODEL)), const((D_MODEL, 2 * D_FF)),
                  const((CONV_WIDTH, 2 * D_FF)), const((1, 2 * D_FF)), const((D_FF, D_MODEL))],
        out_specs=rows,
        out_shape=jax.ShapeDtypeStruct((S, D_MODEL), F32),
        scratch_shapes=[pltpu.VMEM((SUBLANES, 2 * D_FF), F32), pltpu.VMEM((tm, D_FF), BF16)],
        compiler_params=_params("arbitrary"),
        name="conv_ffn",
    )(x, g_ffn, w_up, conv_w, conv_b, w_down)


def _group_mean_matrix():
    g = np.arange(MXU_TILE) // HEAD_DIM
    return jnp.asarray((g[:, None] == g[None, :]).astype(np.float32) / HEAD_DIM, dtype=BF16)


def _layer(x, mem, positions, p):
    S = x.shape[0]
    assert S % (max(d for _, d in DIL_PATTERNS) * DIL_BLOCK) == 0 and S % TM_PROJ == 0
    row = lambda v: v.reshape(1, -1).astype(F32)
    c = np.cumsum([0, MOBA_W, MOBA_W, MOBA_W, DIL_W, DIL_W, DIL_W, MEM_W])
    w_in = p["w_in"]
    seg = lambda k: w_in[:, c[k]:c[k + 1]]
    wt = jnp.concatenate([seg(0), seg(2)], axis=1).T.astype(BF16)
    wn = jnp.concatenate([seg(1), seg(3), seg(4), seg(6), seg(5)], axis=1).astype(BF16)
    w_gate = w_in[:, QKV_COLS:].astype(BF16)
    gnat = jnp.concatenate([jnp.tile(p["moba_k_norm_g"], N_MOBA_HEADS), jnp.tile(p["dil_q_norm_g"], N_DIL_HEADS),
                            jnp.tile(p["dil_k_norm_g"], N_DIL_HEADS), jnp.tile(p["mem_q_norm_g"], N_MEM_HEADS),
                            jnp.ones((_NAT_NORM_PAD - _NAT_NORM,), F32)]).reshape(1, _NAT_NORM_PAD)
    gmat = _group_mean_matrix()

    mk, mv = _mem_kv(mem, row(p["mem_norm_g"]), p["w_mem_kv"].astype(BF16),
                     row(jnp.tile(p["mem_k_norm_g"], N_MEM_HEADS)), gmat)
    (qat, vat, ka, kmean, qm), dil_qkv = _proj(
        x, row(p["mix_norm_g"]), wt, wn, p["moba_q_norm_g"].reshape(HEAD_DIM, 1).astype(F32), gnat, gmat,
        positions, S)
    def score_bound(gq, gk):
        return (_BOUND_SLACK * HEAD_DIM * SCALE * jnp.max(jnp.abs(gq)) * jnp.max(jnp.abs(gk))
                ).reshape(1).astype(F32)

    o_a = _moba(score_bound(p["moba_q_norm_g"], p["moba_k_norm_g"]), qat, ka, vat,
                kmean.reshape(S // MOBA_BLOCK, MOBA_W), S)
    bound_d = score_bound(p["dil_q_norm_g"], p["dil_k_norm_g"])
    o_d = _dilated(bound_d, *dil_qkv, S)
    x1 = _merge(x, row(p["mix_norm_g"]), w_gate, o_a, o_d, qm, mk, mv,
                p["w_branch_moba"].astype(BF16), p["w_branch_dil"].astype(BF16),
                p["w_branch_mem"].astype(BF16), p["w_out"].astype(BF16), S)
    return _ffn(x1, row(p["ffn_norm_g"]), p["w_ffn_up"].astype(BF16), p["ffn_conv_w"].astype(F32),
                row(p["ffn_conv_b"]), p["w_ffn_down"].astype(BF16), S)


def kernel(x, mem, positions, mix_norm_g, mem_norm_g, w_in, moba_q_norm_g, moba_k_norm_g, dil_q_norm_g, dil_k_norm_g, mem_q_norm_g, mem_k_norm_g, w_mem_kv, w_branch_moba, w_branch_dil, w_branch_mem, w_out, ffn_norm_g, w_ffn_up, ffn_conv_w, ffn_conv_b, w_ffn_down):
    params = dict(mix_norm_g=mix_norm_g, mem_norm_g=mem_norm_g, w_in=w_in, moba_q_norm_g=moba_q_norm_g,
                  moba_k_norm_g=moba_k_norm_g, dil_q_norm_g=dil_q_norm_g, dil_k_norm_g=dil_k_norm_g,
                  mem_q_norm_g=mem_q_norm_g, mem_k_norm_g=mem_k_norm_g, w_mem_kv=w_mem_kv,
                  w_branch_moba=w_branch_moba, w_branch_dil=w_branch_dil, w_branch_mem=w_branch_mem,
                  w_out=w_out, ffn_norm_g=ffn_norm_g, w_ffn_up=w_ffn_up, ffn_conv_w=ffn_conv_w,
                  ffn_conv_b=ffn_conv_b, w_ffn_down=w_ffn_down)
    B = x.shape[0]
    depth = w_in.shape[0]
    outs = []
    for b in range(B):
        xb = x.reshape(x.shape[1:]) if B == 1 else x[b]
        for l in range(depth):
            xb = _layer(xb, mem[b], positions[b], {k: v[l] for k, v in params.items()})
        outs.append(xb)
    return outs[0].reshape(x.shape) if B == 1 else jnp.stack(outs, axis=0)
```
